```python
import jax
import jax.numpy as jnp
from jax import lax
import numpy as np


D_MODEL = 2048
BATCH = 2
SEQ = 4096
DEPTH = 4

N_META = 16
BLOCK = 128
PAD = BLOCK - N_META
EPS = 1e-6
NEG_INF = -1e30
ROPE_BASE = 10000.0
HALF_STEP = 0.5

SWA_HEADS = 16
SWA_KV_HEADS = 2
SWA_HEAD_DIM = 64
SWA_GROUP = SWA_HEADS // SWA_KV_HEADS
WINDOW = 128

MLA_HEADS = 8
MLA_Q_RANK = 512
MLA_KV_RANK = 512
MLA_NOPE_DIM = 128
MLA_ROPE_DIM = 64
MLA_V_DIM = 128

RET_HEADS = 4
RET_KEY_DIM = 128
RET_VAL_DIM = 256

D_FF = 5632
N_BRANCH = 3

SWA_OUT = SWA_HEADS * SWA_HEAD_DIM
SWA_KV_W = SWA_KV_HEADS * SWA_HEAD_DIM
MLA_OUT = MLA_HEADS * MLA_V_DIM
RET_QK_W = RET_HEADS * RET_KEY_DIM
RET_OUT = RET_HEADS * RET_VAL_DIM
IN_WIDTHS = (SWA_OUT, SWA_KV_W, SWA_KV_W, MLA_Q_RANK, MLA_KV_RANK, MLA_ROPE_DIM,
             RET_QK_W, RET_QK_W, RET_OUT, RET_OUT, N_BRANCH * D_MODEL)
IN_WIDTH = sum(IN_WIDTHS)

kernel_name = 'hybrid_swa_mla_retention_macaron'


def _rms_norm(x, g):
    xf = x.astype(jnp.float32)
    y = xf * lax.rsqrt(jnp.mean(xf * xf, axis=-1, keepdims=True) + EPS)
    return (y * g.astype(jnp.float32)).astype(x.dtype)


def _rope(x, pos):
    half = x.shape[-1] // 2
    inv_freq = ROPE_BASE ** (-jnp.arange(half, dtype=jnp.float32) / half)
    ang = pos[:, None] * inv_freq[None, :]
    cos = jnp.cos(ang)[None, :, None, :]
    sin = jnp.sin(ang)[None, :, None, :]
    xf = x.astype(jnp.float32)
    x1, x2 = xf[..., :half], xf[..., half:]
    return jnp.concatenate([x1 * cos - x2 * sin, x2 * cos + x1 * sin], axis=-1).astype(x.dtype)


def _swiglu(x, w_gate, w_up, w_down):
    return (jax.nn.silu(x @ w_gate) * (x @ w_up)) @ w_down


def _sliding_window_attention(q, k, v, sinks):
    B, L = q.shape[0], q.shape[1]
    nb = L // BLOCK
    qb = q.reshape(B, nb, BLOCK, SWA_KV_HEADS, SWA_GROUP, SWA_HEAD_DIM)
    kb = k.reshape(B, nb, BLOCK, SWA_KV_HEADS, SWA_HEAD_DIM)
    vb = v.reshape(B, nb, BLOCK, SWA_KV_HEADS, SWA_HEAD_DIM)
    shift = ((0, 0), (1, 0), (0, 0), (0, 0), (0, 0))
    k_prev = jnp.pad(kb[:, :-1], shift)
    v_prev = jnp.pad(vb[:, :-1], shift)
    meta_shape = (B, nb, N_META, SWA_KV_HEADS, SWA_HEAD_DIM)
    k_meta = jnp.broadcast_to(k[:, None, PAD:BLOCK], meta_shape)
    v_meta = jnp.broadcast_to(v[:, None, PAD:BLOCK], meta_shape)
    keys = jnp.concatenate([k_meta, k_prev, kb], axis=2)
    vals = jnp.concatenate([v_meta, v_prev, vb], axis=2)
    s = jnp.einsum('bnqhgd,bnkhd->bnhgqk', qb, keys).astype(jnp.float32) * (SWA_HEAD_DIM ** -0.5)
    q_pos = jnp.arange(nb * BLOCK).reshape(nb, BLOCK)
    k_pos = jnp.arange(nb)[:, None] * BLOCK - BLOCK + jnp.arange(2 * BLOCK)[None, :]
    meta_pos = PAD + jnp.arange(N_META)
    diff_w = q_pos[:, :, None] - k_pos[:, None, :]
    win_ok = (diff_w >= 0) & (diff_w < WINDOW) & (k_pos[:, None, :] >= PAD)
    meta_ok = (q_pos[:, :, None] - meta_pos[None, None, :]) >= WINDOW
    mask = jnp.concatenate([meta_ok, win_ok], axis=-1)
    s = jnp.where(mask[None, :, None, None], s, NEG_INF)
    sink = jnp.broadcast_to(sinks.astype(jnp.float32).reshape(1, 1, SWA_KV_HEADS, SWA_GROUP, 1, 1),
                            s.shape[:-1] + (1,))
    p = jax.nn.softmax(jnp.concatenate([s, sink], axis=-1), axis=-1)[..., :-1]
    o = jnp.einsum('bnhgqk,bnkhd->bnqhgd', p.astype(v.dtype), vals)
    return o.reshape(B, L, SWA_OUT)


def _mla(c_q, c_kv, k_rope, q_a_norm, w_uq, kv_a_norm, w_ukv, qn_norm, qr_norm, kn_norm, kr_norm, pos):
    B, L = c_q.shape[0], c_q.shape[1]
    nb = L // BLOCK
    q = (_rms_norm(c_q, q_a_norm) @ w_uq).reshape(B, L, MLA_HEADS, MLA_NOPE_DIM + MLA_ROPE_DIM)
    kv = (_rms_norm(c_kv, kv_a_norm) @ w_ukv).reshape(B, L, MLA_HEADS, MLA_NOPE_DIM + MLA_V_DIM)
    q_nope = _rms_norm(q[..., :MLA_NOPE_DIM], qn_norm)
    q_rope = _rope(_rms_norm(q[..., MLA_NOPE_DIM:], qr_norm), pos)
    k_nope = _rms_norm(kv[..., :MLA_NOPE_DIM], kn_norm)
    v = kv[..., MLA_NOPE_DIM:]
    k_r = _rope(_rms_norm(k_rope, kr_norm)[:, :, None, :], pos)[:, :, 0, :]
    scale = (MLA_NOPE_DIM + MLA_ROPE_DIM) ** -0.5
    k_idx = jnp.arange(L)
    qn_b = q_nope.reshape(B, nb, BLOCK, MLA_HEADS, MLA_NOPE_DIM).transpose(1, 0, 2, 3, 4)
    qr_b = q_rope.reshape(B, nb, BLOCK, MLA_HEADS, MLA_ROPE_DIM).transpose(1, 0, 2, 3, 4)
    qi_b = k_idx.reshape(nb, BLOCK)

    def q_block(args):
        qn, qr, qi = args
        s = (jnp.einsum('bqhd,bkhd->bhqk', qn, k_nope)
             + jnp.einsum('bqhd,bkd->bhqk', qr, k_r)).astype(jnp.float32) * scale
        ok = (k_idx[None, :] <= qi[:, None]) & (k_idx[None, :] >= PAD)
        p = jax.nn.softmax(jnp.where(ok[None, None], s, NEG_INF), axis=-1)
        return jnp.einsum('bhqk,bkhd->bqhd', p.astype(v.dtype), v)

    o = lax.map(q_block, (qn_b, qr_b, qi_b))
    return o.transpose(1, 0, 2, 3, 4).reshape(B, L, MLA_OUT)


def _retention(q, k, v):
    B, L, H, dk = q.shape
    dv = v.shape[-1]
    nc = L // BLOCK
    log_gamma = jnp.log(1.0 - 2.0 ** (-5.0 - jnp.arange(H, dtype=jnp.float32)))
    idx = jnp.arange(BLOCK, dtype=jnp.float32)
    qc = q.reshape(B, nc, BLOCK, H, dk)
    kc = k.reshape(B, nc, BLOCK, H, dk)
    vc = v.reshape(B, nc, BLOCK, H, dv)
    diff = idx[:, None] - idx[None, :]
    decay = jnp.where(diff[None] >= 0,
                      jnp.exp(jnp.maximum(diff, 0.0)[None] * log_gamma[:, None, None]), 0.0)
    s = jnp.einsum('bnqhd,bnkhd->bnhqk', qc, kc) * decay
    inner = jnp.einsum('bnhqk,bnkhe->bnqhe', s, vc)
    zeta = jnp.exp((BLOCK - 1.0 - idx)[None, :] * log_gamma[:, None])
    kv_chunk = jnp.einsum('bnkhd,hk,bnkhe->nbhde', kc, zeta, vc)
    chunk_decay = jnp.exp(BLOCK * log_gamma)[None, :, None, None]

    def step(state, kv):
        return state * chunk_decay + kv, state

    _, prev = lax.scan(step, jnp.zeros((B, H, dk, dv), jnp.float32), kv_chunk)
    xi = jnp.exp((idx + 1.0)[None, :] * log_gamma[:, None])
    cross = jnp.einsum('bnqhd,nbhde,hq->bnqhe', qc, prev, xi)
    return (inner + cross).reshape(B, L, H, dv)


def _group_norm(o, g):
    B, L = o.shape[0], o.shape[1]
    mu = jnp.mean(o, axis=-1, keepdims=True)
    var = jnp.mean(jnp.square(o - mu), axis=-1, keepdims=True)
    y = ((o - mu) * lax.rsqrt(var + EPS)).reshape(B, L, -1)
    return y * g.astype(jnp.float32)


def _hybrid_mixer(hn, pos, w_in, swa_q_norm, swa_k_norm, swa_sinks,
                  mla_q_a_norm, mla_w_uq, mla_kv_a_norm, mla_w_ukv,
                  mla_qn_norm, mla_qr_norm, mla_kn_norm, mla_kr_norm,
                  ret_gn, w_br_swa, w_br_mla, w_br_ret, w_o):
    B, L, _ = hn.shape
    z = hn @ w_in
    splits = np.cumsum(np.array(IN_WIDTHS))[:-1].tolist()
    (swa_q, swa_k, swa_v, mla_cq, mla_ckv, mla_kr,
     ret_q, ret_k, ret_v, ret_g, gate_pre) = jnp.split(z, splits, axis=-1)

    qa = _rms_norm(swa_q.reshape(B, L, SWA_HEADS, SWA_HEAD_DIM), swa_q_norm)
    ka = _rms_norm(swa_k.reshape(B, L, SWA_KV_HEADS, SWA_HEAD_DIM), swa_k_norm)
    va = swa_v.reshape(B, L, SWA_KV_HEADS, SWA_HEAD_DIM)
    o_a = _sliding_window_attention(qa, ka, va, swa_sinks)

    o_b = _mla(mla_cq, mla_ckv, mla_kr, mla_q_a_norm, mla_w_uq, mla_kv_a_norm, mla_w_ukv,
               mla_qn_norm, mla_qr_norm, mla_kn_norm, mla_kr_norm, pos)

    valid = (jnp.arange(L) >= PAD).astype(jnp.float32)[None, :, None, None]
    qc = _rope(ret_q.reshape(B, L, RET_HEADS, RET_KEY_DIM), pos).astype(jnp.float32)
    kc = _rope(ret_k.reshape(B, L, RET_HEADS, RET_KEY_DIM), pos).astype(jnp.float32) * (RET_KEY_DIM ** -0.5) * valid
    vc = ret_v.reshape(B, L, RET_HEADS, RET_VAL_DIM).astype(jnp.float32)
    o_c = (_group_norm(_retention(qc, kc, vc), ret_gn) * jax.nn.silu(ret_g.astype(jnp.float32))).astype(hn.dtype)

    g = jax.nn.sigmoid(gate_pre.astype(jnp.float32)).astype(hn.dtype).reshape(B, L, N_BRANCH, D_MODEL)
    merged = (g[:, :, 0] * (o_a @ w_br_swa) + g[:, :, 1] * (o_b @ w_br_mla)
              + g[:, :, 2] * (o_c @ w_br_ret))
    return merged @ w_o


def setup_inputs(seed: int = 0) -> dict:
    key = jax.random.key(seed)
    k = jax.random.split(key, 32)
    f32 = jnp.float32

    def w(i, shape, fan_in):
        return jax.random.normal(k[i], shape, f32) * (fan_in ** -0.5)

    def gain(i, shape):
        return 1.0 + 0.02 * jax.random.normal(k[i], shape, f32)

    return {
        'x': jax.random.normal(k[0], (BATCH, SEQ, D_MODEL), f32),
        'meta_tokens': jax.random.normal(k[1], (N_META, D_MODEL), f32),
        'ffn1_norm': gain(2, (DEPTH, D_MODEL)),
        'ffn1_w_gate': w(3, (DEPTH, D_MODEL, D_FF), D_MODEL),
        'ffn1_w_up': w(4, (DEPTH, D_MODEL, D_FF), D_MODEL),
        'ffn1_w_down': w(5, (DEPTH, D_FF, D_MODEL), D_FF),
        'mix_norm': gain(6, (DEPTH, D_MODEL)),
        'w_in': w(7, (DEPTH, D_MODEL, IN_WIDTH), D_MODEL),
        'swa_q_norm': gain(8, (DEPTH, SWA_HEAD_DIM)),
        'swa_k_norm': gain(9, (DEPTH, SWA_HEAD_DIM)),
        'swa_sinks': 0.5 * jax.random.normal(k[10], (DEPTH, SWA_HEADS), f32),
        'mla_q_a_norm': gain(11, (DEPTH, MLA_Q_RANK)),
        'mla_w_uq': w(12, (DEPTH, MLA_Q_RANK, MLA_HEADS * (MLA_NOPE_DIM + MLA_ROPE_DIM)), MLA_Q_RANK),
        'mla_kv_a_norm': gain(13, (DEPTH, MLA_KV_RANK)),
        'mla_w_ukv': w(14, (DEPTH, MLA_KV_RANK, MLA_HEADS * (MLA_NOPE_DIM + MLA_V_DIM)), MLA_KV_RANK),
        'mla_qn_norm': gain(15, (DEPTH, MLA_NOPE_DIM)),
        'mla_qr_norm': gain(16, (DEPTH, MLA_ROPE_DIM)),
        'mla_kn_norm': gain(17, (DEPTH, MLA_NOPE_DIM)),
        'mla_kr_norm': gain(18, (DEPTH, MLA_ROPE_DIM)),
        'ret_gn': gain(19, (DEPTH, RET_OUT)),
        'w_br_swa': w(20, (DEPTH, SWA_OUT, D_MODEL), SWA_OUT),
        'w_br_mla': w(21, (DEPTH, MLA_OUT, D_MODEL), MLA_OUT),
        'w_br_ret': w(22, (DEPTH, RET_OUT, D_MODEL), RET_OUT),
        'w_o': w(23, (DEPTH, D_MODEL, D_MODEL), D_MODEL),
        'ffn2_norm': gain(24, (DEPTH, D_MODEL)),
        'ffn2_w_gate': w(25, (DEPTH, D_MODEL, D_FF), D_MODEL),
        'ffn2_w_up': w(26, (DEPTH, D_MODEL, D_FF), D_MODEL),
        'ffn2_w_down': w(27, (DEPTH, D_FF, D_MODEL), D_FF),
    }


def reference(x, meta_tokens, ffn1_norm, ffn1_w_gate, ffn1_w_up, ffn1_w_down, mix_norm, w_in,
              swa_q_norm, swa_k_norm, swa_sinks, mla_q_a_norm, mla_w_uq, mla_kv_a_norm, mla_w_ukv,
              mla_qn_norm, mla_qr_norm, mla_kn_norm, mla_kr_norm, ret_gn, w_br_swa, w_br_mla,
              w_br_ret, w_o, ffn2_norm, ffn2_w_gate, ffn2_w_up, ffn2_w_down):
    B = x.shape[0]
    meta = jnp.broadcast_to(meta_tokens[None].astype(x.dtype), (B, N_META, D_MODEL))
    pad = jnp.zeros((B, PAD, D_MODEL), x.dtype)
    h = jnp.concatenate([pad, meta, x], axis=1)
    L = h.shape[1]
    pos = (jnp.arange(L) - PAD).astype(jnp.float32)
    for l in range(DEPTH):
        h = h + HALF_STEP * _swiglu(_rms_norm(h, ffn1_norm[l]), ffn1_w_gate[l], ffn1_w_up[l], ffn1_w_down[l])
        h = h + _hybrid_mixer(_rms_norm(h, mix_norm[l]), pos, w_in[l], swa_q_norm[l], swa_k_norm[l],
                              swa_sinks[l], mla_q_a_norm[l], mla_w_uq[l], mla_kv_a_norm[l], mla_w_ukv[l],
                              mla_qn_norm[l], mla_qr_norm[l], mla_kn_norm[l], mla_kr_norm[l], ret_gn[l],
                              w_br_swa[l], w_br_mla[l], w_br_ret[l], w_o[l])
        h = h + HALF_STEP * _swiglu(_rms_norm(h, ffn2_norm[l]), ffn2_w_gate[l], ffn2_w_up[l], ffn2_w_down[l])
    return h[:, BLOCK:]
```

```python
import functools

import jax
import jax.numpy as jnp
import numpy as np
from jax import lax
from jax.experimental import pallas as pl
from jax.experimental.pallas import tpu as pltpu

F32 = jnp.float32
BF16 = jnp.bfloat16

D_MODEL = 2048
DEPTH = 4
N_META = 16
BLOCK = 128
PAD = BLOCK - N_META
EPS = 1e-6
NEG_INF = -1e30
ROPE_BASE = 10000.0
HALF_STEP = 0.5

SWA_HEADS = 16
SWA_KV_HEADS = 2
SWA_HEAD_DIM = 64
SWA_GROUP = SWA_HEADS // SWA_KV_HEADS
WINDOW = 128

MLA_HEADS = 8
MLA_Q_RANK = 512
MLA_KV_RANK = 512
MLA_NOPE_DIM = 128
MLA_ROPE_DIM = 64
MLA_V_DIM = 128
MLA_QK_PAD = 256

RET_HEADS = 4
RET_KEY_DIM = 128
RET_VAL_DIM = 256

D_FF = 5632
N_BRANCH = 3

SWA_OUT = SWA_HEADS * SWA_HEAD_DIM
SWA_KV_W = SWA_KV_HEADS * SWA_HEAD_DIM
MLA_OUT = MLA_HEADS * MLA_V_DIM
RET_QK_W = RET_HEADS * RET_KEY_DIM
RET_OUT = RET_HEADS * RET_VAL_DIM

Z_GATE = 0
Z_RET_V = Z_GATE + N_BRANCH * D_MODEL
Z_RET_G = Z_RET_V + RET_OUT
Z_SWA_Q = Z_RET_G + RET_OUT
Z_MLA_CQ = Z_SWA_Q + SWA_OUT
Z_MLA_CKV = Z_MLA_CQ + MLA_Q_RANK
Z_RET_Q = Z_MLA_CKV + MLA_KV_RANK
Z_RET_K = Z_RET_Q + RET_QK_W
Z_SWA_K = Z_RET_K + RET_QK_W
Z_SWA_V = Z_SWA_K + SWA_KV_W
Z_MLA_KR = Z_SWA_V + SWA_KV_W
Z_WIDTH = Z_MLA_KR + 2 * MLA_ROPE_DIM

VMEM_LIMIT = 56 * 1024 * 1024


def _params(semantics):
    return pltpu.CompilerParams(dimension_semantics=semantics, vmem_limit_bytes=VMEM_LIMIT)


def _rms(x, g):
    return x * lax.rsqrt(jnp.mean(x * x, axis=-1, keepdims=True) + EPS) * g


def _sigmoid(x):
    return 1.0 / (1.0 + jnp.exp(-x))


def _dot(a, b):
    return jnp.dot(a, b, preferred_element_type=F32)


def _dot_nt(a, b):
    return lax.dot_general(a, b, (((1,), (1,)), ((), ())), preferred_element_type=F32)


def _ffn_body(x_ref, g_ref, wg_ref, wu_ref, wd_ref, o_ref, xn_ref):
    @pl.when(pl.program_id(1) == 0)
    def _():
        x = x_ref[...]
        xn_ref[...] = _rms(x, g_ref[...]).astype(BF16)
        o_ref[...] = x

    xn = xn_ref[...]
    gate = _dot(xn, wg_ref[...].astype(BF16))
    up = _dot(xn, wu_ref[...].astype(BF16))
    act = (gate * _sigmoid(gate) * up).astype(BF16)
    o_ref[...] += HALF_STEP * _dot(act, wd_ref[...].astype(BF16))


def _ffn(h, gain, w_gate, w_up, w_down, layer, *, tm=1056, tf=256):
    m = h.shape[0]
    return pl.pallas_call(
        _ffn_body,
        grid=(m // tm, D_FF // tf),
        in_specs=[
            pl.BlockSpec((tm, D_MODEL), lambda i, f: (i, 0)),
            pl.BlockSpec((None, 1, D_MODEL), lambda i, f: (layer, 0, 0)),
            pl.BlockSpec((None, D_MODEL, tf), lambda i, f: (layer, 0, f)),
            pl.BlockSpec((None, D_MODEL, tf), lambda i, f: (layer, 0, f)),
            pl.BlockSpec((None, tf, D_MODEL), lambda i, f: (layer, f, 0)),
        ],
        out_specs=pl.BlockSpec((tm, D_MODEL), lambda i, f: (i, 0)),
        out_shape=jax.ShapeDtypeStruct((m, D_MODEL), F32),
        scratch_shapes=[pltpu.VMEM((tm, D_MODEL), BF16)],
        compiler_params=_params(("parallel", "arbitrary")),
        name="ffn",
    )(h, gain.reshape(DEPTH, 1, D_MODEL), w_gate, w_up, w_down)


def _inproj_body(x_ref, g_ref, w_ref, o_ref, xn_ref):
    @pl.when(pl.program_id(1) == 0)
    def _():
        xn_ref[...] = _rms(x_ref[...], g_ref[...]).astype(BF16)

    o_ref[...] = _dot(xn_ref[...], w_ref[...])


def _inproj(h, gain, w_z, layer, *, tm=1056, tn=896):
    m = h.shape[0]
    return pl.pallas_call(
        _inproj_body,
        grid=(m // tm, Z_WIDTH // tn),
        in_specs=[
            pl.BlockSpec((tm, D_MODEL), lambda i, j: (i, 0)),
            pl.BlockSpec((None, 1, D_MODEL), lambda i, j: (layer, 0, 0)),
            pl.BlockSpec((None, D_MODEL, tn), lambda i, j: (layer, 0, j)),
        ],
        out_specs=pl.BlockSpec((tm, tn), lambda i, j: (i, j)),
        out_shape=jax.ShapeDtypeStruct((m, Z_WIDTH), F32),
        scratch_shapes=[pltpu.VMEM((tm, D_MODEL), BF16)],
        compiler_params=_params(("parallel", "arbitrary")),
        name="inproj",
    )(h, gain.reshape(DEPTH, 1, D_MODEL), w_z)


def _swa_body(q_ref, kc_ref, kp_ref, km_ref, vc_ref, vp_ref, vm_ref, qn_ref, kn_ref, sink_ref, o_ref):
    n = pl.program_id(1)
    rows = SWA_GROUP * BLOCK
    i_loc = lax.broadcasted_iota(jnp.int32, (rows, BLOCK), 0) & (BLOCK - 1)
    j = lax.broadcasted_iota(jnp.int32, (rows, BLOCK), 1)
    q_pos = n * BLOCK + i_loc
    ok_cur = (j <= i_loc) & (n * BLOCK + j >= PAD)
    ok_prev = (j > i_loc) & ((n - 1) * BLOCK + j >= PAD)
    ok_meta = (j >= PAD) & (q_pos - j >= WINDOW)

    q = q_ref[...]
    qn = qn_ref[...]
    kn = kn_ref[...]
    outs = []
    for g in range(SWA_KV_HEADS):
        lo = g * SWA_HEAD_DIM
        hi = lo + SWA_HEAD_DIM
        qs = jnp.concatenate(
            [q[:, (g * SWA_GROUP + h) * SWA_HEAD_DIM:(g * SWA_GROUP + h + 1) * SWA_HEAD_DIM]
             for h in range(SWA_GROUP)], axis=0)
        qs = (_rms(qs, qn) * (SWA_HEAD_DIM ** -0.5)).astype(BF16)
        kc = _rms(kc_ref[:, lo:hi], kn).astype(BF16)
        kp = _rms(kp_ref[:, lo:hi], kn).astype(BF16)
        km = _rms(km_ref[:, lo:hi], kn).astype(BF16)
        s_c = jnp.where(ok_cur, _dot_nt(qs, kc), NEG_INF)
        s_p = jnp.where(ok_prev, _dot_nt(qs, kp), NEG_INF)
        s_m = jnp.where(ok_meta, _dot_nt(qs, km), NEG_INF)
        sink = sink_ref[g]
        mx = jnp.maximum(
            jnp.maximum(jnp.max(s_c, axis=-1, keepdims=True), jnp.max(s_p, axis=-1, keepdims=True)),
            jnp.maximum(jnp.max(s_m, axis=-1, keepdims=True), sink))
        p_c = jnp.exp(s_c - mx)
        p_p = jnp.exp(s_p - mx)
        p_m = jnp.exp(s_m - mx)
        den = (jnp.sum(p_c, axis=-1, keepdims=True) + jnp.sum(p_p, axis=-1, keepdims=True)
               + jnp.sum(p_m, axis=-1, keepdims=True) + jnp.exp(sink - mx))
        o = (_dot(p_c.astype(BF16), vc_ref[:, lo:hi].astype(BF16))
             + _dot(p_p.astype(BF16), vp_ref[:, lo:hi].astype(BF16))
             + _dot(p_m.astype(BF16), vm_ref[:, lo:hi].astype(BF16)))
        o = o / den
        outs.extend(o[h * BLOCK:(h + 1) * BLOCK, :] for h in range(SWA_GROUP))
    o_ref[...] = jnp.concatenate(outs, axis=1).astype(BF16)


def _swa(z, q_norm, k_norm, sink_col, nb, batch):
    m = z.shape[0]
    kcol = Z_SWA_K // SWA_KV_W
    vcol = Z_SWA_V // SWA_KV_W

    def cur(c):
        return pl.BlockSpec((BLOCK, SWA_KV_W), lambda b, n: (b * nb + n, c))

    def prev(c):
        return pl.BlockSpec((BLOCK, SWA_KV_W), lambda b, n: (b * nb + jnp.maximum(n - 1, 0), c))

    def meta(c):
        return pl.BlockSpec((BLOCK, SWA_KV_W), lambda b, n: (b * nb, c))

    return pl.pallas_call(
        _swa_body,
        grid=(batch, nb),
        in_specs=[
            pl.BlockSpec((BLOCK, SWA_OUT), lambda b, n: (b * nb + n, Z_SWA_Q // SWA_OUT)),
            cur(kcol), prev(kcol), meta(kcol),
            cur(vcol), prev(vcol), meta(vcol),
            pl.BlockSpec((1, SWA_HEAD_DIM), lambda b, n: (0, 0)),
            pl.BlockSpec((1, SWA_HEAD_DIM), lambda b, n: (0, 0)),
            pl.BlockSpec((SWA_KV_HEADS, SWA_GROUP * BLOCK, 1), lambda b, n: (0, 0, 0)),
        ],
        out_specs=pl.BlockSpec((BLOCK, SWA_OUT), lambda b, n: (b * nb + n, 0)),
        out_shape=jax.ShapeDtypeStruct((m, SWA_OUT), BF16),
        compiler_params=_params(("parallel", "arbitrary")),
        name="swa",
    )(z, z, z, z, z, z, z, q_norm.reshape(1, -1), k_norm.reshape(1, -1), sink_col)


def _rope_rows(x, cos, sin, half):
    width = x.shape[-1]
    lane = lax.broadcasted_iota(jnp.int32, x.shape, 1)
    rot = jnp.where(lane < half, pltpu.roll(x, width - half, 1), pltpu.roll(x, half, 1))
    return x * cos + rot * sin


def _mla_q_body(c_ref, an_ref, w_ref, qn_ref, qr_ref, cos_ref, sin_ref, o_ref, xn_ref):
    @pl.when(pl.program_id(1) == 0)
    def _():
        xn_ref[...] = _rms(c_ref[...], an_ref[...]).astype(BF16)

    y = _dot(xn_ref[...], w_ref[...].astype(BF16))
    scale = (MLA_NOPE_DIM + MLA_ROPE_DIM) ** -0.5
    nope = _rms(y[:, :MLA_NOPE_DIM], qn_ref[...])
    r = y[:, MLA_NOPE_DIM:]
    r = r * lax.rsqrt(jnp.sum(r * r, axis=-1, keepdims=True) * (1.0 / MLA_ROPE_DIM) + EPS) * qr_ref[...]
    r = _rope_rows(r, cos_ref[...], sin_ref[...], MLA_ROPE_DIM // 2)
    o_ref[...] = (jnp.concatenate([nope, r], axis=1) * scale).astype(BF16)


def _mla_q(z, a_norm, w_uq_pad, qn_norm, qr_norm_pad, cos_t, sin_t, layer, seq_tiles, *, tm=1056):
    m = z.shape[0]
    return pl.pallas_call(
        _mla_q_body,
        grid=(m // tm, MLA_HEADS),
        in_specs=[
            pl.BlockSpec((tm, MLA_Q_RANK), lambda i, h: (i, Z_MLA_CQ // MLA_Q_RANK)),
            pl.BlockSpec((1, MLA_Q_RANK), lambda i, h: (0, 0)),
            pl.BlockSpec((None, MLA_Q_RANK, MLA_QK_PAD), lambda i, h: (layer, 0, h)),
            pl.BlockSpec((1, MLA_NOPE_DIM), lambda i, h: (0, 0)),
            pl.BlockSpec((1, MLA_NOPE_DIM), lambda i, h: (0, 0)),
            pl.BlockSpec((tm, BLOCK), lambda i, h: (i % seq_tiles, 0)),
            pl.BlockSpec((tm, BLOCK), lambda i, h: (i % seq_tiles, 0)),
        ],
        out_specs=pl.BlockSpec((tm, MLA_QK_PAD), lambda i, h: (i, h)),
        out_shape=jax.ShapeDtypeStruct((m, MLA_HEADS * MLA_QK_PAD), BF16),
        scratch_shapes=[pltpu.VMEM((tm, MLA_Q_RANK), BF16)],
        compiler_params=_params(("parallel", "arbitrary")),
        name="mla_q_up",
    )(z, a_norm.reshape(1, -1), w_uq_pad, qn_norm.reshape(1, -1), qr_norm_pad, cos_t, sin_t)


def _mla_kv_body(c_ref, kr_ref, an_ref, w_ref, kn_ref, krn_ref, cos_ref, sin_ref, k_ref, v_ref, xn_ref, krp_ref):
    @pl.when(pl.program_id(1) == 0)
    def _():
        xn_ref[...] = _rms(c_ref[...], an_ref[...]).astype(BF16)
        r = kr_ref[...]
        r = r * lax.rsqrt(jnp.sum(r * r, axis=-1, keepdims=True) * (1.0 / MLA_ROPE_DIM) + EPS) * krn_ref[...]
        krp_ref[...] = _rope_rows(r, cos_ref[...], sin_ref[...], MLA_ROPE_DIM // 2).astype(BF16)

    y = _dot(xn_ref[...], w_ref[...].astype(BF16))
    k_nope = _rms(y[:, :MLA_NOPE_DIM], kn_ref[...]).astype(BF16)
    k_ref[...] = jnp.concatenate([k_nope, krp_ref[...]], axis=1)
    v_ref[...] = y[:, MLA_NOPE_DIM:].astype(BF16)


def _mla_kv(z, a_norm, w_ukv, kn_norm, kr_norm_pad, cos_t, sin_t, layer, seq_tiles, *, tm=1056):
    m = z.shape[0]
    return pl.pallas_call(
        _mla_kv_body,
        grid=(m // tm, MLA_HEADS),
        in_specs=[
            pl.BlockSpec((tm, MLA_KV_RANK), lambda i, h: (i, Z_MLA_CKV // MLA_KV_RANK)),
            pl.BlockSpec((tm, BLOCK), lambda i, h: (i, Z_MLA_KR // BLOCK)),
            pl.BlockSpec((1, MLA_KV_RANK), lambda i, h: (0, 0)),
            pl.BlockSpec((None, MLA_KV_RANK, MLA_NOPE_DIM + MLA_V_DIM), lambda i, h: (layer, 0, h)),
            pl.BlockSpec((1, MLA_NOPE_DIM), lambda i, h: (0, 0)),
            pl.BlockSpec((1, BLOCK), lambda i, h: (0, 0)),
            pl.BlockSpec((tm, BLOCK), lambda i, h: (i % seq_tiles, 0)),
            pl.BlockSpec((tm, BLOCK), lambda i, h: (i % seq_tiles, 0)),
        ],
        out_specs=[
            pl.BlockSpec((tm, MLA_QK_PAD), lambda i, h: (i, h)),
            pl.BlockSpec((tm, MLA_V_DIM), lambda i, h: (i, h)),
        ],
        out_shape=[
            jax.ShapeDtypeStruct((m, MLA_HEADS * MLA_QK_PAD), BF16),
            jax.ShapeDtypeStruct((m, MLA_OUT), BF16),
        ],
        scratch_shapes=[pltpu.VMEM((tm, MLA_KV_RANK), BF16), pltpu.VMEM((tm, BLOCK), BF16)],
        compiler_params=_params(("parallel", "arbitrary")),
        name="mla_kv_up",
    )(z, z, a_norm.reshape(1, -1), w_ukv, kn_norm.reshape(1, -1), kr_norm_pad, cos_t, sin_t)


def _mla_attn_body(q_ref, k_ref, v_ref, o_ref, *, tq, tk):
    i = pl.program_id(2)
    q = q_ref[...]
    row = i * tq + lax.broadcasted_iota(jnp.int32, (tq, tk), 0)
    col0 = lax.broadcasted_iota(jnp.int32, (tq, tk), 1)

    def step(jb, carry):
        m_run, l_run, acc = carry
        off = pl.multiple_of(jb * tk, tk)
        kb = k_ref[pl.ds(off, tk), :]
        vb = v_ref[pl.ds(off, tk), :]
        col = col0 + jb * tk
        s = jnp.where((col <= row) & (col >= PAD), _dot_nt(q, kb), NEG_INF)
        m_new = jnp.maximum(m_run, jnp.max(s, axis=-1, keepdims=True))
        alpha = jnp.exp(m_run - m_new)
        p = jnp.exp(s - m_new)
        l_new = alpha * l_run + jnp.sum(p, axis=-1, keepdims=True)
        acc = alpha * acc + _dot(p.astype(BF16), vb)
        return m_new, l_new, acc

    init = (jnp.full((tq, 1), NEG_INF, F32), jnp.zeros((tq, 1), F32), jnp.zeros((tq, MLA_V_DIM), F32))
    n_kb = (i * tq + tq + tk - 1) // tk
    _, l_fin, acc = lax.fori_loop(0, n_kb, step, init)
    o_ref[...] = (acc / l_fin).astype(BF16)


def _mla_attn(q, k, v, seq, batch, *, tq=384, tk=384):
    m = q.shape[0]
    nq = seq // tq
    return pl.pallas_call(
        functools.partial(_mla_attn_body, tq=tq, tk=tk),
        grid=(batch, MLA_HEADS, nq),
        in_specs=[
            pl.BlockSpec((tq, MLA_QK_PAD), lambda b, h, i: (b * nq + i, h)),
            pl.BlockSpec((seq, MLA_QK_PAD), lambda b, h, i: (b, h)),
            pl.BlockSpec((seq, MLA_V_DIM), lambda b, h, i: (b, h)),
        ],
        out_specs=pl.BlockSpec((tq, MLA_V_DIM), lambda b, h, i: (b * nq + i, h)),
        out_shape=jax.ShapeDtypeStruct((m, MLA_OUT), BF16),
        compiler_params=_params(("parallel", "parallel", "arbitrary")),
        name="mla_attn",
    )(q, k, v)


def _ret_body(q_ref, k_ref, v_ref, g_ref, cos_ref, sin_ref, dec_ref, zeta_ref, xi_ref, gn_ref, o_ref,
              state_ref, *, chunk_decay):
    n = pl.program_id(1)

    @pl.when(n == 0)
    def _():
        state_ref[...] = jnp.zeros_like(state_ref)

    cos = cos_ref[...]
    sin = sin_ref[...]
    valid = (n * BLOCK + lax.broadcasted_iota(jnp.int32, (BLOCK, 1), 0) >= PAD).astype(F32)
    for h in range(RET_HEADS):
        ks = slice(h * RET_KEY_DIM, (h + 1) * RET_KEY_DIM)
        vs = slice(h * RET_VAL_DIM, (h + 1) * RET_VAL_DIM)
        qh = _rope_rows(q_ref[:, ks], cos, sin, RET_KEY_DIM // 2)
        kh = _rope_rows(k_ref[:, ks], cos, sin, RET_KEY_DIM // 2) * (RET_KEY_DIM ** -0.5) * valid
        vb = v_ref[:, vs].astype(BF16)
        s = _dot_nt(qh.astype(BF16), kh.astype(BF16)) * dec_ref[h]
        inner = _dot(s.astype(BF16), vb)
        prev = state_ref[h]
        cross = _dot((qh * xi_ref[h]).astype(BF16), prev.astype(BF16))
        kz_t = (kh * zeta_ref[h]).T.astype(BF16)
        state_ref[h] = prev * chunk_decay[h] + _dot(kz_t, vb)
        o = inner + cross
        mu = jnp.mean(o, axis=-1, keepdims=True)
        d = o - mu
        y = d * lax.rsqrt(jnp.mean(d * d, axis=-1, keepdims=True) + EPS)
        gate = g_ref[:, vs]
        o_ref[:, vs] = (y * gn_ref[:, vs] * (gate * _sigmoid(gate))).astype(BF16)


def _retention(z, cos_t, sin_t, decay, zeta_b, xi_b, gn, chunk_decay, nb, batch):
    m = z.shape[0]
    tab = pl.BlockSpec((RET_HEADS, BLOCK, BLOCK), lambda b, n: (0, 0, 0))
    return pl.pallas_call(
        functools.partial(_ret_body, chunk_decay=chunk_decay),
        grid=(batch, nb),
        in_specs=[
            pl.BlockSpec((BLOCK, RET_QK_W), lambda b, n: (b * nb + n, Z_RET_Q // RET_QK_W)),
            pl.BlockSpec((BLOCK, RET_QK_W), lambda b, n: (b * nb + n, Z_RET_K // RET_QK_W)),
            pl.BlockSpec((BLOCK, RET_OUT), lambda b, n: (b * nb + n, Z_RET_V // RET_OUT)),
            pl.BlockSpec((BLOCK, RET_OUT), lambda b, n: (b * nb + n, Z_RET_G // RET_OUT)),
            pl.BlockSpec((BLOCK, BLOCK), lambda b, n: (n, 0)),
            pl.BlockSpec((BLOCK, BLOCK), lambda b, n: (n, 0)),
            tab, tab, tab,
            pl.BlockSpec((1, RET_OUT), lambda b, n: (0, 0)),
        ],
        out_specs=pl.BlockSpec((BLOCK, RET_OUT), lambda b, n: (b * nb + n, 0)),
        out_shape=jax.ShapeDtypeStruct((m, RET_OUT), BF16),
        scratch_shapes=[pltpu.VMEM((RET_HEADS, RET_KEY_DIM, RET_VAL_DIM), F32)],
        compiler_params=_params(("parallel", "arbitrary")),
        name="retention",
    )(z, z, z, z, cos_t, sin_t, decay, zeta_b, xi_b, gn.reshape(1, -1))


def _merge_body(oa_ref, ob_ref, oc_ref, wa_ref, wb_ref, wc_ref, ga_ref, gb_ref, gc_ref, o_ref):
    def branch(o, w, g):
        return _sigmoid(g[...]) * _dot(o[...], w[...].astype(BF16))

    o_ref[...] = (branch(oa_ref, wa_ref, ga_ref) + branch(ob_ref, wb_ref, gb_ref)
                  + branch(oc_ref, wc_ref, gc_ref)).astype(BF16)


def _merge(z, o_a, o_b, o_c, w_a, w_b, w_c, layer, *, tm=1056, tn=512):
    m = z.shape[0]
    nt = D_MODEL // tn

    def act(width):
        return pl.BlockSpec((tm, width), lambda i, j: (i, 0))

    def wgt(width):
        return pl.BlockSpec((None, width, tn), lambda i, j: (layer, 0, j))

    def gate(br):
        return pl.BlockSpec((tm, tn), lambda i, j: (i, br * nt + j))

    return pl.pallas_call(
        _merge_body,
        grid=(m // tm, nt),
        in_specs=[act(SWA_OUT), act(MLA_OUT), act(RET_OUT), wgt(SWA_OUT), wgt(MLA_OUT), wgt(RET_OUT),
                  gate(0), gate(1), gate(2)],
        out_specs=pl.BlockSpec((tm, tn), lambda i, j: (i, j)),
        out_shape=jax.ShapeDtypeStruct((m, D_MODEL), BF16),
        compiler_params=_params(("parallel", "arbitrary")),
        name="merge",
    )(o_a, o_b, o_c, w_a, w_b, w_c, z, z, z)


def _outproj_body(x_ref, w_ref, r_ref, o_ref):
    o_ref[...] = r_ref[...] + _dot(x_ref[...], w_ref[...].astype(BF16))


def _outproj(x, w, res, layer, *, tm=1056, tn=512):
    m = x.shape[0]
    return pl.pallas_call(
        _outproj_body,
        grid=(m // tm, D_MODEL // tn),
        in_specs=[
            pl.BlockSpec((tm, D_MODEL), lambda i, j: (i, 0)),
            pl.BlockSpec((None, D_MODEL, tn), lambda i, j: (layer, 0, j)),
            pl.BlockSpec((tm, tn), lambda i, j: (i, j)),
        ],
        out_specs=pl.BlockSpec((tm, tn), lambda i, j: (i, j)),
        out_shape=jax.ShapeDtypeStruct((m, D_MODEL), F32),
        compiler_params=_params(("parallel", "arbitrary")),
        name="outproj",
    )(x, w, res)


def _rope_tables(pos, dim, width):
    half = dim // 2
    inv_freq = ROPE_BASE ** (-jnp.arange(half, dtype=F32) / half)
    ang = pos[:, None] * inv_freq[None, :]
    cos = jnp.cos(ang)
    sin = jnp.sin(ang)
    fill = jnp.zeros((pos.shape[0], width - dim), F32)
    return (jnp.concatenate([cos, cos, fill], axis=1), jnp.concatenate([-sin, sin, fill], axis=1))


def kernel(x, meta_tokens, ffn1_norm, ffn1_w_gate, ffn1_w_up, ffn1_w_down, mix_norm, w_in, swa_q_norm, swa_k_norm, swa_sinks, mla_q_a_norm, mla_w_uq, mla_kv_a_norm, mla_w_ukv, mla_qn_norm, mla_qr_norm, mla_kn_norm, mla_kr_norm, ret_gn, w_br_swa, w_br_mla, w_br_ret, w_o, ffn2_norm, ffn2_w_gate, ffn2_w_up, ffn2_w_down):
    batch, seq_in, _ = x.shape
    seq = seq_in + BLOCK
    nb = seq // BLOCK
    tm = 1056
    assert seq % tm == 0
    seq_tiles = seq // tm

    meta = jnp.broadcast_to(meta_tokens[None].astype(x.dtype), (batch, N_META, D_MODEL))
    h = jnp.concatenate([jnp.zeros((batch, PAD, D_MODEL), x.dtype), meta, x], axis=1)
    h = h.reshape(batch * seq, D_MODEL)
    pos = (jnp.arange(seq) - PAD).astype(F32)

    splits = np.cumsum([0, SWA_OUT, SWA_KV_W, SWA_KV_W, MLA_Q_RANK, MLA_KV_RANK, MLA_ROPE_DIM,
                        RET_QK_W, RET_QK_W, RET_OUT, RET_OUT, N_BRANCH * D_MODEL])
    seg = [w_in[:, :, splits[i]:splits[i + 1]] for i in range(11)]
    (s_swa_q, s_swa_k, s_swa_v, s_cq, s_ckv, s_kr, s_ret_q, s_ret_k, s_ret_v, s_ret_g, s_gate) = seg
    w_z = jnp.concatenate(
        [s_gate, s_ret_v, s_ret_g, s_swa_q, s_cq, s_ckv, s_ret_q, s_ret_k, s_swa_k, s_swa_v, s_kr,
         jnp.zeros((DEPTH, D_MODEL, MLA_ROPE_DIM), w_in.dtype)], axis=-1).astype(BF16)

    qk_dim = MLA_NOPE_DIM + MLA_ROPE_DIM
    w_uq_pad = jnp.pad(mla_w_uq.reshape(DEPTH, MLA_Q_RANK, MLA_HEADS, qk_dim),
                       ((0, 0), (0, 0), (0, 0), (0, MLA_QK_PAD - qk_dim)))
    w_uq_pad = w_uq_pad.reshape(DEPTH, MLA_Q_RANK, MLA_HEADS * MLA_QK_PAD)
    rope_fill = jnp.zeros((DEPTH, BLOCK - MLA_ROPE_DIM), F32)
    qr_norm_pad = jnp.concatenate([mla_qr_norm, rope_fill], axis=1)
    kr_norm_pad = jnp.concatenate([mla_kr_norm, rope_fill], axis=1)

    cos_m, sin_m = _rope_tables(pos, MLA_ROPE_DIM, BLOCK)
    cos_r, sin_r = _rope_tables(pos, RET_KEY_DIM, BLOCK)

    log_gamma = jnp.log(1.0 - 2.0 ** (-5.0 - jnp.arange(RET_HEADS, dtype=F32)))
    idx = jnp.arange(BLOCK, dtype=F32)
    diff = idx[:, None] - idx[None, :]
    decay = jnp.where(diff[None] >= 0, jnp.exp(jnp.maximum(diff, 0.0)[None] * log_gamma[:, None, None]), 0.0)
    zeta = jnp.exp((BLOCK - 1.0 - idx)[None, :] * log_gamma[:, None])
    xi = jnp.exp((idx + 1.0)[None, :] * log_gamma[:, None])
    zeta_b = jnp.broadcast_to(zeta[:, :, None], (RET_HEADS, BLOCK, RET_KEY_DIM))
    xi_b = jnp.broadcast_to(xi[:, :, None], (RET_HEADS, BLOCK, RET_KEY_DIM))
    chunk_decay = tuple(float(np.exp(BLOCK * np.log(1.0 - 2.0 ** (-5.0 - hh)))) for hh in range(RET_HEADS))

    for l in range(DEPTH):
        h = _ffn(h, ffn1_norm, ffn1_w_gate, ffn1_w_up, ffn1_w_down, l)
        z = _inproj(h, mix_norm, w_z, l)
        sink_col = jnp.repeat(swa_sinks[l].astype(F32).reshape(SWA_KV_HEADS, SWA_GROUP), BLOCK, axis=1)
        o_a = _swa(z, swa_q_norm[l], swa_k_norm[l], sink_col[:, :, None], nb, batch)
        q = _mla_q(z, mla_q_a_norm[l], w_uq_pad, mla_qn_norm[l], qr_norm_pad[l:l + 1], cos_m, sin_m, l, seq_tiles)
        k, v = _mla_kv(z, mla_kv_a_norm[l], mla_w_ukv, mla_kn_norm[l], kr_norm_pad[l:l + 1], cos_m, sin_m, l,
                       seq_tiles)
        o_b = _mla_attn(q, k, v, seq, batch)
        o_c = _retention(z, cos_r, sin_r, decay, zeta_b, xi_b, ret_gn[l], chunk_decay, nb, batch)
        merged = _merge(z, o_a, o_b, o_c, w_br_swa, w_br_mla, w_br_ret, l)
        h = _outproj(merged, w_o, h, l)
        h = _ffn(h, ffn2_norm, ffn2_w_gate, ffn2_w_up, ffn2_w_down, l)
    return h.reshape(batch, seq, D_MODEL)[:, BLOCK:]
```

```python
import functools

import jax
import jax.numpy as jnp
import numpy as np
from jax import lax
from jax.experimental import pallas as pl
from jax.experimental.pallas import tpu as pltpu

F32 = jnp.float32
BF16 = jnp.bfloat16

D_MODEL = 2048
DEPTH = 4
N_META = 16
BLOCK = 128
PAD = BLOCK - N_META
EPS = 1e-6
NEG_INF = -1e30
ROPE_BASE = 10000.0
HALF_STEP = 0.5
LOG2_E = 1.4426950408889634

SWA_HEADS = 16
SWA_KV_HEADS = 2
SWA_HEAD_DIM = 64
SWA_GROUP = SWA_HEADS // SWA_KV_HEADS
WINDOW = 128

MLA_HEADS = 8
MLA_Q_RANK = 512
MLA_KV_RANK = 512
MLA_NOPE_DIM = 128
MLA_ROPE_DIM = 64
MLA_V_DIM = 128
MLA_QK_PAD = 256

RET_HEADS = 4
RET_KEY_DIM = 128
RET_VAL_DIM = 256

D_FF = 5632
N_BRANCH = 3

SWA_OUT = SWA_HEADS * SWA_HEAD_DIM
SWA_KV_W = SWA_KV_HEADS * SWA_HEAD_DIM
MLA_OUT = MLA_HEADS * MLA_V_DIM
RET_QK_W = RET_HEADS * RET_KEY_DIM
RET_OUT = RET_HEADS * RET_VAL_DIM

Z_GATE = 0
Z_RET_V = Z_GATE + N_BRANCH * D_MODEL
Z_RET_G = Z_RET_V + RET_OUT
Z_SWA_Q = Z_RET_G + RET_OUT
Z_MLA_CQ = Z_SWA_Q + SWA_OUT
Z_MLA_CKV = Z_MLA_CQ + MLA_Q_RANK
Z_RET_Q = Z_MLA_CKV + MLA_KV_RANK
Z_RET_K = Z_RET_Q + RET_QK_W
Z_SWA_K = Z_RET_K + RET_QK_W
Z_SWA_V = Z_SWA_K + SWA_KV_W
Z_MLA_KR = Z_SWA_V + SWA_KV_W
Z_WIDTH = Z_MLA_KR + 2 * MLA_ROPE_DIM

VMEM_LIMIT = 56 * 1024 * 1024


def _params(semantics):
    return pltpu.CompilerParams(dimension_semantics=semantics, vmem_limit_bytes=VMEM_LIMIT)


def _rms(x, g):
    return x * lax.rsqrt(jnp.mean(x * x, axis=-1, keepdims=True) + EPS) * g


def _sigmoid(x):
    return 1.0 / (1.0 + jnp.exp(-x))


def _dot(a, b):
    return jnp.dot(a, b, preferred_element_type=F32)


def _dot_nt(a, b):
    return lax.dot_general(a, b, (((1,), (1,)), ((), ())), preferred_element_type=F32)


def _ffn_body(x_ref, g_ref, wg_ref, wu_ref, wd_ref, o_ref, xn_ref):
    @pl.when(pl.program_id(1) == 0)
    def _():
        x = x_ref[...]
        xn_ref[...] = _rms(x, g_ref[...]).astype(BF16)
        o_ref[...] = x

    xn = xn_ref[...]
    gate = _dot(xn, wg_ref[...].astype(BF16))
    up = _dot(xn, wu_ref[...].astype(BF16))
    act = (gate * _sigmoid(gate) * up).astype(BF16)
    o_ref[...] += HALF_STEP * _dot(act, wd_ref[...].astype(BF16))


def _ffn(h, gain, w_gate, w_up, w_down, layer, *, tm=1056, tf=256):
    m = h.shape[0]
    return pl.pallas_call(
        _ffn_body,
        grid=(m // tm, D_FF // tf),
        in_specs=[
            pl.BlockSpec((tm, D_MODEL), lambda i, f: (i, 0)),
            pl.BlockSpec((None, 1, D_MODEL), lambda i, f: (layer, 0, 0)),
            pl.BlockSpec((None, D_MODEL, tf), lambda i, f: (layer, 0, f)),
            pl.BlockSpec((None, D_MODEL, tf), lambda i, f: (layer, 0, f)),
            pl.BlockSpec((None, tf, D_MODEL), lambda i, f: (layer, f, 0)),
        ],
        out_specs=pl.BlockSpec((tm, D_MODEL), lambda i, f: (i, 0)),
        out_shape=jax.ShapeDtypeStruct((m, D_MODEL), F32),
        scratch_shapes=[pltpu.VMEM((tm, D_MODEL), BF16)],
        compiler_params=_params(("parallel", "arbitrary")),
        name="ffn",
    )(h, gain.reshape(DEPTH, 1, D_MODEL), w_gate, w_up, w_down)


_IN_WIDTHS = (SWA_OUT, SWA_KV_W, SWA_KV_W, MLA_Q_RANK, MLA_KV_RANK, MLA_ROPE_DIM,
              RET_QK_W, RET_QK_W, RET_OUT, RET_OUT, N_BRANCH * D_MODEL)
_IN_STARTS = tuple(int(v) for v in np.cumsum((0,) + _IN_WIDTHS)[:-1])
_Z_STARTS = (Z_SWA_Q, Z_SWA_K, Z_SWA_V, Z_MLA_CQ, Z_MLA_CKV, Z_MLA_KR, Z_RET_Q, Z_RET_K, Z_RET_V, Z_RET_G, Z_GATE)
IN_WIDTH = sum(_IN_WIDTHS)
REPACK_COPY, REPACK_SHIFT, REPACK_HALF = 0, 1, 2


def _repack_tables():
    n_blk = Z_WIDTH // BLOCK
    first = np.zeros(n_blk, np.int32)
    second = np.zeros(n_blk, np.int32)
    mode = np.zeros(n_blk, np.int32)
    half = BLOCK // 2
    for dst, src, width in zip(_Z_STARTS, _IN_STARTS, _IN_WIDTHS):
        for c in range(dst // BLOCK, (dst + max(width, BLOCK)) // BLOCK):
            s = src + (c * BLOCK - dst)
            first[c] = s // BLOCK
            second[c] = min(s // BLOCK + 1, (IN_WIDTH - 1) // BLOCK)
            if width < BLOCK:
                assert s % BLOCK == 0 and width == half
                mode[c] = REPACK_HALF
            else:
                assert s % BLOCK in (0, half)
                mode[c] = REPACK_COPY if s % BLOCK == 0 else REPACK_SHIFT
    return first, second, mode


def _repack_body(first_ref, second_ref, mode_ref, a_ref, b_ref, o_ref):
    del first_ref, second_ref
    mode = mode_ref[pl.program_id(1)]
    half = BLOCK // 2
    a = a_ref[...]
    b = b_ref[...]
    shifted = jnp.concatenate([a[:, half:], b[:, :half]], axis=1)
    padded = jnp.concatenate([a[:, :half], jnp.zeros_like(a[:, :half])], axis=1)
    o_ref[...] = jnp.where(mode == REPACK_COPY, a, jnp.where(mode == REPACK_SHIFT, shifted, padded)).astype(BF16)


def _repack_w_in(w_in):
    first, second, mode = _repack_tables()
    grid_spec = pltpu.PrefetchScalarGridSpec(
        num_scalar_prefetch=3,
        grid=(DEPTH, Z_WIDTH // BLOCK),
        in_specs=[
            pl.BlockSpec((None, D_MODEL, BLOCK), lambda l, c, fi, se, mo: (l, 0, fi[c])),
            pl.BlockSpec((None, D_MODEL, BLOCK), lambda l, c, fi, se, mo: (l, 0, se[c])),
        ],
        out_specs=pl.BlockSpec((None, D_MODEL, BLOCK), lambda l, c, fi, se, mo: (l, 0, c)),
    )
    return pl.pallas_call(
        _repack_body,
        grid_spec=grid_spec,
        out_shape=jax.ShapeDtypeStruct((DEPTH, D_MODEL, Z_WIDTH), BF16),
        compiler_params=_params(("parallel", "arbitrary")),
        name="repack_w_in",
    )(jnp.asarray(first), jnp.asarray(second), jnp.asarray(mode), w_in, w_in)


def _inproj_body(x_ref, g_ref, w_ref, o_ref, xn_ref):
    @pl.when(pl.program_id(1) == 0)
    def _():
        xn_ref[...] = _rms(x_ref[...], g_ref[...]).astype(BF16)

    o_ref[...] = _dot(xn_ref[...], w_ref[...])


def _inproj(h, gain, w_z, layer, *, tm=1056, tn=896):
    m = h.shape[0]
    return pl.pallas_call(
        _inproj_body,
        grid=(m // tm, Z_WIDTH // tn),
        in_specs=[
            pl.BlockSpec((tm, D_MODEL), lambda i, j: (i, 0)),
            pl.BlockSpec((None, 1, D_MODEL), lambda i, j: (layer, 0, 0)),
            pl.BlockSpec((None, D_MODEL, tn), lambda i, j: (layer, 0, j)),
        ],
        out_specs=pl.BlockSpec((tm, tn), lambda i, j: (i, j)),
        out_shape=jax.ShapeDtypeStruct((m, Z_WIDTH), F32),
        scratch_shapes=[pltpu.VMEM((tm, D_MODEL), BF16)],
        compiler_params=_params(("parallel", "arbitrary")),
        name="inproj",
    )(h, gain.reshape(DEPTH, 1, D_MODEL), w_z)


def _swa_body(q_ref, kc_ref, kp_ref, km_ref, vc_ref, vp_ref, vm_ref, qn_ref, kn_ref, sink_ref, o_ref):
    n = pl.program_id(1)
    j = lax.broadcasted_iota(jnp.int32, (BLOCK, BLOCK), 0)
    i_loc = lax.broadcasted_iota(jnp.int32, (BLOCK, BLOCK), 1)
    ok_cur = (j <= i_loc) & (n * BLOCK + j >= PAD)
    ok_prev = (j > i_loc) & ((n - 1) * BLOCK + j >= PAD)
    ok_meta = (j >= PAD) & (n * BLOCK + i_loc - j >= WINDOW)

    def mask_rows(s_blk, ok):
        return jnp.concatenate(
            [jnp.where(ok, s_blk[:, h * BLOCK:(h + 1) * BLOCK], NEG_INF) for h in range(SWA_GROUP)], axis=1)

    q_t = [q_ref[:, a * BLOCK:(a + 1) * BLOCK].T for a in range(SWA_HEADS // 2)]
    v_t = jnp.concatenate([vm_ref[...].T, vp_ref[...].T, vc_ref[...].T], axis=1)
    k_gain = kn_ref[...] * qn_ref[...] * (SWA_HEAD_DIM ** -0.5)

    out_t = []
    for g in range(SWA_KV_HEADS):
        lo = g * SWA_HEAD_DIM
        hi = lo + SWA_HEAD_DIM
        heads = [g * SWA_GROUP + h for h in range(SWA_GROUP)]
        qt = jnp.concatenate(
            [q_t[hd // 2][(hd % 2) * SWA_HEAD_DIM:(hd % 2 + 1) * SWA_HEAD_DIM, :] for hd in heads], axis=1)
        q_rinv = lax.rsqrt(jnp.sum(qt * qt, axis=0, keepdims=True) * (1.0 / SWA_HEAD_DIM) + EPS)
        k_all = jnp.concatenate(
            [_rms(r[:, lo:hi], k_gain) for r in (km_ref, kp_ref, kc_ref)], axis=0).astype(BF16)
        s = _dot(k_all, qt.astype(BF16)) * q_rinv
        s = jnp.concatenate([mask_rows(s[:BLOCK], ok_meta), mask_rows(s[BLOCK:2 * BLOCK], ok_prev),
                             mask_rows(s[2 * BLOCK:], ok_cur)], axis=0)
        sink = sink_ref[g]
        mx = jnp.maximum(jnp.max(s, axis=0, keepdims=True), sink)
        p = jnp.exp(s - mx)
        den = jnp.sum(p, axis=0, keepdims=True) + jnp.exp(sink - mx)
        o_t = _dot(v_t[lo:hi].astype(BF16), p.astype(BF16)) / den
        out_t.extend(o_t[:, h * BLOCK:(h + 1) * BLOCK] for h in range(SWA_GROUP))
    for a in range(SWA_HEADS // 2):
        pair = jnp.concatenate([out_t[2 * a], out_t[2 * a + 1]], axis=0)
        o_ref[:, a * BLOCK:(a + 1) * BLOCK] = pair.T.astype(BF16)


def _swa(z, q_norm, k_norm, sink_col, nb, batch):
    m = z.shape[0]
    kcol = Z_SWA_K // SWA_KV_W
    vcol = Z_SWA_V // SWA_KV_W

    def cur(c):
        return pl.BlockSpec((BLOCK, SWA_KV_W), lambda b, n: (b * nb + n, c))

    def prev(c):
        return pl.BlockSpec((BLOCK, SWA_KV_W), lambda b, n: (b * nb + jnp.maximum(n - 1, 0), c))

    def meta(c):
        return pl.BlockSpec((BLOCK, SWA_KV_W), lambda b, n: (b * nb, c))

    return pl.pallas_call(
        _swa_body,
        grid=(batch, nb),
        in_specs=[
            pl.BlockSpec((BLOCK, SWA_OUT), lambda b, n: (b * nb + n, Z_SWA_Q // SWA_OUT)),
            cur(kcol), prev(kcol), meta(kcol),
            cur(vcol), prev(vcol), meta(vcol),
            pl.BlockSpec((1, SWA_HEAD_DIM), lambda b, n: (0, 0)),
            pl.BlockSpec((1, SWA_HEAD_DIM), lambda b, n: (0, 0)),
            pl.BlockSpec((SWA_KV_HEADS, 1, SWA_GROUP * BLOCK), lambda b, n: (0, 0, 0)),
        ],
        out_specs=pl.BlockSpec((BLOCK, SWA_OUT), lambda b, n: (b * nb + n, 0)),
        out_shape=jax.ShapeDtypeStruct((m, SWA_OUT), BF16),
        compiler_params=_params(("parallel", "arbitrary")),
        name="swa",
    )(z, z, z, z, z, z, z, q_norm.reshape(1, -1), k_norm.reshape(1, -1), sink_col)


def _rope_rows(x, cos, sin, half):
    width = x.shape[-1]
    lane = lax.broadcasted_iota(jnp.int32, x.shape, 1)
    rot = jnp.where(lane < half, pltpu.roll(x, width - half, 1), pltpu.roll(x, half, 1))
    return x * cos + rot * sin


def _rope_key_norm(r, gain):
    return r * lax.rsqrt(jnp.sum(r * r, axis=-1, keepdims=True) * (1.0 / MLA_ROPE_DIM) + EPS) * gain


def _mla_up_body(cq_ref, ckv_ref, kr_ref, qa_ref, kva_ref, wq_ref, wkv_ref, qn_ref, qr_ref, kn_ref, krn_ref,
                 cos_ref, sin_ref, q_ref, k_ref, v_ref):
    cos = cos_ref[...]
    sin = sin_ref[...]
    half = MLA_ROPE_DIM // 2
    scale = (MLA_NOPE_DIM + MLA_ROPE_DIM) ** -0.5 * LOG2_E
    yq = _dot(_rms(cq_ref[...], qa_ref[...]).astype(BF16), wq_ref[...])
    ykv = _dot(_rms(ckv_ref[...], kva_ref[...]).astype(BF16), wkv_ref[...].astype(BF16))
    k_rope = _rope_rows(_rope_key_norm(kr_ref[...], krn_ref[...]), cos, sin, half).astype(BF16)
    for h in range(MLA_HEADS):
        lo = h * MLA_QK_PAD
        nope = _rms(yq[:, lo:lo + MLA_NOPE_DIM], qn_ref[...])
        r = _rope_rows(_rope_key_norm(yq[:, lo + MLA_NOPE_DIM:lo + MLA_QK_PAD], qr_ref[...]), cos, sin, half)
        q_ref[:, lo:lo + MLA_NOPE_DIM] = (nope * scale).astype(BF16)
        q_ref[:, lo + MLA_NOPE_DIM:lo + MLA_QK_PAD] = (r * scale).astype(BF16)
        k_ref[:, lo:lo + MLA_NOPE_DIM] = _rms(ykv[:, lo:lo + MLA_NOPE_DIM], kn_ref[...]).astype(BF16)
        k_ref[:, lo + MLA_NOPE_DIM:lo + MLA_QK_PAD] = k_rope
        v_ref[:, h * MLA_V_DIM:(h + 1) * MLA_V_DIM] = ykv[:, lo + MLA_NOPE_DIM:lo + MLA_QK_PAD].astype(BF16)


def _mla_up(z, qa_norm, kva_norm, w_uq_pad, w_ukv, qn_norm, qr_norm_pad, kn_norm, kr_norm_pad, cos_t, sin_t,
            layer, seq, *, tm=528):
    m = z.shape[0]
    seq_tiles = seq // tm
    kv_w = MLA_NOPE_DIM + MLA_V_DIM

    def row(width, col):
        return pl.BlockSpec((tm, width), lambda i: (i, col))

    def vec(width):
        return pl.BlockSpec((1, width), lambda i: (0, 0))

    return pl.pallas_call(
        _mla_up_body,
        grid=(m // tm,),
        in_specs=[
            row(MLA_Q_RANK, Z_MLA_CQ // MLA_Q_RANK), row(MLA_KV_RANK, Z_MLA_CKV // MLA_KV_RANK),
            row(BLOCK, Z_MLA_KR // BLOCK),
            vec(MLA_Q_RANK), vec(MLA_KV_RANK),
            pl.BlockSpec((None, MLA_Q_RANK, MLA_HEADS * MLA_QK_PAD), lambda i: (layer, 0, 0)),
            pl.BlockSpec((None, MLA_KV_RANK, MLA_HEADS * kv_w), lambda i: (layer, 0, 0)),
            vec(MLA_NOPE_DIM), vec(BLOCK), vec(MLA_NOPE_DIM), vec(BLOCK),
            pl.BlockSpec((tm, BLOCK), lambda i: (i % seq_tiles, 0)),
            pl.BlockSpec((tm, BLOCK), lambda i: (i % seq_tiles, 0)),
        ],
        out_specs=[row(MLA_HEADS * MLA_QK_PAD, 0), row(MLA_HEADS * MLA_QK_PAD, 0), row(MLA_OUT, 0)],
        out_shape=[
            jax.ShapeDtypeStruct((m, MLA_HEADS * MLA_QK_PAD), BF16),
            jax.ShapeDtypeStruct((m, MLA_HEADS * MLA_QK_PAD), BF16),
            jax.ShapeDtypeStruct((m, MLA_OUT), BF16),
        ],
        compiler_params=_params(("parallel",)),
        name="mla_up",
    )(z, z, z, qa_norm.reshape(1, -1), kva_norm.reshape(1, -1), w_uq_pad, w_ukv, qn_norm.reshape(1, -1),
      qr_norm_pad, kn_norm.reshape(1, -1), kr_norm_pad, cos_t, sin_t)


def _mla_attn_body(q_ref, k_ref, v_ref, o_ref, m_ref, acc_ref, *, t, hb):
    i = pl.program_id(2)
    m_ref[...] = jnp.full(m_ref.shape, NEG_INF, F32)
    acc_ref[...] = jnp.zeros(acc_ref.shape, F32)

    def tile(blk, nblk, masked):
        width = nblk * t
        off = blk * t if isinstance(blk, int) else pl.multiple_of(blk * t, t)
        if masked:
            row = i * t + lax.broadcasted_iota(jnp.int32, (t, width), 0)
            col = blk * t + lax.broadcasted_iota(jnp.int32, (t, width), 1)
            ok = (col <= row) & (col >= PAD)
        for h in range(hb):
            q = q_ref[:, h * MLA_QK_PAD:(h + 1) * MLA_QK_PAD]
            kb = k_ref[pl.ds(off, width), h * MLA_QK_PAD:(h + 1) * MLA_QK_PAD]
            vb = v_ref[pl.ds(off, width), h * MLA_V_DIM:(h + 1) * MLA_V_DIM]
            s = _dot_nt(q, kb)
            if masked:
                s = jnp.where(ok, s, NEG_INF)
            m_run = m_ref[h]
            m_new = jnp.maximum(m_run, jnp.max(s, axis=-1, keepdims=True))
            alpha = jnp.exp2(m_run - m_new)
            p = jnp.concatenate(
                [jnp.exp2(s[:, c * BLOCK:(c + 1) * BLOCK] - m_new) for c in range(width // BLOCK)], axis=1)
            v_aug = jnp.concatenate([vb, jnp.ones_like(vb)], axis=1)
            acc_ref[h] = jnp.concatenate([alpha, alpha], axis=1) * acc_ref[h] + _dot(p.astype(BF16), v_aug)
            m_ref[h] = m_new

    @pl.when(i >= 2)
    def _():
        tile(0, 2, True)

    def body(jj, carry):
        tile(2 * jj, 2, False)
        return carry

    lax.fori_loop(1, i // 2, body, 0)

    @pl.when(i % 2 == 1)
    def _():
        tile(i - 1, 1, True)

    tile(i, 1, True)

    for h in range(hb):
        acc = acc_ref[h]
        o_ref[:, h * MLA_V_DIM:(h + 1) * MLA_V_DIM] = (acc[:, :MLA_V_DIM] / acc[:, MLA_V_DIM:]).astype(BF16)


def _mla_attn(q, k, v, seq, batch, *, t=384, hb=4):
    m = q.shape[0]
    nq = seq // t
    return pl.pallas_call(
        functools.partial(_mla_attn_body, t=t, hb=hb),
        grid=(batch, MLA_HEADS // hb, nq),
        in_specs=[
            pl.BlockSpec((t, hb * MLA_QK_PAD), lambda b, h, i: (b * nq + i, h)),
            pl.BlockSpec((seq, hb * MLA_QK_PAD), lambda b, h, i: (b, h)),
            pl.BlockSpec((seq, hb * MLA_V_DIM), lambda b, h, i: (b, h)),
        ],
        out_specs=pl.BlockSpec((t, hb * MLA_V_DIM), lambda b, h, i: (b * nq + i, h)),
        out_shape=jax.ShapeDtypeStruct((m, MLA_OUT), BF16),
        scratch_shapes=[pltpu.VMEM((hb, t, BLOCK), F32), pltpu.VMEM((hb, t, 2 * MLA_V_DIM), F32)],
        compiler_params=_params(("parallel", "parallel", "arbitrary")),
        name="mla_attn",
    )(q, k, v)


def _ret_body(q_ref, k_ref, v_ref, g_ref, cos_ref, sin_ref, dec_ref, zeta_ref, xi_ref, gn_ref, o_ref,
              state_ref, *, chunk_decay):
    n = pl.program_id(1)

    @pl.when(n == 0)
    def _():
        state_ref[...] = jnp.zeros_like(state_ref)

    cos = cos_ref[...]
    sin = sin_ref[...]
    valid = (n * BLOCK + lax.broadcasted_iota(jnp.int32, (BLOCK, 1), 0) >= PAD).astype(F32)
    for h in range(RET_HEADS):
        ks = slice(h * RET_KEY_DIM, (h + 1) * RET_KEY_DIM)
        vs = slice(h * RET_VAL_DIM, (h + 1) * RET_VAL_DIM)
        qh = _rope_rows(q_ref[:, ks], cos, sin, RET_KEY_DIM // 2)
        kh = _rope_rows(k_ref[:, ks], cos, sin, RET_KEY_DIM // 2) * (RET_KEY_DIM ** -0.5) * valid
        vb = v_ref[:, vs].astype(BF16)
        s = _dot_nt(qh.astype(BF16), kh.astype(BF16)) * dec_ref[h]
        inner = _dot(s.astype(BF16), vb)
        prev = state_ref[h]
        cross = _dot((qh * xi_ref[h]).astype(BF16), prev.astype(BF16))
        kz_t = (kh * zeta_ref[h]).T.astype(BF16)
        state_ref[h] = prev * chunk_decay[h] + _dot(kz_t, vb)
        o = inner + cross
        mu = jnp.mean(o, axis=-1, keepdims=True)
        d = o - mu
        y = d * lax.rsqrt(jnp.mean(d * d, axis=-1, keepdims=True) + EPS)
        gate = g_ref[:, vs]
        o_ref[:, vs] = (y * gn_ref[:, vs] * (gate * _sigmoid(gate))).astype(BF16)


def _retention(z, cos_t, sin_t, decay, zeta_b, xi_b, gn, chunk_decay, nb, batch):
    m = z.shape[0]
    tab = pl.BlockSpec((RET_HEADS, BLOCK, BLOCK), lambda b, n: (0, 0, 0))
    return pl.pallas_call(
        functools.partial(_ret_body, chunk_decay=chunk_decay),
        grid=(batch, nb),
        in_specs=[
            pl.BlockSpec((BLOCK, RET_QK_W), lambda b, n: (b * nb + n, Z_RET_Q // RET_QK_W)),
            pl.BlockSpec((BLOCK, RET_QK_W), lambda b, n: (b * nb + n, Z_RET_K // RET_QK_W)),
            pl.BlockSpec((BLOCK, RET_OUT), lambda b, n: (b * nb + n, Z_RET_V // RET_OUT)),
            pl.BlockSpec((BLOCK, RET_OUT), lambda b, n: (b * nb + n, Z_RET_G // RET_OUT)),
            pl.BlockSpec((BLOCK, BLOCK), lambda b, n: (n, 0)),
            pl.BlockSpec((BLOCK, BLOCK), lambda b, n: (n, 0)),
            tab, tab, tab,
            pl.BlockSpec((1, RET_OUT), lambda b, n: (0, 0)),
        ],
        out_specs=pl.BlockSpec((BLOCK, RET_OUT), lambda b, n: (b * nb + n, 0)),
        out_shape=jax.ShapeDtypeStruct((m, RET_OUT), BF16),
        scratch_shapes=[pltpu.VMEM((RET_HEADS, RET_KEY_DIM, RET_VAL_DIM), F32)],
        compiler_params=_params(("parallel", "arbitrary")),
        name="retention",
    )(z, z, z, z, cos_t, sin_t, decay, zeta_b, xi_b, gn.reshape(1, -1))


def _merge_body(oa_ref, ob_ref, oc_ref, wa_ref, wb_ref, wc_ref, ga_ref, gb_ref, gc_ref, o_ref):
    def branch(o, w, g):
        return _sigmoid(g[...]) * _dot(o[...], w[...].astype(BF16))

    o_ref[...] = (branch(oa_ref, wa_ref, ga_ref) + branch(ob_ref, wb_ref, gb_ref)
                  + branch(oc_ref, wc_ref, gc_ref)).astype(BF16)


def _merge(z, o_a, o_b, o_c, w_a, w_b, w_c, layer, *, tm=1056, tn=512):
    m = z.shape[0]
    nt = D_MODEL // tn

    def act(width):
        return pl.BlockSpec((tm, width), lambda i, j: (i, 0))

    def wgt(width):
        return pl.BlockSpec((None, width, tn), lambda i, j: (layer, 0, j))

    def gate(br):
        return pl.BlockSpec((tm, tn), lambda i, j: (i, br * nt + j))

    return pl.pallas_call(
        _merge_body,
        grid=(m // tm, nt),
        in_specs=[act(SWA_OUT), act(MLA_OUT), act(RET_OUT), wgt(SWA_OUT), wgt(MLA_OUT), wgt(RET_OUT),
                  gate(0), gate(1), gate(2)],
        out_specs=pl.BlockSpec((tm, tn), lambda i, j: (i, j)),
        out_shape=jax.ShapeDtypeStruct((m, D_MODEL), BF16),
        compiler_params=_params(("parallel", "arbitrary")),
        name="merge",
    )(o_a, o_b, o_c, w_a, w_b, w_c, z, z, z)


def _outproj_body(x_ref, w_ref, r_ref, o_ref):
    o_ref[...] = r_ref[...] + _dot(x_ref[...], w_ref[...].astype(BF16))


def _outproj(x, w, res, layer, *, tm=1056, tn=512):
    m = x.shape[0]
    return pl.pallas_call(
        _outproj_body,
        grid=(m // tm, D_MODEL // tn),
        in_specs=[
            pl.BlockSpec((tm, D_MODEL), lambda i, j: (i, 0)),
            pl.BlockSpec((None, D_MODEL, tn), lambda i, j: (layer, 0, j)),
            pl.BlockSpec((tm, tn), lambda i, j: (i, j)),
        ],
        out_specs=pl.BlockSpec((tm, tn), lambda i, j: (i, j)),
        out_shape=jax.ShapeDtypeStruct((m, D_MODEL), F32),
        compiler_params=_params(("parallel", "arbitrary")),
        name="outproj",
    )(x, w, res)


def _rope_tables(pos, dim, width):
    half = dim // 2
    inv_freq = ROPE_BASE ** (-jnp.arange(half, dtype=F32) / half)
    ang = pos[:, None] * inv_freq[None, :]
    cos = jnp.cos(ang)
    sin = jnp.sin(ang)
    fill = jnp.zeros((pos.shape[0], width - dim), F32)
    return (jnp.concatenate([cos, cos, fill], axis=1), jnp.concatenate([-sin, sin, fill], axis=1))


def kernel(x, meta_tokens, ffn1_norm, ffn1_w_gate, ffn1_w_up, ffn1_w_down, mix_norm, w_in, swa_q_norm, swa_k_norm, swa_sinks, mla_q_a_norm, mla_w_uq, mla_kv_a_norm, mla_w_ukv, mla_qn_norm, mla_qr_norm, mla_kn_norm, mla_kr_norm, ret_gn, w_br_swa, w_br_mla, w_br_ret, w_o, ffn2_norm, ffn2_w_gate, ffn2_w_up, ffn2_w_down):
    batch, seq_in, _ = x.shape
    seq = seq_in + BLOCK
    nb = seq // BLOCK

    meta = jnp.broadcast_to(meta_tokens[None].astype(x.dtype), (batch, N_META, D_MODEL))
    h = jnp.concatenate([jnp.zeros((batch, PAD, D_MODEL), x.dtype), meta, x], axis=1)
    h = h.reshape(batch * seq, D_MODEL)
    pos = (jnp.arange(seq) - PAD).astype(F32)

    w_z = _repack_w_in(w_in)

    qk_dim = MLA_NOPE_DIM + MLA_ROPE_DIM
    w_uq_pad = jnp.pad(mla_w_uq.reshape(DEPTH, MLA_Q_RANK, MLA_HEADS, qk_dim),
                       ((0, 0), (0, 0), (0, 0), (0, MLA_QK_PAD - qk_dim)))
    w_uq_pad = w_uq_pad.reshape(DEPTH, MLA_Q_RANK, MLA_HEADS * MLA_QK_PAD).astype(BF16)
    rope_fill = jnp.zeros((DEPTH, BLOCK - MLA_ROPE_DIM), F32)
    qr_norm_pad = jnp.concatenate([mla_qr_norm, rope_fill], axis=1)
    kr_norm_pad = jnp.concatenate([mla_kr_norm, rope_fill], axis=1)

    cos_m, sin_m = _rope_tables(pos, MLA_ROPE_DIM, BLOCK)
    cos_r, sin_r = _rope_tables(pos, RET_KEY_DIM, BLOCK)

    log_gamma = jnp.log(1.0 - 2.0 ** (-5.0 - jnp.arange(RET_HEADS, dtype=F32)))
    idx = jnp.arange(BLOCK, dtype=F32)
    diff = idx[:, None] - idx[None, :]
    decay = jnp.where(diff[None] >= 0, jnp.exp(jnp.maximum(diff, 0.0)[None] * log_gamma[:, None, None]), 0.0)
    zeta = jnp.exp((BLOCK - 1.0 - idx)[None, :] * log_gamma[:, None])
    xi = jnp.exp((idx + 1.0)[None, :] * log_gamma[:, None])
    zeta_b = jnp.broadcast_to(zeta[:, :, None], (RET_HEADS, BLOCK, RET_KEY_DIM))
    xi_b = jnp.broadcast_to(xi[:, :, None], (RET_HEADS, BLOCK, RET_KEY_DIM))
    chunk_decay = tuple(float(np.exp(BLOCK * np.log(1.0 - 2.0 ** (-5.0 - hh)))) for hh in range(RET_HEADS))

    for l in range(DEPTH):
        h = _ffn(h, ffn1_norm, ffn1_w_gate, ffn1_w_up, ffn1_w_down, l)
        z = _inproj(h, mix_norm, w_z, l)
        sink_col = jnp.repeat(swa_sinks[l].astype(F32).reshape(SWA_KV_HEADS, SWA_GROUP), BLOCK, axis=1)
        o_a = _swa(z, swa_q_norm[l], swa_k_norm[l], sink_col[:, None, :], nb, batch)
        q, k, v = _mla_up(z, mla_q_a_norm[l], mla_kv_a_norm[l], w_uq_pad, mla_w_ukv, mla_qn_norm[l],
                          qr_norm_pad[l:l + 1], mla_kn_norm[l], kr_norm_pad[l:l + 1], cos_m, sin_m, l, seq)
        o_b = _mla_attn(q, k, v, seq, batch)
        o_c = _retention(z, cos_r, sin_r, decay, zeta_b, xi_b, ret_gn[l], chunk_decay, nb, batch)
        merged = _merge(z, o_a, o_b, o_c, w_br_swa, w_br_mla, w_br_ret, l)
        h = _outproj(merged, w_o, h, l)
        h = _ffn(h, ffn2_norm, ffn2_w_gate, ffn2_w_up, ffn2_w_down, l)
    return h.reshape(batch, seq, D_MODEL)[:, BLOCK:]
```

```python
import functools

import jax
import jax.numpy as jnp
import numpy as np
from jax import lax
from jax.experimental import pallas as pl
from jax.experimental.pallas import tpu as pltpu

F32 = jnp.float32
BF16 = jnp.bfloat16

D_MODEL = 2048
DEPTH = 4
N_META = 16
BLOCK = 128
PAD = BLOCK - N_META
EPS = 1e-6
NEG_INF = -1e30
ROPE_BASE = 10000.0
HALF_STEP = 0.5
LOG2_E = 1.4426950408889634

SWA_HEADS = 16
SWA_KV_HEADS = 2
SWA_HEAD_DIM = 64
SWA_GROUP = SWA_HEADS // SWA_KV_HEADS
WINDOW = 128

MLA_HEADS = 8
MLA_Q_RANK = 512
MLA_KV_RANK = 512
MLA_NOPE_DIM = 128
MLA_ROPE_DIM = 64
MLA_V_DIM = 128
MLA_QK_PAD = 256

RET_HEADS = 4
RET_KEY_DIM = 128
RET_VAL_DIM = 256

D_FF = 5632
N_BRANCH = 3

SWA_OUT = SWA_HEADS * SWA_HEAD_DIM
SWA_KV_W = SWA_KV_HEADS * SWA_HEAD_DIM
MLA_OUT = MLA_HEADS * MLA_V_DIM
RET_QK_W = RET_HEADS * RET_KEY_DIM
RET_OUT = RET_HEADS * RET_VAL_DIM

A_SWA_Q = 0
A_SWA_K = A_SWA_Q + SWA_OUT
A_SWA_V = A_SWA_K + SWA_KV_W
A_MLA_CQ = A_SWA_V + SWA_KV_W
A_MLA_CKV = A_MLA_CQ + MLA_Q_RANK
IN_KR = A_MLA_CKV + MLA_KV_RANK
IN_B = IN_KR + MLA_ROPE_DIM
B_RET_Q = 0
B_RET_K = B_RET_Q + RET_QK_W
B_RET_V = B_RET_K + RET_QK_W
B_RET_G = B_RET_V + RET_OUT
B_GATE = B_RET_G + RET_OUT
B_WIDTH = B_GATE + N_BRANCH * D_MODEL
IN_WIDTH = IN_B + B_WIDTH

VMEM_LIMIT = 56 * 1024 * 1024


def _params(semantics):
    return pltpu.CompilerParams(dimension_semantics=semantics, vmem_limit_bytes=VMEM_LIMIT)


def _rms(x, g):
    return x * lax.rsqrt(jnp.mean(x * x, axis=-1, keepdims=True) + EPS) * g


def _sigmoid(x):
    return 1.0 / (1.0 + jnp.exp(-x))


def _dot(a, b):
    return jnp.dot(a, b, preferred_element_type=F32)


def _dot_nt(a, b):
    return lax.dot_general(a, b, (((1,), (1,)), ((), ())), preferred_element_type=F32)


def _ffn_body(x_ref, g_ref, wg_ref, wu_ref, wd_ref, o_ref, xn_ref):
    @pl.when(pl.program_id(1) == 0)
    def _():
        x = x_ref[...]
        xn_ref[...] = _rms(x, g_ref[...]).astype(BF16)
        o_ref[...] = x

    xn = xn_ref[...]
    gate = _dot(xn, wg_ref[...].astype(BF16))
    up = _dot(xn, wu_ref[...].astype(BF16))
    act = (gate * _sigmoid(gate) * up).astype(BF16)
    o_ref[...] += HALF_STEP * _dot(act, wd_ref[...].astype(BF16))


def _ffn(h, gain, w_gate, w_up, w_down, layer, *, tm=1056, tf=256):
    m = h.shape[0]
    return pl.pallas_call(
        _ffn_body,
        grid=(m // tm, D_FF // tf),
        in_specs=[
            pl.BlockSpec((tm, D_MODEL), lambda i, f: (i, 0)),
            pl.BlockSpec((None, 1, D_MODEL), lambda i, f: (layer, 0, 0)),
            pl.BlockSpec((None, D_MODEL, tf), lambda i, f: (layer, 0, f)),
            pl.BlockSpec((None, D_MODEL, tf), lambda i, f: (layer, 0, f)),
            pl.BlockSpec((None, tf, D_MODEL), lambda i, f: (layer, f, 0)),
        ],
        out_specs=pl.BlockSpec((tm, D_MODEL), lambda i, f: (i, 0)),
        out_shape=jax.ShapeDtypeStruct((m, D_MODEL), F32),
        scratch_shapes=[pltpu.VMEM((tm, D_MODEL), BF16)],
        compiler_params=_params(("parallel", "arbitrary")),
        name="ffn",
    )(h, gain.reshape(DEPTH, 1, D_MODEL), w_gate, w_up, w_down)


def _inproj_a_body(x_ref, g_ref, w_ref, wkr_ref, za_ref, zk_ref, xn_ref):
    @pl.when(pl.program_id(1) == 0)
    def _():
        xn = _rms(x_ref[...], g_ref[...]).astype(BF16)
        xn_ref[...] = xn
        kr = _dot_nt(xn, wkr_ref[...].astype(BF16))
        zk_ref[...] = jnp.concatenate([kr, jnp.zeros_like(kr)], axis=1).astype(BF16)

    za_ref[...] = _dot_nt(xn_ref[...], w_ref[...].astype(BF16)).astype(BF16)


def _inproj_a(h, gain, w_in_t, layer, *, tm=1056, tn=768):
    m = h.shape[0]
    return pl.pallas_call(
        _inproj_a_body,
        grid=(m // tm, IN_KR // tn),
        in_specs=[
            pl.BlockSpec((tm, D_MODEL), lambda i, j: (i, 0)),
            pl.BlockSpec((None, 1, D_MODEL), lambda i, j: (layer, 0, 0)),
            pl.BlockSpec((None, tn, D_MODEL), lambda i, j: (layer, j, 0)),
            pl.BlockSpec((None, MLA_ROPE_DIM, D_MODEL), lambda i, j: (layer, IN_KR // MLA_ROPE_DIM, 0)),
        ],
        out_specs=[
            pl.BlockSpec((tm, tn), lambda i, j: (i, j)),
            pl.BlockSpec((tm, BLOCK), lambda i, j: (i, 0)),
            pl.BlockSpec((tm, D_MODEL), lambda i, j: (i, 0)),
        ],
        out_shape=[
            jax.ShapeDtypeStruct((m, IN_KR), BF16),
            jax.ShapeDtypeStruct((m, BLOCK), BF16),
            jax.ShapeDtypeStruct((m, D_MODEL), BF16),
        ],
        compiler_params=_params(("parallel", "arbitrary")),
        name="inproj_a",
    )(h, gain.reshape(DEPTH, 1, D_MODEL), w_in_t, w_in_t)


def _inproj_b_body(xn_ref, w_ref, o_ref):
    o_ref[...] = _dot_nt(xn_ref[...], w_ref[0].astype(BF16)).astype(BF16)


def _inproj_b(xn, w_in_t, layer, *, tm=1056, tn=768):
    m = xn.shape[0]
    return pl.pallas_call(
        _inproj_b_body,
        grid=(m // tm, B_WIDTH // tn),
        in_specs=[
            pl.BlockSpec((tm, D_MODEL), lambda i, j: (i, 0)),
            pl.BlockSpec((pl.Element(1), pl.Element(tn), pl.Element(D_MODEL)),
                         lambda i, j: (layer, pl.multiple_of(IN_B + j * tn, 8), 0)),
        ],
        out_specs=pl.BlockSpec((tm, tn), lambda i, j: (i, j)),
        out_shape=jax.ShapeDtypeStruct((m, B_WIDTH), BF16),
        compiler_params=_params(("parallel", "arbitrary")),
        name="inproj_b",
    )(xn, w_in_t)


def _swa_body(q_ref, kc_ref, kp_ref, km_ref, vc_ref, vp_ref, vm_ref, qn_ref, kn_ref, sink_ref, o_ref):
    n = pl.program_id(1)
    j = lax.broadcasted_iota(jnp.int32, (BLOCK, BLOCK), 0)
    i_loc = lax.broadcasted_iota(jnp.int32, (BLOCK, BLOCK), 1)
    ok_cur = (j <= i_loc) & (n * BLOCK + j >= PAD)
    ok_prev = (j > i_loc) & ((n - 1) * BLOCK + j >= PAD)
    ok_meta = (j >= PAD) & (n * BLOCK + i_loc - j >= WINDOW)

    def mask_rows(s_blk, ok):
        return jnp.concatenate(
            [jnp.where(ok, s_blk[:, h * BLOCK:(h + 1) * BLOCK], NEG_INF) for h in range(SWA_GROUP)], axis=1)

    q_t = [q_ref[:, a * BLOCK:(a + 1) * BLOCK].astype(F32).T for a in range(SWA_HEADS // 2)]
    v_t = jnp.concatenate([r[...].astype(F32).T for r in (vm_ref, vp_ref, vc_ref)], axis=1)
    k_gain = kn_ref[...] * qn_ref[...] * (SWA_HEAD_DIM ** -0.5)

    out_t = []
    for g in range(SWA_KV_HEADS):
        lo = g * SWA_HEAD_DIM
        hi = lo + SWA_HEAD_DIM
        heads = [g * SWA_GROUP + h for h in range(SWA_GROUP)]
        qt = jnp.concatenate(
            [q_t[hd // 2][(hd % 2) * SWA_HEAD_DIM:(hd % 2 + 1) * SWA_HEAD_DIM, :] for hd in heads], axis=1)
        q_rinv = lax.rsqrt(jnp.sum(qt * qt, axis=0, keepdims=True) * (1.0 / SWA_HEAD_DIM) + EPS)
        k_all = jnp.concatenate(
            [_rms(r[:, lo:hi].astype(F32), k_gain) for r in (km_ref, kp_ref, kc_ref)], axis=0).astype(BF16)
        s = _dot(k_all, qt.astype(BF16)) * q_rinv
        s = jnp.concatenate([mask_rows(s[:BLOCK], ok_meta), mask_rows(s[BLOCK:2 * BLOCK], ok_prev),
                             mask_rows(s[2 * BLOCK:], ok_cur)], axis=0)
        sink = sink_ref[g]
        mx = jnp.maximum(jnp.max(s, axis=0, keepdims=True), sink)
        p = jnp.exp(s - mx)
        den = jnp.sum(p, axis=0, keepdims=True) + jnp.exp(sink - mx)
        o_t = _dot(v_t[lo:hi].astype(BF16), p.astype(BF16)) / den
        out_t.extend(o_t[:, h * BLOCK:(h + 1) * BLOCK] for h in range(SWA_GROUP))
    for a in range(SWA_HEADS // 2):
        pair = jnp.concatenate([out_t[2 * a], out_t[2 * a + 1]], axis=0)
        o_ref[:, a * BLOCK:(a + 1) * BLOCK] = pair.T.astype(BF16)


def _swa(z, q_norm, k_norm, sink_col, nb, batch):
    m = z.shape[0]
    kcol = A_SWA_K // SWA_KV_W
    vcol = A_SWA_V // SWA_KV_W

    def cur(c):
        return pl.BlockSpec((BLOCK, SWA_KV_W), lambda b, n: (b * nb + n, c))

    def prev(c):
        return pl.BlockSpec((BLOCK, SWA_KV_W), lambda b, n: (b * nb + jnp.maximum(n - 1, 0), c))

    def meta(c):
        return pl.BlockSpec((BLOCK, SWA_KV_W), lambda b, n: (b * nb, c))

    return pl.pallas_call(
        _swa_body,
        grid=(batch, nb),
        in_specs=[
            pl.BlockSpec((BLOCK, SWA_OUT), lambda b, n: (b * nb + n, A_SWA_Q // SWA_OUT)),
            cur(kcol), prev(kcol), meta(kcol),
            cur(vcol), prev(vcol), meta(vcol),
            pl.BlockSpec((1, SWA_HEAD_DIM), lambda b, n: (0, 0)),
            pl.BlockSpec((1, SWA_HEAD_DIM), lambda b, n: (0, 0)),
            pl.BlockSpec((SWA_KV_HEADS, 1, SWA_GROUP * BLOCK), lambda b, n: (0, 0, 0)),
        ],
        out_specs=pl.BlockSpec((BLOCK, SWA_OUT), lambda b, n: (b * nb + n, 0)),
        out_shape=jax.ShapeDtypeStruct((m, SWA_OUT), BF16),
        compiler_params=_params(("parallel", "arbitrary")),
        name="swa",
    )(z, z, z, z, z, z, z, q_norm.reshape(1, -1), k_norm.reshape(1, -1), sink_col)


def _rope_rows(x, cos, sin, half):
    width = x.shape[-1]
    lane = lax.broadcasted_iota(jnp.int32, x.shape, 1)
    rot = jnp.where(lane < half, pltpu.roll(x, width - half, 1), pltpu.roll(x, half, 1))
    return x * cos + rot * sin


def _rope_key_norm(r, gain):
    return r * lax.rsqrt(jnp.sum(r * r, axis=-1, keepdims=True) * (1.0 / MLA_ROPE_DIM) + EPS) * gain


def _mla_up_body(cq0_ref, cq1_ref, ckv0_ref, ckv1_ref, kr_ref, qa_ref, kva_ref, wq_ref, wkv_ref, qn_ref, qr_ref,
                 kn_ref, krn_ref, cos_ref, sin_ref, q_ref, k_ref, v_ref):
    cos = cos_ref[...]
    sin = sin_ref[...]
    half = MLA_ROPE_DIM // 2
    scale = (MLA_NOPE_DIM + MLA_ROPE_DIM) ** -0.5 * LOG2_E
    c_q = jnp.concatenate([cq0_ref[...], cq1_ref[...]], axis=1).astype(F32)
    c_kv = jnp.concatenate([ckv0_ref[...], ckv1_ref[...]], axis=1).astype(F32)
    yq = _dot(_rms(c_q, qa_ref[...]).astype(BF16), wq_ref[...])
    ykv = _dot(_rms(c_kv, kva_ref[...]).astype(BF16), wkv_ref[...].astype(BF16))
    k_rope = _rope_rows(_rope_key_norm(kr_ref[...].astype(F32), krn_ref[...]), cos, sin, half).astype(BF16)
    for h in range(MLA_HEADS):
        lo = h * MLA_QK_PAD
        nope = _rms(yq[:, lo:lo + MLA_NOPE_DIM], qn_ref[...])
        r = _rope_rows(_rope_key_norm(yq[:, lo + MLA_NOPE_DIM:lo + MLA_QK_PAD], qr_ref[...]), cos, sin, half)
        q_ref[:, lo:lo + MLA_NOPE_DIM] = (nope * scale).astype(BF16)
        q_ref[:, lo + MLA_NOPE_DIM:lo + MLA_QK_PAD] = (r * scale).astype(BF16)
        k_ref[:, lo:lo + MLA_NOPE_DIM] = _rms(ykv[:, lo:lo + MLA_NOPE_DIM], kn_ref[...]).astype(BF16)
        k_ref[:, lo + MLA_NOPE_DIM:lo + MLA_QK_PAD] = k_rope
        v_ref[:, h * MLA_V_DIM:(h + 1) * MLA_V_DIM] = ykv[:, lo + MLA_NOPE_DIM:lo + MLA_QK_PAD].astype(BF16)


def _mla_up(z_a, z_kr, qa_norm, kva_norm, w_uq_pad, w_ukv, qn_norm, qr_norm_pad, kn_norm, kr_norm_pad, cos_t, sin_t,
            layer, seq, *, tm=528):
    m = z_a.shape[0]
    seq_tiles = seq // tm
    kv_w = MLA_NOPE_DIM + MLA_V_DIM
    half_rank = MLA_Q_RANK // 2

    def row(width, col):
        return pl.BlockSpec((tm, width), lambda i: (i, col))

    def vec(width):
        return pl.BlockSpec((1, width), lambda i: (0, 0))

    return pl.pallas_call(
        _mla_up_body,
        grid=(m // tm,),
        in_specs=[
            row(half_rank, A_MLA_CQ // half_rank), row(half_rank, A_MLA_CQ // half_rank + 1),
            row(half_rank, A_MLA_CKV // half_rank), row(half_rank, A_MLA_CKV // half_rank + 1),
            row(BLOCK, 0),
            vec(MLA_Q_RANK), vec(MLA_KV_RANK),
            pl.BlockSpec((None, MLA_Q_RANK, MLA_HEADS * MLA_QK_PAD), lambda i: (layer, 0, 0)),
            pl.BlockSpec((None, MLA_KV_RANK, MLA_HEADS * kv_w), lambda i: (layer, 0, 0)),
            vec(MLA_NOPE_DIM), vec(BLOCK), vec(MLA_NOPE_DIM), vec(BLOCK),
            pl.BlockSpec((tm, BLOCK), lambda i: (i % seq_tiles, 0)),
            pl.BlockSpec((tm, BLOCK), lambda i: (i % seq_tiles, 0)),
        ],
        out_specs=[row(MLA_HEADS * MLA_QK_PAD, 0), row(MLA_HEADS * MLA_QK_PAD, 0), row(MLA_OUT, 0)],
        out_shape=[
            jax.ShapeDtypeStruct((m, MLA_HEADS * MLA_QK_PAD), BF16),
            jax.ShapeDtypeStruct((m, MLA_HEADS * MLA_QK_PAD), BF16),
            jax.ShapeDtypeStruct((m, MLA_OUT), BF16),
        ],
        compiler_params=_params(("parallel",)),
        name="mla_up",
    )(z_a, z_a, z_a, z_a, z_kr, qa_norm.reshape(1, -1), kva_norm.reshape(1, -1), w_uq_pad, w_ukv,
      qn_norm.reshape(1, -1), qr_norm_pad, kn_norm.reshape(1, -1), kr_norm_pad, cos_t, sin_t)


def _mla_attn_body(q_ref, k_ref, v_ref, o_ref, m_ref, acc_ref, *, t, hb):
    i = pl.program_id(2)
    m_ref[...] = jnp.full(m_ref.shape, NEG_INF, F32)
    acc_ref[...] = jnp.zeros(acc_ref.shape, F32)

    def tile(blk, nblk, masked):
        width = nblk * t
        off = blk * t if isinstance(blk, int) else pl.multiple_of(blk * t, t)
        if masked:
            row = i * t + lax.broadcasted_iota(jnp.int32, (t, width), 0)
            col = blk * t + lax.broadcasted_iota(jnp.int32, (t, width), 1)
            ok = (col <= row) & (col >= PAD)
        for h in range(hb):
            q = q_ref[:, h * MLA_QK_PAD:(h + 1) * MLA_QK_PAD]
            kb = k_ref[pl.ds(off, width), h * MLA_QK_PAD:(h + 1) * MLA_QK_PAD]
            vb = v_ref[pl.ds(off, width), h * MLA_V_DIM:(h + 1) * MLA_V_DIM]
            s = _dot_nt(q, kb)
            if masked:
                s = jnp.where(ok, s, NEG_INF)
            m_run = m_ref[h]
            m_new = jnp.maximum(m_run, jnp.max(s, axis=-1, keepdims=True))
            alpha = jnp.exp2(m_run - m_new)
            p = jnp.concatenate(
                [jnp.exp2(s[:, c * BLOCK:(c + 1) * BLOCK] - m_new) for c in range(width // BLOCK)], axis=1)
            v_aug = jnp.concatenate([vb, jnp.ones_like(vb)], axis=1)
            acc_ref[h] = jnp.concatenate([alpha, alpha], axis=1) * acc_ref[h] + _dot(p.astype(BF16), v_aug)
            m_ref[h] = m_new

    @pl.when(i >= 2)
    def _():
        tile(0, 2, True)

    def body(jj, carry):
        tile(2 * jj, 2, False)
        return carry

    lax.fori_loop(1, i // 2, body, 0)

    @pl.when(i % 2 == 1)
    def _():
        tile(i - 1, 1, True)

    tile(i, 1, True)

    for h in range(hb):
        acc = acc_ref[h]
        o_ref[:, h * MLA_V_DIM:(h + 1) * MLA_V_DIM] = (acc[:, :MLA_V_DIM] / acc[:, MLA_V_DIM:]).astype(BF16)


def _mla_attn(q, k, v, seq, batch, *, t=384, hb=4):
    m = q.shape[0]
    nq = seq // t
    return pl.pallas_call(
        functools.partial(_mla_attn_body, t=t, hb=hb),
        grid=(batch, MLA_HEADS // hb, nq),
        in_specs=[
            pl.BlockSpec((t, hb * MLA_QK_PAD), lambda b, h, i: (b * nq + i, h)),
            pl.BlockSpec((seq, hb * MLA_QK_PAD), lambda b, h, i: (b, h)),
            pl.BlockSpec((seq, hb * MLA_V_DIM), lambda b, h, i: (b, h)),
        ],
        out_specs=pl.BlockSpec((t, hb * MLA_V_DIM), lambda b, h, i: (b * nq + i, h)),
        out_shape=jax.ShapeDtypeStruct((m, MLA_OUT), BF16),
        scratch_shapes=[pltpu.VMEM((hb, t, BLOCK), F32), pltpu.VMEM((hb, t, 2 * MLA_V_DIM), F32)],
        compiler_params=_params(("parallel", "parallel", "arbitrary")),
        name="mla_attn",
    )(q, k, v)


def _ret_body(q_ref, k_ref, v_ref, g_ref, cos_ref, sin_ref, dec_ref, zeta_ref, xi_ref, gn_ref, o_ref,
              state_ref, *, chunk_decay):
    n = pl.program_id(1)

    @pl.when(n == 0)
    def _():
        state_ref[...] = jnp.zeros_like(state_ref)

    cos = cos_ref[...]
    sin = sin_ref[...]
    valid = (n * BLOCK + lax.broadcasted_iota(jnp.int32, (BLOCK, 1), 0) >= PAD).astype(F32)
    for h in range(RET_HEADS):
        ks = slice(h * RET_KEY_DIM, (h + 1) * RET_KEY_DIM)
        vs = slice(h * RET_VAL_DIM, (h + 1) * RET_VAL_DIM)
        qh = _rope_rows(q_ref[:, ks].astype(F32), cos, sin, RET_KEY_DIM // 2)
        kh = _rope_rows(k_ref[:, ks].astype(F32), cos, sin, RET_KEY_DIM // 2) * (RET_KEY_DIM ** -0.5) * valid
        vb = v_ref[:, vs]
        s = _dot_nt(qh.astype(BF16), kh.astype(BF16)) * dec_ref[h]
        inner = _dot(s.astype(BF16), vb)
        prev = state_ref[h]
        cross = _dot((qh * xi_ref[h]).astype(BF16), prev.astype(BF16))
        kz_t = (kh * zeta_ref[h]).T.astype(BF16)
        state_ref[h] = prev * chunk_decay[h] + _dot(kz_t, vb)
        o = inner + cross
        mu = jnp.mean(o, axis=-1, keepdims=True)
        d = o - mu
        y = d * lax.rsqrt(jnp.mean(d * d, axis=-1, keepdims=True) + EPS)
        gate = g_ref[:, vs].astype(F32)
        o_ref[:, vs] = (y * gn_ref[:, vs] * (gate * _sigmoid(gate))).astype(BF16)


def _retention(z, cos_t, sin_t, decay, zeta_b, xi_b, gn, chunk_decay, nb, batch):
    m = z.shape[0]
    tab = pl.BlockSpec((RET_HEADS, BLOCK, BLOCK), lambda b, n: (0, 0, 0))
    return pl.pallas_call(
        functools.partial(_ret_body, chunk_decay=chunk_decay),
        grid=(batch, nb),
        in_specs=[
            pl.BlockSpec((BLOCK, RET_QK_W), lambda b, n: (b * nb + n, B_RET_Q // RET_QK_W)),
            pl.BlockSpec((BLOCK, RET_QK_W), lambda b, n: (b * nb + n, B_RET_K // RET_QK_W)),
            pl.BlockSpec((BLOCK, RET_OUT), lambda b, n: (b * nb + n, B_RET_V // RET_OUT)),
            pl.BlockSpec((BLOCK, RET_OUT), lambda b, n: (b * nb + n, B_RET_G // RET_OUT)),
            pl.BlockSpec((BLOCK, BLOCK), lambda b, n: (n, 0)),
            pl.BlockSpec((BLOCK, BLOCK), lambda b, n: (n, 0)),
            tab, tab, tab,
            pl.BlockSpec((1, RET_OUT), lambda b, n: (0, 0)),
        ],
        out_specs=pl.BlockSpec((BLOCK, RET_OUT), lambda b, n: (b * nb + n, 0)),
        out_shape=jax.ShapeDtypeStruct((m, RET_OUT), BF16),
        scratch_shapes=[pltpu.VMEM((RET_HEADS, RET_KEY_DIM, RET_VAL_DIM), F32)],
        compiler_params=_params(("parallel", "arbitrary")),
        name="retention",
    )(z, z, z, z, cos_t, sin_t, decay, zeta_b, xi_b, gn.reshape(1, -1))


def _merge_body(oa_ref, ob_ref, oc_ref, wa_ref, wb_ref, wc_ref, ga_ref, gb_ref, gc_ref, o_ref):
    def branch(o, w, g):
        return _sigmoid(g[...].astype(F32)) * _dot(o[...], w[...].astype(BF16))

    o_ref[...] = (branch(oa_ref, wa_ref, ga_ref) + branch(ob_ref, wb_ref, gb_ref)
                  + branch(oc_ref, wc_ref, gc_ref)).astype(BF16)


def _merge(z, o_a, o_b, o_c, w_a, w_b, w_c, layer, *, tm=1056, tn=512):
    m = z.shape[0]
    nt = D_MODEL // tn

    def act(width):
        return pl.BlockSpec((tm, width), lambda i, j: (i, 0))

    def wgt(width):
        return pl.BlockSpec((None, width, tn), lambda i, j: (layer, 0, j))

    def gate(br):
        return pl.BlockSpec((tm, tn), lambda i, j: (i, B_GATE // tn + br * nt + j))

    return pl.pallas_call(
        _merge_body,
        grid=(m // tm, nt),
        in_specs=[act(SWA_OUT), act(MLA_OUT), act(RET_OUT), wgt(SWA_OUT), wgt(MLA_OUT), wgt(RET_OUT),
                  gate(0), gate(1), gate(2)],
        out_specs=pl.BlockSpec((tm, tn), lambda i, j: (i, j)),
        out_shape=jax.ShapeDtypeStruct((m, D_MODEL), BF16),
        compiler_params=_params(("parallel", "arbitrary")),
        name="merge",
    )(o_a, o_b, o_c, w_a, w_b, w_c, z, z, z)


def _outproj_body(x_ref, w_ref, r_ref, o_ref):
    o_ref[...] = r_ref[...] + _dot(x_ref[...], w_ref[...].astype(BF16))


def _outproj(x, w, res, layer, *, tm=1056, tn=512):
    m = x.shape[0]
    return pl.pallas_call(
        _outproj_body,
        grid=(m // tm, D_MODEL // tn),
        in_specs=[
            pl.BlockSpec((tm, D_MODEL), lambda i, j: (i, 0)),
            pl.BlockSpec((None, D_MODEL, tn), lambda i, j: (layer, 0, j)),
            pl.BlockSpec((tm, tn), lambda i, j: (i, j)),
        ],
        out_specs=pl.BlockSpec((tm, tn), lambda i, j: (i, j)),
        out_shape=jax.ShapeDtypeStruct((m, D_MODEL), F32),
        compiler_params=_params(("parallel", "arbitrary")),
        name="outproj",
    )(x, w, res)


def _rope_tables(pos, dim, width):
    half = dim // 2
    inv_freq = ROPE_BASE ** (-jnp.arange(half, dtype=F32) / half)
    ang = pos[:, None] * inv_freq[None, :]
    cos = jnp.cos(ang)
    sin = jnp.sin(ang)
    fill = jnp.zeros((pos.shape[0], width - dim), F32)
    return (jnp.concatenate([cos, cos, fill], axis=1), jnp.concatenate([-sin, sin, fill], axis=1))


def kernel(x, meta_tokens, ffn1_norm, ffn1_w_gate, ffn1_w_up, ffn1_w_down, mix_norm, w_in, swa_q_norm, swa_k_norm, swa_sinks, mla_q_a_norm, mla_w_uq, mla_kv_a_norm, mla_w_ukv, mla_qn_norm, mla_qr_norm, mla_kn_norm, mla_kr_norm, ret_gn, w_br_swa, w_br_mla, w_br_ret, w_o, ffn2_norm, ffn2_w_gate, ffn2_w_up, ffn2_w_down):
    batch, seq_in, _ = x.shape
    seq = seq_in + BLOCK
    nb = seq // BLOCK

    meta = jnp.broadcast_to(meta_tokens[None].astype(x.dtype), (batch, N_META, D_MODEL))
    h = jnp.concatenate([jnp.zeros((batch, PAD, D_MODEL), x.dtype), meta, x], axis=1)
    h = h.reshape(batch * seq, D_MODEL)
    pos = (jnp.arange(seq) - PAD).astype(F32)

    w_in_t = jnp.swapaxes(w_in, 1, 2)

    qk_dim = MLA_NOPE_DIM + MLA_ROPE_DIM
    w_uq_pad = jnp.pad(mla_w_uq.reshape(DEPTH, MLA_Q_RANK, MLA_HEADS, qk_dim),
                       ((0, 0), (0, 0), (0, 0), (0, MLA_QK_PAD - qk_dim)))
    w_uq_pad = w_uq_pad.reshape(DEPTH, MLA_Q_RANK, MLA_HEADS * MLA_QK_PAD).astype(BF16)
    rope_fill = jnp.zeros((DEPTH, BLOCK - MLA_ROPE_DIM), F32)
    qr_norm_pad = jnp.concatenate([mla_qr_norm, rope_fill], axis=1)
    kr_norm_pad = jnp.concatenate([mla_kr_norm, rope_fill], axis=1)

    cos_m, sin_m = _rope_tables(pos, MLA_ROPE_DIM, BLOCK)
    cos_r, sin_r = _rope_tables(pos, RET_KEY_DIM, BLOCK)

    log_gamma = jnp.log(1.0 - 2.0 ** (-5.0 - jnp.arange(RET_HEADS, dtype=F32)))
    idx = jnp.arange(BLOCK, dtype=F32)
    diff = idx[:, None] - idx[None, :]
    decay = jnp.where(diff[None] >= 0, jnp.exp(jnp.maximum(diff, 0.0)[None] * log_gamma[:, None, None]), 0.0)
    zeta = jnp.exp((BLOCK - 1.0 - idx)[None, :] * log_gamma[:, None])
    xi = jnp.exp((idx + 1.0)[None, :] * log_gamma[:, None])
    zeta_b = jnp.broadcast_to(zeta[:, :, None], (RET_HEADS, BLOCK, RET_KEY_DIM))
    xi_b = jnp.broadcast_to(xi[:, :, None], (RET_HEADS, BLOCK, RET_KEY_DIM))
    chunk_decay = tuple(float(np.exp(BLOCK * np.log(1.0 - 2.0 ** (-5.0 - hh)))) for hh in range(RET_HEADS))

    for l in range(DEPTH):
        h = _ffn(h, ffn1_norm, ffn1_w_gate, ffn1_w_up, ffn1_w_down, l)
        z_a, z_kr, hn = _inproj_a(h, mix_norm, w_in_t, l)
        z_b = _inproj_b(hn, w_in_t, l)
        sink_col = jnp.repeat(swa_sinks[l].astype(F32).reshape(SWA_KV_HEADS, SWA_GROUP), BLOCK, axis=1)
        o_a = _swa(z_a, swa_q_norm[l], swa_k_norm[l], sink_col[:, None, :], nb, batch)
        q, k, v = _mla_up(z_a, z_kr, mla_q_a_norm[l], mla_kv_a_norm[l], w_uq_pad, mla_w_ukv, mla_qn_norm[l],
                          qr_norm_pad[l:l + 1], mla_kn_norm[l], kr_norm_pad[l:l + 1], cos_m, sin_m, l, seq)
        o_b = _mla_attn(q, k, v, seq, batch)
        o_c = _retention(z_b, cos_r, sin_r, decay, zeta_b, xi_b, ret_gn[l], chunk_decay, nb, batch)
        merged = _merge(z_b, o_a, o_b, o_c, w_br_swa, w_br_mla, w_br_ret, l)
        h = _outproj(merged, w_o, h, l)
        h = _ffn(h, ffn2_norm, ffn2_w_gate, ffn2_w_up, ffn2_w_down, l)
    return h.reshape(batch, seq, D_MODEL)[:, BLOCK:]
```

```python
import functools

import jax
import jax.numpy as jnp
import numpy as np
from jax import lax
from jax.experimental import pallas as pl
from jax.experimental.pallas import tpu as pltpu

F32 = jnp.float32
BF16 = jnp.bfloat16

D_MODEL = 2048
DEPTH = 4
N_META = 16
BLOCK = 128
PAD = BLOCK - N_META
EPS = 1e-6
NEG_INF = -1e30
ROPE_BASE = 10000.0
HALF_STEP = 0.5
LOG2_E = 1.4426950408889634

SWA_HEADS = 16
SWA_KV_HEADS = 2
SWA_HEAD_DIM = 64
SWA_GROUP = SWA_HEADS // SWA_KV_HEADS
WINDOW = 128

MLA_HEADS = 8
MLA_Q_RANK = 512
MLA_KV_RANK = 512
MLA_NOPE_DIM = 128
MLA_ROPE_DIM = 64
MLA_V_DIM = 128
MLA_QK_PAD = 256

RET_HEADS = 4
RET_KEY_DIM = 128
RET_VAL_DIM = 256

D_FF = 5632
N_BRANCH = 3

SWA_OUT = SWA_HEADS * SWA_HEAD_DIM
SWA_KV_W = SWA_KV_HEADS * SWA_HEAD_DIM
MLA_OUT = MLA_HEADS * MLA_V_DIM
RET_QK_W = RET_HEADS * RET_KEY_DIM
RET_OUT = RET_HEADS * RET_VAL_DIM

A_SWA_Q = 0
A_SWA_K = A_SWA_Q + SWA_OUT
A_SWA_V = A_SWA_K + SWA_KV_W
A_MLA_CQ = A_SWA_V + SWA_KV_W
A_MLA_CKV = A_MLA_CQ + MLA_Q_RANK
IN_KR = A_MLA_CKV + MLA_KV_RANK
IN_B = IN_KR + MLA_ROPE_DIM
B_RET_Q = 0
B_RET_K = B_RET_Q + RET_QK_W
B_RET_V = B_RET_K + RET_QK_W
B_RET_G = B_RET_V + RET_OUT
B_GATE = B_RET_G + RET_OUT
B_WIDTH = B_GATE + N_BRANCH * D_MODEL
IN_WIDTH = IN_B + B_WIDTH

VMEM_LIMIT = 56 * 1024 * 1024


def _params(semantics):
    return pltpu.CompilerParams(dimension_semantics=semantics, vmem_limit_bytes=VMEM_LIMIT)


def _rms(x, g):
    return x * lax.rsqrt(jnp.mean(x * x, axis=-1, keepdims=True) + EPS) * g


def _sigmoid(x):
    return 1.0 / (1.0 + jnp.exp(-x))


def _dot(a, b):
    return jnp.dot(a, b, preferred_element_type=F32)


def _dot_nt(a, b):
    return lax.dot_general(a, b, (((1,), (1,)), ((), ())), preferred_element_type=F32)


def _ffn_body(x_ref, g_ref, wg_ref, wu_ref, wd_ref, o_ref, xn_ref):
    @pl.when(pl.program_id(1) == 0)
    def _():
        x = x_ref[...]
        xn_ref[...] = _rms(x, g_ref[...]).astype(BF16)
        o_ref[...] = x

    xn = xn_ref[...]
    gate = _dot(xn, wg_ref[...].astype(BF16))
    up = _dot(xn, wu_ref[...].astype(BF16))
    act = (gate * _sigmoid(gate) * up).astype(BF16)
    o_ref[...] += HALF_STEP * _dot(act, wd_ref[...].astype(BF16))


def _ffn(h, gain, w_gate, w_up, w_down, layer, *, tm=1056, tf=512):
    m = h.shape[0]
    return pl.pallas_call(
        _ffn_body,
        grid=(m // tm, D_FF // tf),
        in_specs=[
            pl.BlockSpec((tm, D_MODEL), lambda i, f: (i, 0)),
            pl.BlockSpec((None, 1, D_MODEL), lambda i, f: (layer, 0, 0)),
            pl.BlockSpec((None, D_MODEL, tf), lambda i, f: (layer, 0, f)),
            pl.BlockSpec((None, D_MODEL, tf), lambda i, f: (layer, 0, f)),
            pl.BlockSpec((None, tf, D_MODEL), lambda i, f: (layer, f, 0)),
        ],
        out_specs=pl.BlockSpec((tm, D_MODEL), lambda i, f: (i, 0)),
        out_shape=jax.ShapeDtypeStruct((m, D_MODEL), F32),
        scratch_shapes=[pltpu.VMEM((tm, D_MODEL), BF16)],
        compiler_params=_params(("parallel", "arbitrary")),
        name="ffn",
    )(h, gain.reshape(DEPTH, 1, D_MODEL), w_gate, w_up, w_down)


def _inproj_a_body(x_ref, g_ref, w_ref, wkr_ref, za_ref, zk_ref, xn_ref):
    @pl.when(pl.program_id(1) == 0)
    def _():
        xn = _rms(x_ref[...], g_ref[...]).astype(BF16)
        xn_ref[...] = xn
        kr = _dot_nt(xn, wkr_ref[...].astype(BF16))
        zk_ref[...] = jnp.concatenate([kr, jnp.zeros_like(kr)], axis=1).astype(BF16)

    za_ref[...] = _dot_nt(xn_ref[...], w_ref[...].astype(BF16)).astype(BF16)


def _inproj_a(h, gain, w_in_t, layer, *, tm=1056, tn=768):
    m = h.shape[0]
    return pl.pallas_call(
        _inproj_a_body,
        grid=(m // tm, IN_KR // tn),
        in_specs=[
            pl.BlockSpec((tm, D_MODEL), lambda i, j: (i, 0)),
            pl.BlockSpec((None, 1, D_MODEL), lambda i, j: (layer, 0, 0)),
            pl.BlockSpec((None, tn, D_MODEL), lambda i, j: (layer, j, 0)),
            pl.BlockSpec((None, MLA_ROPE_DIM, D_MODEL), lambda i, j: (layer, IN_KR // MLA_ROPE_DIM, 0)),
        ],
        out_specs=[
            pl.BlockSpec((tm, tn), lambda i, j: (i, j)),
            pl.BlockSpec((tm, BLOCK), lambda i, j: (i, 0)),
            pl.BlockSpec((tm, D_MODEL), lambda i, j: (i, 0)),
        ],
        out_shape=[
            jax.ShapeDtypeStruct((m, IN_KR), BF16),
            jax.ShapeDtypeStruct((m, BLOCK), BF16),
            jax.ShapeDtypeStruct((m, D_MODEL), BF16),
        ],
        compiler_params=_params(("parallel", "arbitrary")),
        name="inproj_a",
    )(h, gain.reshape(DEPTH, 1, D_MODEL), w_in_t, w_in_t)


def _inproj_b_body(xn_ref, w_ref, o_ref):
    o_ref[...] = _dot_nt(xn_ref[...], w_ref[0].astype(BF16)).astype(BF16)


def _inproj_b(xn, w_in_t, layer, *, tm=2112, tn=768):
    m = xn.shape[0]
    return pl.pallas_call(
        _inproj_b_body,
        grid=(m // tm, B_WIDTH // tn),
        in_specs=[
            pl.BlockSpec((tm, D_MODEL), lambda i, j: (i, 0)),
            pl.BlockSpec((pl.Element(1), pl.Element(tn), pl.Element(D_MODEL)),
                         lambda i, j: (layer, pl.multiple_of(IN_B + j * tn, 8), 0)),
        ],
        out_specs=pl.BlockSpec((tm, tn), lambda i, j: (i, j)),
        out_shape=jax.ShapeDtypeStruct((m, B_WIDTH), BF16),
        compiler_params=_params(("parallel", "arbitrary")),
        name="inproj_b",
    )(xn, w_in_t)


def _swa_body(q_ref, kc_ref, kp_ref, km_ref, vc_ref, vp_ref, vm_ref, qn_ref, kn_ref, sink_ref, o_ref):
    n = pl.program_id(1)
    j = lax.broadcasted_iota(jnp.int32, (BLOCK, BLOCK), 0)
    i_loc = lax.broadcasted_iota(jnp.int32, (BLOCK, BLOCK), 1)
    ok_cur = (j <= i_loc) & (n * BLOCK + j >= PAD)
    ok_prev = (j > i_loc) & ((n - 1) * BLOCK + j >= PAD)
    ok_meta = (j >= PAD) & (n * BLOCK + i_loc - j >= WINDOW)

    def mask_rows(s_blk, ok):
        return jnp.concatenate(
            [jnp.where(ok, s_blk[:, h * BLOCK:(h + 1) * BLOCK], NEG_INF) for h in range(SWA_GROUP)], axis=1)

    q_t = [q_ref[:, a * BLOCK:(a + 1) * BLOCK].astype(F32).T for a in range(SWA_HEADS // 2)]
    v_t = jnp.concatenate([r[...].astype(F32).T for r in (vm_ref, vp_ref, vc_ref)], axis=1)
    k_gain = kn_ref[...] * qn_ref[...] * (SWA_HEAD_DIM ** -0.5)

    out_t = []
    for g in range(SWA_KV_HEADS):
        lo = g * SWA_HEAD_DIM
        hi = lo + SWA_HEAD_DIM
        heads = [g * SWA_GROUP + h for h in range(SWA_GROUP)]
        qt = jnp.concatenate(
            [q_t[hd // 2][(hd % 2) * SWA_HEAD_DIM:(hd % 2 + 1) * SWA_HEAD_DIM, :] for hd in heads], axis=1)
        q_rinv = lax.rsqrt(jnp.sum(qt * qt, axis=0, keepdims=True) * (1.0 / SWA_HEAD_DIM) + EPS)
        k_all = jnp.concatenate(
            [_rms(r[:, lo:hi].astype(F32), k_gain) for r in (km_ref, kp_ref, kc_ref)], axis=0).astype(BF16)
        s = _dot(k_all, qt.astype(BF16)) * q_rinv
        s = jnp.concatenate([mask_rows(s[:BLOCK], ok_meta), mask_rows(s[BLOCK:2 * BLOCK], ok_prev),
                             mask_rows(s[2 * BLOCK:], ok_cur)], axis=0)
        sink = sink_ref[g]
        mx = jnp.maximum(jnp.max(s, axis=0, keepdims=True), sink)
        p = jnp.exp(s - mx)
        den = jnp.sum(p, axis=0, keepdims=True) + jnp.exp(sink - mx)
        o_t = _dot(v_t[lo:hi].astype(BF16), p.astype(BF16)) / den
        out_t.extend(o_t[:, h * BLOCK:(h + 1) * BLOCK] for h in range(SWA_GROUP))
    for a in range(SWA_HEADS // 2):
        pair = jnp.concatenate([out_t[2 * a], out_t[2 * a + 1]], axis=0)
        o_ref[:, a * BLOCK:(a + 1) * BLOCK] = pair.T.astype(BF16)


def _swa(z, q_norm, k_norm, sink_col, nb, batch):
    m = z.shape[0]
    kcol = A_SWA_K // SWA_KV_W
    vcol = A_SWA_V // SWA_KV_W

    def cur(c):
        return pl.BlockSpec((BLOCK, SWA_KV_W), lambda b, n: (b * nb + n, c))

    def prev(c):
        return pl.BlockSpec((BLOCK, SWA_KV_W), lambda b, n: (b * nb + jnp.maximum(n - 1, 0), c))

    def meta(c):
        return pl.BlockSpec((BLOCK, SWA_KV_W), lambda b, n: (b * nb, c))

    return pl.pallas_call(
        _swa_body,
        grid=(batch, nb),
        in_specs=[
            pl.BlockSpec((BLOCK, SWA_OUT), lambda b, n: (b * nb + n, A_SWA_Q // SWA_OUT)),
            cur(kcol), prev(kcol), meta(kcol),
            cur(vcol), prev(vcol), meta(vcol),
            pl.BlockSpec((1, SWA_HEAD_DIM), lambda b, n: (0, 0)),
            pl.BlockSpec((1, SWA_HEAD_DIM), lambda b, n: (0, 0)),
            pl.BlockSpec((SWA_KV_HEADS, 1, SWA_GROUP * BLOCK), lambda b, n: (0, 0, 0)),
        ],
        out_specs=pl.BlockSpec((BLOCK, SWA_OUT), lambda b, n: (b * nb + n, 0)),
        out_shape=jax.ShapeDtypeStruct((m, SWA_OUT), BF16),
        compiler_params=_params(("parallel", "arbitrary")),
        name="swa",
    )(z, z, z, z, z, z, z, q_norm.reshape(1, -1), k_norm.reshape(1, -1), sink_col)


def _rope_rows(x, cos, sin, half):
    width = x.shape[-1]
    lane = lax.broadcasted_iota(jnp.int32, x.shape, 1)
    rot = jnp.where(lane < half, pltpu.roll(x, width - half, 1), pltpu.roll(x, half, 1))
    return x * cos + rot * sin


def _rope_key_norm(r, gain):
    return r * lax.rsqrt(jnp.sum(r * r, axis=-1, keepdims=True) * (1.0 / MLA_ROPE_DIM) + EPS) * gain


def _mla_up_body(cq0_ref, cq1_ref, ckv0_ref, ckv1_ref, kr_ref, qa_ref, kva_ref, wq_ref, wkv_ref, qn_ref, qr_ref,
                 kn_ref, krn_ref, cos_ref, sin_ref, q_ref, k_ref, v_ref):
    cos = cos_ref[...]
    sin = sin_ref[...]
    half = MLA_ROPE_DIM // 2
    scale = (MLA_NOPE_DIM + MLA_ROPE_DIM) ** -0.5 * LOG2_E
    c_q = jnp.concatenate([cq0_ref[...], cq1_ref[...]], axis=1).astype(F32)
    c_kv = jnp.concatenate([ckv0_ref[...], ckv1_ref[...]], axis=1).astype(F32)
    yq = _dot(_rms(c_q, qa_ref[...]).astype(BF16), wq_ref[...])
    ykv = _dot(_rms(c_kv, kva_ref[...]).astype(BF16), wkv_ref[...].astype(BF16))
    k_rope = _rope_rows(_rope_key_norm(kr_ref[...].astype(F32), krn_ref[...]), cos, sin, half).astype(BF16)
    for h in range(MLA_HEADS):
        lo = h * MLA_QK_PAD
        nope = _rms(yq[:, lo:lo + MLA_NOPE_DIM], qn_ref[...])
        r = _rope_rows(_rope_key_norm(yq[:, lo + MLA_NOPE_DIM:lo + MLA_QK_PAD], qr_ref[...]), cos, sin, half)
        q_ref[:, lo:lo + MLA_NOPE_DIM] = (nope * scale).astype(BF16)
        q_ref[:, lo + MLA_NOPE_DIM:lo + MLA_QK_PAD] = (r * scale).astype(BF16)
        k_ref[:, lo:lo + MLA_NOPE_DIM] = _rms(ykv[:, lo:lo + MLA_NOPE_DIM], kn_ref[...]).astype(BF16)
        k_ref[:, lo + MLA_NOPE_DIM:lo + MLA_QK_PAD] = k_rope
        v_ref[:, h * MLA_V_DIM:(h + 1) * MLA_V_DIM] = ykv[:, lo + MLA_NOPE_DIM:lo + MLA_QK_PAD].astype(BF16)


def _mla_up(z_a, z_kr, qa_norm, kva_norm, w_uq_pad, w_ukv, qn_norm, qr_norm_pad, kn_norm, kr_norm_pad, cos_t, sin_t,
            layer, seq, *, tm=528):
    m = z_a.shape[0]
    seq_tiles = seq // tm
    kv_w = MLA_NOPE_DIM + MLA_V_DIM
    half_rank = MLA_Q_RANK // 2

    def row(width, col):
        return pl.BlockSpec((tm, width), lambda i: (i, col))

    def vec(width):
        return pl.BlockSpec((1, width), lambda i: (0, 0))

    return pl.pallas_call(
        _mla_up_body,
        grid=(m // tm,),
        in_specs=[
            row(half_rank, A_MLA_CQ // half_rank), row(half_rank, A_MLA_CQ // half_rank + 1),
            row(half_rank, A_MLA_CKV // half_rank), row(half_rank, A_MLA_CKV // half_rank + 1),
            row(BLOCK, 0),
            vec(MLA_Q_RANK), vec(MLA_KV_RANK),
            pl.BlockSpec((None, MLA_Q_RANK, MLA_HEADS * MLA_QK_PAD), lambda i: (layer, 0, 0)),
            pl.BlockSpec((None, MLA_KV_RANK, MLA_HEADS * kv_w), lambda i: (layer, 0, 0)),
            vec(MLA_NOPE_DIM), vec(BLOCK), vec(MLA_NOPE_DIM), vec(BLOCK),
            pl.BlockSpec((tm, BLOCK), lambda i: (i % seq_tiles, 0)),
            pl.BlockSpec((tm, BLOCK), lambda i: (i % seq_tiles, 0)),
        ],
        out_specs=[row(MLA_HEADS * MLA_QK_PAD, 0), row(MLA_HEADS * MLA_QK_PAD, 0), row(MLA_OUT, 0)],
        out_shape=[
            jax.ShapeDtypeStruct((m, MLA_HEADS * MLA_QK_PAD), BF16),
            jax.ShapeDtypeStruct((m, MLA_HEADS * MLA_QK_PAD), BF16),
            jax.ShapeDtypeStruct((m, MLA_OUT), BF16),
        ],
        compiler_params=_params(("parallel",)),
        name="mla_up",
    )(z_a, z_a, z_a, z_a, z_kr, qa_norm.reshape(1, -1), kva_norm.reshape(1, -1), w_uq_pad, w_ukv,
      qn_norm.reshape(1, -1), qr_norm_pad, kn_norm.reshape(1, -1), kr_norm_pad, cos_t, sin_t)


def _mla_attn_body(q_ref, k_ref, v_ref, o_ref, m_ref, acc_ref, *, t, hb):
    i = pl.program_id(2)
    m_ref[...] = jnp.full(m_ref.shape, NEG_INF, F32)
    acc_ref[...] = jnp.zeros(acc_ref.shape, F32)

    def tile(blk, nblk, masked):
        width = nblk * t
        off = blk * t if isinstance(blk, int) else pl.multiple_of(blk * t, t)
        if masked:
            row = i * t + lax.broadcasted_iota(jnp.int32, (t, width), 0)
            col = blk * t + lax.broadcasted_iota(jnp.int32, (t, width), 1)
            ok = (col <= row) & (col >= PAD)
        for h in range(hb):
            q = q_ref[:, h * MLA_QK_PAD:(h + 1) * MLA_QK_PAD]
            kb = k_ref[pl.ds(off, width), h * MLA_QK_PAD:(h + 1) * MLA_QK_PAD]
            vb = v_ref[pl.ds(off, width), h * MLA_V_DIM:(h + 1) * MLA_V_DIM]
            s = _dot_nt(q, kb)
            if masked:
                s = jnp.where(ok, s, NEG_INF)
            m_run = m_ref[h]
            m_new = jnp.maximum(m_run, jnp.max(s, axis=-1, keepdims=True))
            alpha = jnp.exp2(m_run - m_new)
            p = jnp.concatenate(
                [jnp.exp2(s[:, c * BLOCK:(c + 1) * BLOCK] - m_new) for c in range(width // BLOCK)], axis=1)
            v_aug = jnp.concatenate([vb, jnp.ones_like(vb)], axis=1)
            acc_ref[h] = jnp.concatenate([alpha, alpha], axis=1) * acc_ref[h] + _dot(p.astype(BF16), v_aug)
            m_ref[h] = m_new

    @pl.when(i >= 2)
    def _():
        tile(0, 2, True)

    def body(jj, carry):
        tile(2 * jj, 2, False)
        return carry

    lax.fori_loop(1, i // 2, body, 0)

    @pl.when(i % 2 == 1)
    def _():
        tile(i - 1, 1, True)

    tile(i, 1, True)

    for h in range(hb):
        acc = acc_ref[h]
        o_ref[:, h * MLA_V_DIM:(h + 1) * MLA_V_DIM] = (acc[:, :MLA_V_DIM] / acc[:, MLA_V_DIM:]).astype(BF16)


def _mla_attn(q, k, v, seq, batch, *, t=384, hb=4):
    m = q.shape[0]
    nq = seq // t
    return pl.pallas_call(
        functools.partial(_mla_attn_body, t=t, hb=hb),
        grid=(batch, MLA_HEADS // hb, nq),
        in_specs=[
            pl.BlockSpec((t, hb * MLA_QK_PAD), lambda b, h, i: (b * nq + i, h)),
            pl.BlockSpec((seq, hb * MLA_QK_PAD), lambda b, h, i: (b, h)),
            pl.BlockSpec((seq, hb * MLA_V_DIM), lambda b, h, i: (b, h)),
        ],
        out_specs=pl.BlockSpec((t, hb * MLA_V_DIM), lambda b, h, i: (b * nq + i, h)),
        out_shape=jax.ShapeDtypeStruct((m, MLA_OUT), BF16),
        scratch_shapes=[pltpu.VMEM((hb, t, BLOCK), F32), pltpu.VMEM((hb, t, 2 * MLA_V_DIM), F32)],
        compiler_params=_params(("parallel", "parallel", "arbitrary")),
        name="mla_attn",
    )(q, k, v)


def _ret_body(q_ref, k_ref, v_ref, g_ref, cos_ref, sin_ref, dec_ref, zeta_ref, xi_ref, gn_ref, o_ref,
              state_ref, *, chunk_decay):
    n = pl.program_id(1)

    @pl.when(n == 0)
    def _():
        state_ref[...] = jnp.zeros_like(state_ref)

    cos = cos_ref[...]
    sin = sin_ref[...]
    valid = (n * BLOCK + lax.broadcasted_iota(jnp.int32, (BLOCK, 1), 0) >= PAD).astype(F32)
    for h in range(RET_HEADS):
        ks = slice(h * RET_KEY_DIM, (h + 1) * RET_KEY_DIM)
        vs = slice(h * RET_VAL_DIM, (h + 1) * RET_VAL_DIM)
        qh = _rope_rows(q_ref[:, ks].astype(F32), cos, sin, RET_KEY_DIM // 2)
        kh = _rope_rows(k_ref[:, ks].astype(F32), cos, sin, RET_KEY_DIM // 2) * (RET_KEY_DIM ** -0.5) * valid
        vb = v_ref[:, vs]
        s = _dot_nt(qh.astype(BF16), kh.astype(BF16)) * dec_ref[h]
        inner = _dot(s.astype(BF16), vb)
        prev = state_ref[h]
        cross = _dot((qh * xi_ref[h]).astype(BF16), prev.astype(BF16))
        kz_t = (kh * zeta_ref[h]).T.astype(BF16)
        state_ref[h] = prev * chunk_decay[h] + _dot(kz_t, vb)
        o = inner + cross
        mu = jnp.mean(o, axis=-1, keepdims=True)
        d = o - mu
        y = d * lax.rsqrt(jnp.mean(d * d, axis=-1, keepdims=True) + EPS)
        gate = g_ref[:, vs].astype(F32)
        o_ref[:, vs] = (y * gn_ref[:, vs] * (gate * _sigmoid(gate))).astype(BF16)


def _retention(z, cos_t, sin_t, decay, zeta_b, xi_b, gn, chunk_decay, nb, batch):
    m = z.shape[0]
    tab = pl.BlockSpec((RET_HEADS, BLOCK, BLOCK), lambda b, n: (0, 0, 0))
    return pl.pallas_call(
        functools.partial(_ret_body, chunk_decay=chunk_decay),
        grid=(batch, nb),
        in_specs=[
            pl.BlockSpec((BLOCK, RET_QK_W), lambda b, n: (b * nb + n, B_RET_Q // RET_QK_W)),
            pl.BlockSpec((BLOCK, RET_QK_W), lambda b, n: (b * nb + n, B_RET_K // RET_QK_W)),
            pl.BlockSpec((BLOCK, RET_OUT), lambda b, n: (b * nb + n, B_RET_V // RET_OUT)),
            pl.BlockSpec((BLOCK, RET_OUT), lambda b, n: (b * nb + n, B_RET_G // RET_OUT)),
            pl.BlockSpec((BLOCK, BLOCK), lambda b, n: (n, 0)),
            pl.BlockSpec((BLOCK, BLOCK), lambda b, n: (n, 0)),
            tab, tab, tab,
            pl.BlockSpec((1, RET_OUT), lambda b, n: (0, 0)),
        ],
        out_specs=pl.BlockSpec((BLOCK, RET_OUT), lambda b, n: (b * nb + n, 0)),
        out_shape=jax.ShapeDtypeStruct((m, RET_OUT), BF16),
        scratch_shapes=[pltpu.VMEM((RET_HEADS, RET_KEY_DIM, RET_VAL_DIM), F32)],
        compiler_params=_params(("parallel", "arbitrary")),
        name="retention",
    )(z, z, z, z, cos_t, sin_t, decay, zeta_b, xi_b, gn.reshape(1, -1))


def _merge_body(oa_ref, ob_ref, oc_ref, wa_ref, wb_ref, wc_ref, ga_ref, gb_ref, gc_ref, o_ref):
    def branch(o, w, g):
        return _sigmoid(g[...].astype(F32)) * _dot(o[...], w[...].astype(BF16))

    o_ref[...] = (branch(oa_ref, wa_ref, ga_ref) + branch(ob_ref, wb_ref, gb_ref)
                  + branch(oc_ref, wc_ref, gc_ref)).astype(BF16)


def _merge(z, o_a, o_b, o_c, w_a, w_b, w_c, layer, *, tm=1056, tn=512):
    m = z.shape[0]
    nt = D_MODEL // tn

    def act(width):
        return pl.BlockSpec((tm, width), lambda i, j: (i, 0))

    def wgt(width):
        return pl.BlockSpec((None, width, tn), lambda i, j: (layer, 0, j))

    def gate(br):
        return pl.BlockSpec((tm, tn), lambda i, j: (i, B_GATE // tn + br * nt + j))

    return pl.pallas_call(
        _merge_body,
        grid=(m // tm, nt),
        in_specs=[act(SWA_OUT), act(MLA_OUT), act(RET_OUT), wgt(SWA_OUT), wgt(MLA_OUT), wgt(RET_OUT),
                  gate(0), gate(1), gate(2)],
        out_specs=pl.BlockSpec((tm, tn), lambda i, j: (i, j)),
        out_shape=jax.ShapeDtypeStruct((m, D_MODEL), BF16),
        compiler_params=_params(("parallel", "arbitrary")),
        name="merge",
    )(o_a, o_b, o_c, w_a, w_b, w_c, z, z, z)


def _outproj_body(x_ref, w_ref, r_ref, o_ref):
    o_ref[...] = r_ref[...] + _dot(x_ref[...], w_ref[...].astype(BF16))


def _outproj(x, w, res, layer, *, tm=1056, tn=512):
    m = x.shape[0]
    return pl.pallas_call(
        _outproj_body,
        grid=(m // tm, D_MODEL // tn),
        in_specs=[
            pl.BlockSpec((tm, D_MODEL), lambda i, j: (i, 0)),
            pl.BlockSpec((None, D_MODEL, tn), lambda i, j: (layer, 0, j)),
            pl.BlockSpec((tm, tn), lambda i, j: (i, j)),
        ],
        out_specs=pl.BlockSpec((tm, tn), lambda i, j: (i, j)),
        out_shape=jax.ShapeDtypeStruct((m, D_MODEL), F32),
        compiler_params=_params(("parallel", "arbitrary")),
        name="outproj",
    )(x, w, res)


def _rope_tables(pos, dim, width):
    half = dim // 2
    inv_freq = ROPE_BASE ** (-jnp.arange(half, dtype=F32) / half)
    ang = pos[:, None] * inv_freq[None, :]
    cos = jnp.cos(ang)
    sin = jnp.sin(ang)
    fill = jnp.zeros((pos.shape[0], width - dim), F32)
    return (jnp.concatenate([cos, cos, fill], axis=1), jnp.concatenate([-sin, sin, fill], axis=1))


def kernel(x, meta_tokens, ffn1_norm, ffn1_w_gate, ffn1_w_up, ffn1_w_down, mix_norm, w_in, swa_q_norm, swa_k_norm, swa_sinks, mla_q_a_norm, mla_w_uq, mla_kv_a_norm, mla_w_ukv, mla_qn_norm, mla_qr_norm, mla_kn_norm, mla_kr_norm, ret_gn, w_br_swa, w_br_mla, w_br_ret, w_o, ffn2_norm, ffn2_w_gate, ffn2_w_up, ffn2_w_down):
    batch, seq_in, _ = x.shape
    seq = seq_in + BLOCK
    nb = seq // BLOCK

    meta = jnp.broadcast_to(meta_tokens[None].astype(x.dtype), (batch, N_META, D_MODEL))
    h = jnp.concatenate([jnp.zeros((batch, PAD, D_MODEL), x.dtype), meta, x], axis=1)
    h = h.reshape(batch * seq, D_MODEL)
    pos = (jnp.arange(seq) - PAD).astype(F32)

    w_in_t = jnp.swapaxes(w_in, 1, 2)
    ffn1_w_gate, ffn1_w_up, ffn1_w_down, ffn2_w_gate, ffn2_w_up, ffn2_w_down = (
        w.astype(BF16) for w in (ffn1_w_gate, ffn1_w_up, ffn1_w_down, ffn2_w_gate, ffn2_w_up, ffn2_w_down))

    qk_dim = MLA_NOPE_DIM + MLA_ROPE_DIM
    w_uq_pad = jnp.pad(mla_w_uq.reshape(DEPTH, MLA_Q_RANK, MLA_HEADS, qk_dim),
                       ((0, 0), (0, 0), (0, 0), (0, MLA_QK_PAD - qk_dim)))
    w_uq_pad = w_uq_pad.reshape(DEPTH, MLA_Q_RANK, MLA_HEADS * MLA_QK_PAD).astype(BF16)
    rope_fill = jnp.zeros((DEPTH, BLOCK - MLA_ROPE_DIM), F32)
    qr_norm_pad = jnp.concatenate([mla_qr_norm, rope_fill], axis=1)
    kr_norm_pad = jnp.concatenate([mla_kr_norm, rope_fill], axis=1)

    cos_m, sin_m = _rope_tables(pos, MLA_ROPE_DIM, BLOCK)
    cos_r, sin_r = _rope_tables(pos, RET_KEY_DIM, BLOCK)

    log_gamma = jnp.log(1.0 - 2.0 ** (-5.0 - jnp.arange(RET_HEADS, dtype=F32)))
    idx = jnp.arange(BLOCK, dtype=F32)
    diff = idx[:, None] - idx[None, :]
    decay = jnp.where(diff[None] >= 0, jnp.exp(jnp.maximum(diff, 0.0)[None] * log_gamma[:, None, None]), 0.0)
    zeta = jnp.exp((BLOCK - 1.0 - idx)[None, :] * log_gamma[:, None])
    xi = jnp.exp((idx + 1.0)[None, :] * log_gamma[:, None])
    zeta_b = jnp.broadcast_to(zeta[:, :, None], (RET_HEADS, BLOCK, RET_KEY_DIM))
    xi_b = jnp.broadcast_to(xi[:, :, None], (RET_HEADS, BLOCK, RET_KEY_DIM))
    chunk_decay = tuple(float(np.exp(BLOCK * np.log(1.0 - 2.0 ** (-5.0 - hh)))) for hh in range(RET_HEADS))

    for l in range(DEPTH):
        h = _ffn(h, ffn1_norm, ffn1_w_gate, ffn1_w_up, ffn1_w_down, l)
        z_a, z_kr, hn = _inproj_a(h, mix_norm, w_in_t, l)
        z_b = _inproj_b(hn, w_in_t, l)
        sink_col = jnp.repeat(swa_sinks[l].astype(F32).reshape(SWA_KV_HEADS, SWA_GROUP), BLOCK, axis=1)
        o_a = _swa(z_a, swa_q_norm[l], swa_k_norm[l], sink_col[:, None, :], nb, batch)
        q, k, v = _mla_up(z_a, z_kr, mla_q_a_norm[l], mla_kv_a_norm[l], w_uq_pad, mla_w_ukv, mla_qn_norm[l],
                          qr_norm_pad[l:l + 1], mla_kn_norm[l], kr_norm_pad[l:l + 1], cos_m, sin_m, l, seq)
        o_b = _mla_attn(q, k, v, seq, batch)
        o_c = _retention(z_b, cos_r, sin_r, decay, zeta_b, xi_b, ret_gn[l], chunk_decay, nb, batch)
        merged = _merge(z_b, o_a, o_b, o_c, w_br_swa, w_br_mla, w_br_ret, l)
        h = _outproj(merged, w_o, h, l)
        h = _ffn(h, ffn2_norm, ffn2_w_gate, ffn2_w_up, ffn2_w_down, l)
    return h.reshape(batch, seq, D_MODEL)[:, BLOCK:]
```

```python
import functools

import jax
import jax.numpy as jnp
import numpy as np
from jax import lax
from jax.experimental import pallas as pl
from jax.experimental.pallas import tpu as pltpu

F32 = jnp.float32
BF16 = jnp.bfloat16

D_MODEL = 2048
DEPTH = 4
N_META = 16
BLOCK = 128
PAD = BLOCK - N_META
EPS = 1e-6
NEG_INF = -1e30
ROPE_BASE = 10000.0
HALF_STEP = 0.5
LOG2_E = 1.4426950408889634

SWA_HEADS = 16
SWA_KV_HEADS = 2
SWA_HEAD_DIM = 64
SWA_GROUP = SWA_HEADS // SWA_KV_HEADS
WINDOW = 128

MLA_HEADS = 8
MLA_Q_RANK = 512
MLA_KV_RANK = 512
MLA_NOPE_DIM = 128
MLA_ROPE_DIM = 64
MLA_V_DIM = 128
MLA_QK_PAD = 256

RET_HEADS = 4
RET_KEY_DIM = 128
RET_VAL_DIM = 256

D_FF = 5632
N_BRANCH = 3

SWA_OUT = SWA_HEADS * SWA_HEAD_DIM
SWA_KV_W = SWA_KV_HEADS * SWA_HEAD_DIM
MLA_OUT = MLA_HEADS * MLA_V_DIM
RET_QK_W = RET_HEADS * RET_KEY_DIM
RET_OUT = RET_HEADS * RET_VAL_DIM

A_SWA_Q = 0
A_SWA_K = A_SWA_Q + SWA_OUT
A_SWA_V = A_SWA_K + SWA_KV_W
A_MLA_CQ = A_SWA_V + SWA_KV_W
A_MLA_CKV = A_MLA_CQ + MLA_Q_RANK
IN_KR = A_MLA_CKV + MLA_KV_RANK
IN_B = IN_KR + MLA_ROPE_DIM
B_RET_Q = 0
B_RET_K = B_RET_Q + RET_QK_W
B_RET_V = B_RET_K + RET_QK_W
B_RET_G = B_RET_V + RET_OUT
B_GATE = B_RET_G + RET_OUT
B_WIDTH = B_GATE + N_BRANCH * D_MODEL
IN_WIDTH = IN_B + B_WIDTH

VMEM_LIMIT = 56 * 1024 * 1024


def _params(semantics):
    return pltpu.CompilerParams(dimension_semantics=semantics, vmem_limit_bytes=VMEM_LIMIT)


def _rms(x, g):
    return x * lax.rsqrt(jnp.mean(x * x, axis=-1, keepdims=True) + EPS) * g


def _sigmoid(x):
    return 1.0 / (1.0 + jnp.exp(-x))


def _dot(a, b):
    return jnp.dot(a, b, preferred_element_type=F32)


def _dot_nt(a, b):
    return lax.dot_general(a, b, (((1,), (1,)), ((), ())), preferred_element_type=F32)


FFN_TM = 1056
FFN_TF_HEAD = 256
FFN_TF_TAIL = 512


def _ffn_start(x_ref, g_ref, o_ref, xn_ref):
    x = x_ref[...]
    xn_ref[...] = _rms(x, g_ref[...]).astype(BF16)
    o_ref[...] = x


def _ffn_accumulate(xn_ref, wg, wu, wd, o_ref):
    xn = xn_ref[...]
    gate = _dot(xn, wg)
    up = _dot(xn, wu)
    act = (gate * _sigmoid(gate) * up).astype(BF16)
    o_ref[...] += HALF_STEP * _dot(act, wd)


def _ffn_head_body(x_ref, g_ref, wg_ref, wu_ref, wd_ref, o_ref, wgb_ref, wub_ref, wdb_ref, xn_ref):
    @pl.when(pl.program_id(0) == 0)
    def _():
        _ffn_start(x_ref, g_ref, o_ref, xn_ref)

    wg = wg_ref[...].astype(BF16)
    wu = wu_ref[...].astype(BF16)
    wd = wd_ref[...].astype(BF16)
    wgb_ref[...] = wg
    wub_ref[...] = wu
    wdb_ref[...] = wd
    _ffn_accumulate(xn_ref, wg, wu, wd, o_ref)


def _ffn_tail_body(x_ref, g_ref, wg_ref, wu_ref, wd_ref, head_ref, o_ref, xn_ref):
    del head_ref
    @pl.when(pl.program_id(1) == 0)
    def _():
        _ffn_start(x_ref, g_ref, o_ref, xn_ref)

    _ffn_accumulate(xn_ref, wg_ref[...], wu_ref[...], wd_ref[...], o_ref)


def _ffn(h, gain, w_gate, w_up, w_down, layer):
    m = h.shape[0]
    tm, tfh, tft = FFN_TM, FFN_TF_HEAD, FFN_TF_TAIL
    gain = gain.reshape(DEPTH, 1, D_MODEL)
    head, wg_b, wu_b, wd_b = pl.pallas_call(
        _ffn_head_body,
        grid=(D_FF // tfh,),
        in_specs=[
            pl.BlockSpec((tm, D_MODEL), lambda f: (0, 0), pipeline_mode=pl.Buffered(1)),
            pl.BlockSpec((None, 1, D_MODEL), lambda f: (layer, 0, 0)),
            pl.BlockSpec((None, D_MODEL, tfh), lambda f: (layer, 0, f)),
            pl.BlockSpec((None, D_MODEL, tfh), lambda f: (layer, 0, f)),
            pl.BlockSpec((None, tfh, D_MODEL), lambda f: (layer, f, 0)),
        ],
        out_specs=[
            pl.BlockSpec((tm, D_MODEL), lambda f: (0, 0)),
            pl.BlockSpec((D_MODEL, tfh), lambda f: (0, f)),
            pl.BlockSpec((D_MODEL, tfh), lambda f: (0, f)),
            pl.BlockSpec((tfh, D_MODEL), lambda f: (f, 0)),
        ],
        out_shape=[
            jax.ShapeDtypeStruct((m, D_MODEL), F32),
            jax.ShapeDtypeStruct((D_MODEL, D_FF), BF16),
            jax.ShapeDtypeStruct((D_MODEL, D_FF), BF16),
            jax.ShapeDtypeStruct((D_FF, D_MODEL), BF16),
        ],
        scratch_shapes=[pltpu.VMEM((tm, D_MODEL), BF16)],
        compiler_params=_params(("arbitrary",)),
        name="ffn_head",
    )(h, gain, w_gate, w_up, w_down)
    return pl.pallas_call(
        _ffn_tail_body,
        grid=(m // tm - 1, D_FF // tft),
        in_specs=[
            pl.BlockSpec((tm, D_MODEL), lambda i, f: (i + 1, 0)),
            pl.BlockSpec((None, 1, D_MODEL), lambda i, f: (layer, 0, 0)),
            pl.BlockSpec((D_MODEL, tft), lambda i, f: (0, f)),
            pl.BlockSpec((D_MODEL, tft), lambda i, f: (0, f)),
            pl.BlockSpec((tft, D_MODEL), lambda i, f: (f, 0)),
            pl.BlockSpec(memory_space=pl.ANY),
        ],
        out_specs=pl.BlockSpec((tm, D_MODEL), lambda i, f: (i + 1, 0)),
        out_shape=jax.ShapeDtypeStruct((m, D_MODEL), F32),
        scratch_shapes=[pltpu.VMEM((tm, D_MODEL), BF16)],
        input_output_aliases={5: 0},
        compiler_params=_params(("parallel", "arbitrary")),
        name="ffn_tail",
    )(h, gain, wg_b, wu_b, wd_b, head)


def _inproj_a_body(x_ref, g_ref, w_ref, wkr_ref, za_ref, zk_ref, xn_ref):
    @pl.when(pl.program_id(1) == 0)
    def _():
        xn = _rms(x_ref[...], g_ref[...]).astype(BF16)
        xn_ref[...] = xn
        kr = _dot_nt(xn, wkr_ref[...].astype(BF16))
        zk_ref[...] = jnp.concatenate([kr, jnp.zeros_like(kr)], axis=1).astype(BF16)

    za_ref[...] = _dot_nt(xn_ref[...], w_ref[...].astype(BF16)).astype(BF16)


def _inproj_a(h, gain, w_in_t, layer, *, tm=1056, tn=768):
    m = h.shape[0]
    return pl.pallas_call(
        _inproj_a_body,
        grid=(m // tm, IN_KR // tn),
        in_specs=[
            pl.BlockSpec((tm, D_MODEL), lambda i, j: (i, 0)),
            pl.BlockSpec((None, 1, D_MODEL), lambda i, j: (layer, 0, 0)),
            pl.BlockSpec((None, tn, D_MODEL), lambda i, j: (layer, j, 0)),
            pl.BlockSpec((None, MLA_ROPE_DIM, D_MODEL), lambda i, j: (layer, IN_KR // MLA_ROPE_DIM, 0)),
        ],
        out_specs=[
            pl.BlockSpec((tm, tn), lambda i, j: (i, j)),
            pl.BlockSpec((tm, BLOCK), lambda i, j: (i, 0)),
            pl.BlockSpec((tm, D_MODEL), lambda i, j: (i, 0)),
        ],
        out_shape=[
            jax.ShapeDtypeStruct((m, IN_KR), BF16),
            jax.ShapeDtypeStruct((m, BLOCK), BF16),
            jax.ShapeDtypeStruct((m, D_MODEL), BF16),
        ],
        compiler_params=_params(("parallel", "arbitrary")),
        name="inproj_a",
    )(h, gain.reshape(DEPTH, 1, D_MODEL), w_in_t, w_in_t)


def _inproj_b_body(xn_ref, w_ref, o_ref):
    o_ref[...] = _dot_nt(xn_ref[...], w_ref[0].astype(BF16)).astype(BF16)


def _inproj_b(xn, w_in_t, layer, *, tm=2112, tn=768):
    m = xn.shape[0]
    return pl.pallas_call(
        _inproj_b_body,
        grid=(m // tm, B_WIDTH // tn),
        in_specs=[
            pl.BlockSpec((tm, D_MODEL), lambda i, j: (i, 0)),
            pl.BlockSpec((pl.Element(1), pl.Element(tn), pl.Element(D_MODEL)),
                         lambda i, j: (layer, pl.multiple_of(IN_B + j * tn, 8), 0)),
        ],
        out_specs=pl.BlockSpec((tm, tn), lambda i, j: (i, j)),
        out_shape=jax.ShapeDtypeStruct((m, B_WIDTH), BF16),
        compiler_params=_params(("parallel", "arbitrary")),
        name="inproj_b",
    )(xn, w_in_t)


def _swa_body(q_ref, kc_ref, kp_ref, km_ref, vc_ref, vp_ref, vm_ref, qn_ref, kn_ref, sink_ref, o_ref):
    n = pl.program_id(0)
    j = lax.broadcasted_iota(jnp.int32, (BLOCK, BLOCK), 0)
    i_loc = lax.broadcasted_iota(jnp.int32, (BLOCK, BLOCK), 1)
    ok_cur = (j <= i_loc) & (n * BLOCK + j >= PAD)
    ok_prev = (j > i_loc) & ((n - 1) * BLOCK + j >= PAD)
    ok_meta = (j >= PAD) & (n * BLOCK + i_loc - j >= WINDOW)

    def mask_rows(s_blk, ok):
        return jnp.concatenate(
            [jnp.where(ok, s_blk[:, h * BLOCK:(h + 1) * BLOCK], NEG_INF) for h in range(SWA_GROUP)], axis=1)

    k_gain = kn_ref[...] * qn_ref[...] * (SWA_HEAD_DIM ** -0.5)
    for b in range(q_ref.shape[0]):
        _swa_block(b, q_ref, kc_ref, kp_ref, km_ref, vc_ref, vp_ref, vm_ref, sink_ref, o_ref, k_gain,
                   (ok_meta, ok_prev, ok_cur), mask_rows)


def _swa_block(b, q_ref, kc_ref, kp_ref, km_ref, vc_ref, vp_ref, vm_ref, sink_ref, o_ref, k_gain, oks, mask_rows):
    ok_meta, ok_prev, ok_cur = oks
    q_t = [q_ref[b, :, a * BLOCK:(a + 1) * BLOCK].astype(F32).T for a in range(SWA_HEADS // 2)]
    v_t = jnp.concatenate([r[b].astype(F32).T for r in (vm_ref, vp_ref, vc_ref)], axis=1)

    out_t = []
    for g in range(SWA_KV_HEADS):
        lo = g * SWA_HEAD_DIM
        hi = lo + SWA_HEAD_DIM
        heads = [g * SWA_GROUP + h for h in range(SWA_GROUP)]
        qt = jnp.concatenate(
            [q_t[hd // 2][(hd % 2) * SWA_HEAD_DIM:(hd % 2 + 1) * SWA_HEAD_DIM, :] for hd in heads], axis=1)
        q_rinv = lax.rsqrt(jnp.sum(qt * qt, axis=0, keepdims=True) * (1.0 / SWA_HEAD_DIM) + EPS)
        k_all = jnp.concatenate(
            [_rms(r[b, :, lo:hi].astype(F32), k_gain) for r in (km_ref, kp_ref, kc_ref)], axis=0).astype(BF16)
        s = _dot(k_all, qt.astype(BF16)) * q_rinv
        s = jnp.concatenate([mask_rows(s[:BLOCK], ok_meta), mask_rows(s[BLOCK:2 * BLOCK], ok_prev),
                             mask_rows(s[2 * BLOCK:], ok_cur)], axis=0)
        sink = sink_ref[g]
        mx = jnp.maximum(jnp.max(s, axis=0, keepdims=True), sink)
        p = jnp.exp(s - mx)
        den = jnp.sum(p, axis=0, keepdims=True) + jnp.exp(sink - mx)
        o_t = _dot(v_t[lo:hi].astype(BF16), p.astype(BF16)) / den
        out_t.extend(o_t[:, h * BLOCK:(h + 1) * BLOCK] for h in range(SWA_GROUP))
    for a in range(SWA_HEADS // 2):
        pair = jnp.concatenate([out_t[2 * a], out_t[2 * a + 1]], axis=0)
        o_ref[b, :, a * BLOCK:(a + 1) * BLOCK] = pair.T.astype(BF16)


def _swa(z, q_norm, k_norm, sink_col, nb, batch):
    m = z.shape[0]
    z = z.reshape(batch, nb * BLOCK, z.shape[1])
    kcol = A_SWA_K // SWA_KV_W
    vcol = A_SWA_V // SWA_KV_W

    def cur(c):
        return pl.BlockSpec((batch, BLOCK, SWA_KV_W), lambda n: (0, n, c))

    def prev(c):
        return pl.BlockSpec((batch, BLOCK, SWA_KV_W), lambda n: (0, jnp.maximum(n - 1, 0), c))

    def meta(c):
        return pl.BlockSpec((batch, BLOCK, SWA_KV_W), lambda n: (0, 0, c))

    out = pl.pallas_call(
        _swa_body,
        grid=(nb,),
        in_specs=[
            pl.BlockSpec((batch, BLOCK, SWA_OUT), lambda n: (0, n, A_SWA_Q // SWA_OUT)),
            cur(kcol), prev(kcol), meta(kcol),
            cur(vcol), prev(vcol), meta(vcol),
            pl.BlockSpec((1, SWA_HEAD_DIM), lambda n: (0, 0)),
            pl.BlockSpec((1, SWA_HEAD_DIM), lambda n: (0, 0)),
            pl.BlockSpec((SWA_KV_HEADS, 1, SWA_GROUP * BLOCK), lambda n: (0, 0, 0)),
        ],
        out_specs=pl.BlockSpec((batch, BLOCK, SWA_OUT), lambda n: (0, n, 0)),
        out_shape=jax.ShapeDtypeStruct((batch, nb * BLOCK, SWA_OUT), BF16),
        compiler_params=_params(("arbitrary",)),
        name="swa",
    )(z, z, z, z, z, z, z, q_norm.reshape(1, -1), k_norm.reshape(1, -1), sink_col)
    return out.reshape(m, SWA_OUT)


def _rope_rows(x, cos, sin, half):
    width = x.shape[-1]
    lane = lax.broadcasted_iota(jnp.int32, x.shape, 1)
    rot = jnp.where(lane < half, pltpu.roll(x, width - half, 1), pltpu.roll(x, half, 1))
    return x * cos + rot * sin


def _rope_key_norm(r, gain):
    return r * lax.rsqrt(jnp.sum(r * r, axis=-1, keepdims=True) * (1.0 / MLA_ROPE_DIM) + EPS) * gain


def _mla_up_body(cq0_ref, cq1_ref, ckv0_ref, ckv1_ref, kr_ref, qa_ref, kva_ref, wq_ref, wkv_ref, qn_ref, qr_ref,
                 kn_ref, krn_ref, cos_ref, sin_ref, q_ref, k_ref, v_ref):
    cos = cos_ref[...]
    sin = sin_ref[...]
    half = MLA_ROPE_DIM // 2
    scale = (MLA_NOPE_DIM + MLA_ROPE_DIM) ** -0.5 * LOG2_E
    c_q = jnp.concatenate([cq0_ref[...], cq1_ref[...]], axis=1).astype(F32)
    c_kv = jnp.concatenate([ckv0_ref[...], ckv1_ref[...]], axis=1).astype(F32)
    yq = _dot(_rms(c_q, qa_ref[...]).astype(BF16), wq_ref[...])
    ykv = _dot(_rms(c_kv, kva_ref[...]).astype(BF16), wkv_ref[...].astype(BF16))
    k_rope = _rope_rows(_rope_key_norm(kr_ref[...].astype(F32), krn_ref[...]), cos, sin, half).astype(BF16)
    for h in range(MLA_HEADS):
        lo = h * MLA_QK_PAD
        nope = _rms(yq[:, lo:lo + MLA_NOPE_DIM], qn_ref[...])
        r = _rope_rows(_rope_key_norm(yq[:, lo + MLA_NOPE_DIM:lo + MLA_QK_PAD], qr_ref[...]), cos, sin, half)
        q_ref[:, lo:lo + MLA_NOPE_DIM] = (nope * scale).astype(BF16)
        q_ref[:, lo + MLA_NOPE_DIM:lo + MLA_QK_PAD] = (r * scale).astype(BF16)
        k_ref[:, lo:lo + MLA_NOPE_DIM] = _rms(ykv[:, lo:lo + MLA_NOPE_DIM], kn_ref[...]).astype(BF16)
        k_ref[:, lo + MLA_NOPE_DIM:lo + MLA_QK_PAD] = k_rope
        v_ref[:, h * MLA_V_DIM:(h + 1) * MLA_V_DIM] = ykv[:, lo + MLA_NOPE_DIM:lo + MLA_QK_PAD].astype(BF16)


def _mla_up(z_a, z_kr, qa_norm, kva_norm, w_uq_pad, w_ukv, qn_norm, qr_norm_pad, kn_norm, kr_norm_pad, cos_t, sin_t,
            layer, seq, *, tm=528):
    m = z_a.shape[0]
    seq_tiles = seq // tm
    kv_w = MLA_NOPE_DIM + MLA_V_DIM
    half_rank = MLA_Q_RANK // 2

    def row(width, col):
        return pl.BlockSpec((tm, width), lambda i: (i, col))

    def vec(width):
        return pl.BlockSpec((1, width), lambda i: (0, 0))

    return pl.pallas_call(
        _mla_up_body,
        grid=(m // tm,),
        in_specs=[
            row(half_rank, A_MLA_CQ // half_rank), row(half_rank, A_MLA_CQ // half_rank + 1),
            row(half_rank, A_MLA_CKV // half_rank), row(half_rank, A_MLA_CKV // half_rank + 1),
            row(BLOCK, 0),
            vec(MLA_Q_RANK), vec(MLA_KV_RANK),
            pl.BlockSpec((None, MLA_Q_RANK, MLA_HEADS * MLA_QK_PAD), lambda i: (layer, 0, 0)),
            pl.BlockSpec((None, MLA_KV_RANK, MLA_HEADS * kv_w), lambda i: (layer, 0, 0)),
            vec(MLA_NOPE_DIM), vec(BLOCK), vec(MLA_NOPE_DIM), vec(BLOCK),
            pl.BlockSpec((tm, BLOCK), lambda i: (i % seq_tiles, 0)),
            pl.BlockSpec((tm, BLOCK), lambda i: (i % seq_tiles, 0)),
        ],
        out_specs=[row(MLA_HEADS * MLA_QK_PAD, 0), row(MLA_HEADS * MLA_QK_PAD, 0), row(MLA_OUT, 0)],
        out_shape=[
            jax.ShapeDtypeStruct((m, MLA_HEADS * MLA_QK_PAD), BF16),
            jax.ShapeDtypeStruct((m, MLA_HEADS * MLA_QK_PAD), BF16),
            jax.ShapeDtypeStruct((m, MLA_OUT), BF16),
        ],
        compiler_params=_params(("parallel",)),
        name="mla_up",
    )(z_a, z_a, z_a, z_a, z_kr, qa_norm.reshape(1, -1), kva_norm.reshape(1, -1), w_uq_pad, w_ukv,
      qn_norm.reshape(1, -1), qr_norm_pad, kn_norm.reshape(1, -1), kr_norm_pad, cos_t, sin_t)


def _mla_attn_body(q_ref, k_ref, v_ref, o_ref, m_ref, acc_ref, *, t, hb):
    i = pl.program_id(2)
    m_ref[...] = jnp.full(m_ref.shape, NEG_INF, F32)
    acc_ref[...] = jnp.zeros(acc_ref.shape, F32)

    def tile(blk, nblk, masked):
        width = nblk * t
        off = blk * t if isinstance(blk, int) else pl.multiple_of(blk * t, t)
        if masked:
            row = i * t + lax.broadcasted_iota(jnp.int32, (t, width), 0)
            col = blk * t + lax.broadcasted_iota(jnp.int32, (t, width), 1)
            ok = (col <= row) & (col >= PAD)
        for h in range(hb):
            q = q_ref[:, h * MLA_QK_PAD:(h + 1) * MLA_QK_PAD]
            kb = k_ref[pl.ds(off, width), h * MLA_QK_PAD:(h + 1) * MLA_QK_PAD]
            vb = v_ref[pl.ds(off, width), h * MLA_V_DIM:(h + 1) * MLA_V_DIM]
            s = _dot_nt(q, kb)
            if masked:
                s = jnp.where(ok, s, NEG_INF)
            m_run = m_ref[h]
            m_new = jnp.maximum(m_run, jnp.max(s, axis=-1, keepdims=True))
            alpha = jnp.exp2(m_run - m_new)
            p = jnp.concatenate(
                [jnp.exp2(s[:, c * BLOCK:(c + 1) * BLOCK] - m_new) for c in range(width // BLOCK)], axis=1)
            v_aug = jnp.concatenate([vb, jnp.ones_like(vb)], axis=1)
            acc_ref[h] = jnp.concatenate([alpha, alpha], axis=1) * acc_ref[h] + _dot(p.astype(BF16), v_aug)
            m_ref[h] = m_new

    @pl.when(i >= 2)
    def _():
        tile(0, 2, True)

    def body(jj, carry):
        tile(2 * jj, 2, False)
        return carry

    lax.fori_loop(1, i // 2, body, 0)

    @pl.when(i % 2 == 1)
    def _():
        tile(i - 1, 1, True)

    tile(i, 1, True)

    for h in range(hb):
        acc = acc_ref[h]
        o_ref[:, h * MLA_V_DIM:(h + 1) * MLA_V_DIM] = (acc[:, :MLA_V_DIM] / acc[:, MLA_V_DIM:]).astype(BF16)


def _mla_attn(q, k, v, seq, batch, *, t=384, hb=4):
    m = q.shape[0]
    nq = seq // t
    return pl.pallas_call(
        functools.partial(_mla_attn_body, t=t, hb=hb),
        grid=(batch, MLA_HEADS // hb, nq),
        in_specs=[
            pl.BlockSpec((t, hb * MLA_QK_PAD), lambda b, h, i: (b * nq + i, h)),
            pl.BlockSpec((seq, hb * MLA_QK_PAD), lambda b, h, i: (b, h)),
            pl.BlockSpec((seq, hb * MLA_V_DIM), lambda b, h, i: (b, h)),
        ],
        out_specs=pl.BlockSpec((t, hb * MLA_V_DIM), lambda b, h, i: (b * nq + i, h)),
        out_shape=jax.ShapeDtypeStruct((m, MLA_OUT), BF16),
        scratch_shapes=[pltpu.VMEM((hb, t, BLOCK), F32), pltpu.VMEM((hb, t, 2 * MLA_V_DIM), F32)],
        compiler_params=_params(("parallel", "parallel", "arbitrary")),
        name="mla_attn",
    )(q, k, v)


def _ret_body(q_ref, k_ref, v_ref, g_ref, cos_ref, sin_ref, dec_ref, zeta_ref, xi_ref, gn_ref, o_ref,
              state_ref, *, chunk_decay):
    n = pl.program_id(0)

    @pl.when(n == 0)
    def _():
        state_ref[...] = jnp.zeros_like(state_ref)

    cos = cos_ref[...]
    sin = sin_ref[...]
    valid = (n * BLOCK + lax.broadcasted_iota(jnp.int32, (BLOCK, 1), 0) >= PAD).astype(F32)
    for b in range(q_ref.shape[0]):
        for h in range(RET_HEADS):
            ks = slice(h * RET_KEY_DIM, (h + 1) * RET_KEY_DIM)
            vs = slice(h * RET_VAL_DIM, (h + 1) * RET_VAL_DIM)
            qh = _rope_rows(q_ref[b, :, ks].astype(F32), cos, sin, RET_KEY_DIM // 2)
            kh = (_rope_rows(k_ref[b, :, ks].astype(F32), cos, sin, RET_KEY_DIM // 2)
                  * (RET_KEY_DIM ** -0.5) * valid)
            vb = v_ref[b, :, vs]
            s = _dot_nt(qh.astype(BF16), kh.astype(BF16)) * dec_ref[h]
            inner = _dot(s.astype(BF16), vb)
            prev = state_ref[b, h]
            cross = _dot((qh * xi_ref[h]).astype(BF16), prev.astype(BF16))
            kz_t = (kh * zeta_ref[h]).T.astype(BF16)
            state_ref[b, h] = prev * chunk_decay[h] + _dot(kz_t, vb)
            o = inner + cross
            mu = jnp.mean(o, axis=-1, keepdims=True)
            d = o - mu
            y = d * lax.rsqrt(jnp.mean(d * d, axis=-1, keepdims=True) + EPS)
            gate = g_ref[b, :, vs].astype(F32)
            o_ref[b, :, vs] = (y * gn_ref[:, vs] * (gate * _sigmoid(gate))).astype(BF16)


def _retention(z, cos_t, sin_t, decay, zeta_b, xi_b, gn, chunk_decay, nb, batch):
    m = z.shape[0]
    z = z.reshape(batch, nb * BLOCK, z.shape[1])
    tab = pl.BlockSpec((RET_HEADS, BLOCK, BLOCK), lambda n: (0, 0, 0))

    def rows(width, col):
        return pl.BlockSpec((batch, BLOCK, width), lambda n: (0, n, col // width))

    out = pl.pallas_call(
        functools.partial(_ret_body, chunk_decay=chunk_decay),
        grid=(nb,),
        in_specs=[
            rows(RET_QK_W, B_RET_Q), rows(RET_QK_W, B_RET_K), rows(RET_OUT, B_RET_V), rows(RET_OUT, B_RET_G),
            pl.BlockSpec((BLOCK, BLOCK), lambda n: (n, 0)),
            pl.BlockSpec((BLOCK, BLOCK), lambda n: (n, 0)),
            tab, tab, tab,
            pl.BlockSpec((1, RET_OUT), lambda n: (0, 0)),
        ],
        out_specs=pl.BlockSpec((batch, BLOCK, RET_OUT), lambda n: (0, n, 0)),
        out_shape=jax.ShapeDtypeStruct((batch, nb * BLOCK, RET_OUT), BF16),
        scratch_shapes=[pltpu.VMEM((batch, RET_HEADS, RET_KEY_DIM, RET_VAL_DIM), F32)],
        compiler_params=_params(("arbitrary",)),
        name="retention",
    )(z, z, z, z, cos_t, sin_t, decay, zeta_b, xi_b, gn.reshape(1, -1))
    return out.reshape(m, RET_OUT)


def _merge_body(oa_ref, ob_ref, oc_ref, wa_ref, wb_ref, wc_ref, ga_ref, gb_ref, gc_ref, o_ref):
    def branch(o, w, g):
        return _sigmoid(g[...].astype(F32)) * _dot(o[...], w[...].astype(BF16))

    o_ref[...] = (branch(oa_ref, wa_ref, ga_ref) + branch(ob_ref, wb_ref, gb_ref)
                  + branch(oc_ref, wc_ref, gc_ref)).astype(BF16)


def _merge(z, o_a, o_b, o_c, w_a, w_b, w_c, layer, *, tm=1056, tn=512):
    m = z.shape[0]
    nt = D_MODEL // tn

    def act(width):
        return pl.BlockSpec((tm, width), lambda i, j: (i, 0))

    def wgt(width):
        return pl.BlockSpec((None, width, tn), lambda i, j: (layer, 0, j))

    def gate(br):
        return pl.BlockSpec((tm, tn), lambda i, j: (i, B_GATE // tn + br * nt + j))

    return pl.pallas_call(
        _merge_body,
        grid=(m // tm, nt),
        in_specs=[act(SWA_OUT), act(MLA_OUT), act(RET_OUT), wgt(SWA_OUT), wgt(MLA_OUT), wgt(RET_OUT),
                  gate(0), gate(1), gate(2)],
        out_specs=pl.BlockSpec((tm, tn), lambda i, j: (i, j)),
        out_shape=jax.ShapeDtypeStruct((m, D_MODEL), BF16),
        compiler_params=_params(("parallel", "arbitrary")),
        name="merge",
    )(o_a, o_b, o_c, w_a, w_b, w_c, z, z, z)


def _outproj_body(x_ref, w_ref, r_ref, o_ref):
    o_ref[...] = r_ref[...] + _dot(x_ref[...], w_ref[...].astype(BF16))


def _outproj(x, w, res, layer, *, tm=1056, tn=512):
    m = x.shape[0]
    return pl.pallas_call(
        _outproj_body,
        grid=(m // tm, D_MODEL // tn),
        in_specs=[
            pl.BlockSpec((tm, D_MODEL), lambda i, j: (i, 0)),
            pl.BlockSpec((None, D_MODEL, tn), lambda i, j: (layer, 0, j)),
            pl.BlockSpec((tm, tn), lambda i, j: (i, j)),
        ],
        out_specs=pl.BlockSpec((tm, tn), lambda i, j: (i, j)),
        out_shape=jax.ShapeDtypeStruct((m, D_MODEL), F32),
        compiler_params=_params(("parallel", "arbitrary")),
        name="outproj",
    )(x, w, res)


def _rope_tables(pos, dim, width):
    half = dim // 2
    inv_freq = ROPE_BASE ** (-jnp.arange(half, dtype=F32) / half)
    ang = pos[:, None] * inv_freq[None, :]
    cos = jnp.cos(ang)
    sin = jnp.sin(ang)
    fill = jnp.zeros((pos.shape[0], width - dim), F32)
    return (jnp.concatenate([cos, cos, fill], axis=1), jnp.concatenate([-sin, sin, fill], axis=1))


def kernel(x, meta_tokens, ffn1_norm, ffn1_w_gate, ffn1_w_up, ffn1_w_down, mix_norm, w_in, swa_q_norm, swa_k_norm, swa_sinks, mla_q_a_norm, mla_w_uq, mla_kv_a_norm, mla_w_ukv, mla_qn_norm, mla_qr_norm, mla_kn_norm, mla_kr_norm, ret_gn, w_br_swa, w_br_mla, w_br_ret, w_o, ffn2_norm, ffn2_w_gate, ffn2_w_up, ffn2_w_down):
    batch, seq_in, _ = x.shape
    seq = seq_in + BLOCK
    nb = seq // BLOCK

    meta = jnp.broadcast_to(meta_tokens[None].astype(x.dtype), (batch, N_META, D_MODEL))
    h = jnp.concatenate([jnp.zeros((batch, PAD, D_MODEL), x.dtype), meta, x], axis=1)
    h = h.reshape(batch * seq, D_MODEL)
    pos = (jnp.arange(seq) - PAD).astype(F32)

    w_in_t = jnp.swapaxes(w_in, 1, 2)

    qk_dim = MLA_NOPE_DIM + MLA_ROPE_DIM
    w_uq_pad = jnp.pad(mla_w_uq.reshape(DEPTH, MLA_Q_RANK, MLA_HEADS, qk_dim),
                       ((0, 0), (0, 0), (0, 0), (0, MLA_QK_PAD - qk_dim)))
    w_uq_pad = w_uq_pad.reshape(DEPTH, MLA_Q_RANK, MLA_HEADS * MLA_QK_PAD).astype(BF16)
    rope_fill = jnp.zeros((DEPTH, BLOCK - MLA_ROPE_DIM), F32)
    qr_norm_pad = jnp.concatenate([mla_qr_norm, rope_fill], axis=1)
    kr_norm_pad = jnp.concatenate([mla_kr_norm, rope_fill], axis=1)

    cos_m, sin_m = _rope_tables(pos, MLA_ROPE_DIM, BLOCK)
    cos_r, sin_r = _rope_tables(pos, RET_KEY_DIM, BLOCK)

    log_gamma = jnp.log(1.0 - 2.0 ** (-5.0 - jnp.arange(RET_HEADS, dtype=F32)))
    idx = jnp.arange(BLOCK, dtype=F32)
    diff = idx[:, None] - idx[None, :]
    decay = jnp.where(diff[None] >= 0, jnp.exp(jnp.maximum(diff, 0.0)[None] * log_gamma[:, None, None]), 0.0)
    zeta = jnp.exp((BLOCK - 1.0 - idx)[None, :] * log_gamma[:, None])
    xi = jnp.exp((idx + 1.0)[None, :] * log_gamma[:, None])
    zeta_b = jnp.broadcast_to(zeta[:, :, None], (RET_HEADS, BLOCK, RET_KEY_DIM))
    xi_b = jnp.broadcast_to(xi[:, :, None], (RET_HEADS, BLOCK, RET_KEY_DIM))
    chunk_decay = tuple(float(np.exp(BLOCK * np.log(1.0 - 2.0 ** (-5.0 - hh)))) for hh in range(RET_HEADS))

    for l in range(DEPTH):
        h = _ffn(h, ffn1_norm, ffn1_w_gate, ffn1_w_up, ffn1_w_down, l)
        z_a, z_kr, hn = _inproj_a(h, mix_norm, w_in_t, l)
        z_b = _inproj_b(hn, w_in_t, l)
        sink_col = jnp.repeat(swa_sinks[l].astype(F32).reshape(SWA_KV_HEADS, SWA_GROUP), BLOCK, axis=1)
        o_a = _swa(z_a, swa_q_norm[l], swa_k_norm[l], sink_col[:, None, :], nb, batch)
        q, k, v = _mla_up(z_a, z_kr, mla_q_a_norm[l], mla_kv_a_norm[l], w_uq_pad, mla_w_ukv, mla_qn_norm[l],
                          qr_norm_pad[l:l + 1], mla_kn_norm[l], kr_norm_pad[l:l + 1], cos_m, sin_m, l, seq)
        o_b = _mla_attn(q, k, v, seq, batch)
        o_c = _retention(z_b, cos_r, sin_r, decay, zeta_b, xi_b, ret_gn[l], chunk_decay, nb, batch)
        merged = _merge(z_b, o_a, o_b, o_c, w_br_swa, w_br_mla, w_br_ret, l)
        h = _outproj(merged, w_o, h, l)
        h = _ffn(h, ffn2_norm, ffn2_w_gate, ffn2_w_up, ffn2_w_down, l)
    return h.reshape(batch, seq, D_MODEL)[:, BLOCK:]
```

```python
import functools

import jax
import jax.numpy as jnp
import numpy as np
from jax import lax
from jax.experimental import pallas as pl
from jax.experimental.pallas import tpu as pltpu

F32 = jnp.float32
BF16 = jnp.bfloat16

D_MODEL = 2048
DEPTH = 4
N_META = 16
BLOCK = 128
PAD = BLOCK - N_META
EPS = 1e-6
NEG_INF = -1e30
ROPE_BASE = 10000.0
HALF_STEP = 0.5
LOG2_E = 1.4426950408889634

SWA_HEADS = 16
SWA_KV_HEADS = 2
SWA_HEAD_DIM = 64
SWA_GROUP = SWA_HEADS // SWA_KV_HEADS
WINDOW = 128

MLA_HEADS = 8
MLA_Q_RANK = 512
MLA_KV_RANK = 512
MLA_NOPE_DIM = 128
MLA_ROPE_DIM = 64
MLA_V_DIM = 128
MLA_QK_PAD = 256

RET_HEADS = 4
RET_KEY_DIM = 128
RET_VAL_DIM = 256

D_FF = 5632
N_BRANCH = 3

SWA_OUT = SWA_HEADS * SWA_HEAD_DIM
SWA_KV_W = SWA_KV_HEADS * SWA_HEAD_DIM
MLA_OUT = MLA_HEADS * MLA_V_DIM
RET_QK_W = RET_HEADS * RET_KEY_DIM
RET_OUT = RET_HEADS * RET_VAL_DIM

A_SWA_Q = 0
A_SWA_K = A_SWA_Q + SWA_OUT
A_SWA_V = A_SWA_K + SWA_KV_W
A_MLA_CQ = A_SWA_V + SWA_KV_W
A_MLA_CKV = A_MLA_CQ + MLA_Q_RANK
IN_KR = A_MLA_CKV + MLA_KV_RANK
IN_B = IN_KR + MLA_ROPE_DIM
B_RET_Q = 0
B_RET_K = B_RET_Q + RET_QK_W
B_RET_V = B_RET_K + RET_QK_W
B_RET_G = B_RET_V + RET_OUT
B_GATE = B_RET_G + RET_OUT
B_WIDTH = B_GATE + N_BRANCH * D_MODEL
IN_WIDTH = IN_B + B_WIDTH

VMEM_LIMIT = 56 * 1024 * 1024
VMEM_LIMIT_FFN = 60 * 1024 * 1024


def _params(semantics, vmem_limit=VMEM_LIMIT):
    return pltpu.CompilerParams(dimension_semantics=semantics, vmem_limit_bytes=vmem_limit)


def _rms(x, g):
    return x * lax.rsqrt(jnp.mean(x * x, axis=-1, keepdims=True) + EPS) * g


def _sigmoid(x):
    return 1.0 / (1.0 + jnp.exp(-x))


def _dot(a, b):
    return jnp.dot(a, b, preferred_element_type=F32)


def _dot_nt(a, b):
    return lax.dot_general(a, b, (((1,), (1,)), ((), ())), preferred_element_type=F32)


FFN_TM = 1056
FFN_TF_HEAD = 256
FFN_TF_TAIL = 512


def _ffn_start(x_ref, g_ref, o_ref, xn_ref):
    x = x_ref[...]
    xn_ref[...] = _rms(x, g_ref[...]).astype(BF16)
    o_ref[...] = x


def _ffn_accumulate(xn_ref, wg, wu, wd, o_ref):
    xn = xn_ref[...]
    gate = _dot(xn, wg)
    up = _dot(xn, wu)
    act = (gate * _sigmoid(gate) * up).astype(BF16)
    o_ref[...] += HALF_STEP * _dot(act, wd)


def _ffn_head_body(x_ref, g_ref, wg_ref, wu_ref, wd_ref, g2_ref, o_ref, wgb_ref, wub_ref, wdb_ref, hn_ref, *,
                   emit_norm):
    f = pl.program_id(0)

    @pl.when(f == 0)
    def _():
        _ffn_start(x_ref, g_ref, o_ref, hn_ref)

    wg = wg_ref[...].astype(BF16)
    wu = wu_ref[...].astype(BF16)
    wd = wd_ref[...].astype(BF16)
    wgb_ref[...] = wg
    wub_ref[...] = wu
    wdb_ref[...] = wd
    _ffn_accumulate(hn_ref, wg, wu, wd, o_ref)

    if emit_norm:
        @pl.when(f == pl.num_programs(0) - 1)
        def _():
            hn_ref[...] = _rms(o_ref[...], g2_ref[...]).astype(BF16)


def _ffn_tail_body(x_ref, g_ref, wg_ref, wu_ref, wd_ref, g2_ref, head_ref, head_n_ref, o_ref, hn_ref, *, emit_norm):
    del head_ref, head_n_ref
    f = pl.program_id(1)

    @pl.when(f == 0)
    def _():
        _ffn_start(x_ref, g_ref, o_ref, hn_ref)

    _ffn_accumulate(hn_ref, wg_ref[...], wu_ref[...], wd_ref[...], o_ref)

    if emit_norm:
        @pl.when(f == pl.num_programs(1) - 1)
        def _():
            hn_ref[...] = _rms(o_ref[...], g2_ref[...]).astype(BF16)


def _ffn(h, gain, w_gate, w_up, w_down, next_gain, layer, emit_norm):
    m = h.shape[0]
    tm, tfh, tft = FFN_TM, FFN_TF_HEAD, FFN_TF_TAIL
    gain = gain.reshape(DEPTH, 1, D_MODEL)
    next_gain = next_gain.reshape(DEPTH, 1, D_MODEL)
    head, wg_b, wu_b, wd_b, head_n = pl.pallas_call(
        functools.partial(_ffn_head_body, emit_norm=emit_norm),
        grid=(D_FF // tfh,),
        in_specs=[
            pl.BlockSpec((tm, D_MODEL), lambda f: (0, 0), pipeline_mode=pl.Buffered(1)),
            pl.BlockSpec((None, 1, D_MODEL), lambda f: (layer, 0, 0)),
            pl.BlockSpec((None, D_MODEL, tfh), lambda f: (layer, 0, f)),
            pl.BlockSpec((None, D_MODEL, tfh), lambda f: (layer, 0, f)),
            pl.BlockSpec((None, tfh, D_MODEL), lambda f: (layer, f, 0)),
            pl.BlockSpec((None, 1, D_MODEL), lambda f: (layer, 0, 0)),
        ],
        out_specs=[
            pl.BlockSpec((tm, D_MODEL), lambda f: (0, 0)),
            pl.BlockSpec((D_MODEL, tfh), lambda f: (0, f)),
            pl.BlockSpec((D_MODEL, tfh), lambda f: (0, f)),
            pl.BlockSpec((tfh, D_MODEL), lambda f: (f, 0)),
            pl.BlockSpec((tm, D_MODEL), lambda f: (0, 0)),
        ],
        out_shape=[
            jax.ShapeDtypeStruct((m, D_MODEL), F32),
            jax.ShapeDtypeStruct((D_MODEL, D_FF), BF16),
            jax.ShapeDtypeStruct((D_MODEL, D_FF), BF16),
            jax.ShapeDtypeStruct((D_FF, D_MODEL), BF16),
            jax.ShapeDtypeStruct((m, D_MODEL), BF16),
        ],
        compiler_params=_params(("arbitrary",), VMEM_LIMIT_FFN),
        name="ffn_head",
    )(h, gain, w_gate, w_up, w_down, next_gain)
    return pl.pallas_call(
        functools.partial(_ffn_tail_body, emit_norm=emit_norm),
        grid=(m // tm - 1, D_FF // tft),
        in_specs=[
            pl.BlockSpec((tm, D_MODEL), lambda i, f: (i + 1, 0)),
            pl.BlockSpec((None, 1, D_MODEL), lambda i, f: (layer, 0, 0)),
            pl.BlockSpec((D_MODEL, tft), lambda i, f: (0, f)),
            pl.BlockSpec((D_MODEL, tft), lambda i, f: (0, f)),
            pl.BlockSpec((tft, D_MODEL), lambda i, f: (f, 0)),
            pl.BlockSpec((None, 1, D_MODEL), lambda i, f: (layer, 0, 0)),
            pl.BlockSpec(memory_space=pl.ANY),
            pl.BlockSpec(memory_space=pl.ANY),
        ],
        out_specs=[
            pl.BlockSpec((tm, D_MODEL), lambda i, f: (i + 1, 0)),
            pl.BlockSpec((tm, D_MODEL), lambda i, f: (i + 1, 0)),
        ],
        out_shape=[
            jax.ShapeDtypeStruct((m, D_MODEL), F32),
            jax.ShapeDtypeStruct((m, D_MODEL), BF16),
        ],
        input_output_aliases={6: 0, 7: 1},
        compiler_params=_params(("parallel", "arbitrary"), VMEM_LIMIT_FFN),
        name="ffn_tail",
    )(h, gain, wg_b, wu_b, wd_b, next_gain, head, head_n)


def _inproj_a_body(xn_ref, w_ref, wkr_ref, za_ref, zk_ref):
    @pl.when(pl.program_id(1) == 0)
    def _():
        kr = _dot_nt(xn_ref[...], wkr_ref[...].astype(BF16))
        zk_ref[...] = jnp.concatenate([kr, jnp.zeros_like(kr)], axis=1).astype(BF16)

    za_ref[...] = _dot_nt(xn_ref[...], w_ref[...].astype(BF16)).astype(BF16)


def _inproj_a(xn, w_in_t, layer, *, tm=2112, tn=768):
    m = xn.shape[0]
    return pl.pallas_call(
        _inproj_a_body,
        grid=(m // tm, IN_KR // tn),
        in_specs=[
            pl.BlockSpec((tm, D_MODEL), lambda i, j: (i, 0)),
            pl.BlockSpec((None, tn, D_MODEL), lambda i, j: (layer, j, 0)),
            pl.BlockSpec((None, MLA_ROPE_DIM, D_MODEL), lambda i, j: (layer, IN_KR // MLA_ROPE_DIM, 0)),
        ],
        out_specs=[
            pl.BlockSpec((tm, tn), lambda i, j: (i, j)),
            pl.BlockSpec((tm, BLOCK), lambda i, j: (i, 0)),
        ],
        out_shape=[
            jax.ShapeDtypeStruct((m, IN_KR), BF16),
            jax.ShapeDtypeStruct((m, BLOCK), BF16),
        ],
        compiler_params=_params(("parallel", "arbitrary")),
        name="inproj_a",
    )(xn, w_in_t, w_in_t)


def _inproj_b_body(xn_ref, w_ref, o_ref):
    o_ref[...] = _dot_nt(xn_ref[...], w_ref[0].astype(BF16)).astype(BF16)


def _inproj_b(xn, w_in_t, layer, *, tm=2112, tn=768):
    m = xn.shape[0]
    return pl.pallas_call(
        _inproj_b_body,
        grid=(m // tm, B_WIDTH // tn),
        in_specs=[
            pl.BlockSpec((tm, D_MODEL), lambda i, j: (i, 0)),
            pl.BlockSpec((pl.Element(1), pl.Element(tn), pl.Element(D_MODEL)),
                         lambda i, j: (layer, pl.multiple_of(IN_B + j * tn, 8), 0)),
        ],
        out_specs=pl.BlockSpec((tm, tn), lambda i, j: (i, j)),
        out_shape=jax.ShapeDtypeStruct((m, B_WIDTH), BF16),
        compiler_params=_params(("parallel", "arbitrary")),
        name="inproj_b",
    )(xn, w_in_t)


def _swa_body(q_ref, kc_ref, kp_ref, km_ref, vc_ref, vp_ref, vm_ref, qn_ref, kn_ref, sink_ref, o_ref):
    n = pl.program_id(0)
    j = lax.broadcasted_iota(jnp.int32, (BLOCK, BLOCK), 0)
    i_loc = lax.broadcasted_iota(jnp.int32, (BLOCK, BLOCK), 1)
    ok_cur = (j <= i_loc) & (n * BLOCK + j >= PAD)
    ok_prev = (j > i_loc) & ((n - 1) * BLOCK + j >= PAD)
    ok_meta = (j >= PAD) & (n * BLOCK + i_loc - j >= WINDOW)

    def mask_rows(s_blk, ok):
        return jnp.concatenate(
            [jnp.where(ok, s_blk[:, h * BLOCK:(h + 1) * BLOCK], NEG_INF) for h in range(SWA_GROUP)], axis=1)

    k_gain = kn_ref[...] * qn_ref[...] * (SWA_HEAD_DIM ** -0.5)
    for b in range(q_ref.shape[0]):
        _swa_block(b, q_ref, kc_ref, kp_ref, km_ref, vc_ref, vp_ref, vm_ref, sink_ref, o_ref, k_gain,
                   (ok_meta, ok_prev, ok_cur), mask_rows)


def _swa_block(b, q_ref, kc_ref, kp_ref, km_ref, vc_ref, vp_ref, vm_ref, sink_ref, o_ref, k_gain, oks, mask_rows):
    ok_meta, ok_prev, ok_cur = oks
    q_t = [q_ref[b, :, a * BLOCK:(a + 1) * BLOCK].astype(F32).T for a in range(SWA_HEADS // 2)]
    v_t = jnp.concatenate([r[b].astype(F32).T for r in (vm_ref, vp_ref, vc_ref)], axis=1)

    out_t = []
    for g in range(SWA_KV_HEADS):
        lo = g * SWA_HEAD_DIM
        hi = lo + SWA_HEAD_DIM
        heads = [g * SWA_GROUP + h for h in range(SWA_GROUP)]
        qt = jnp.concatenate(
            [q_t[hd // 2][(hd % 2) * SWA_HEAD_DIM:(hd % 2 + 1) * SWA_HEAD_DIM, :] for hd in heads], axis=1)
        q_rinv = lax.rsqrt(jnp.sum(qt * qt, axis=0, keepdims=True) * (1.0 / SWA_HEAD_DIM) + EPS)
        k_all = jnp.concatenate(
            [_rms(r[b, :, lo:hi].astype(F32), k_gain) for r in (km_ref, kp_ref, kc_ref)], axis=0).astype(BF16)
        s = _dot(k_all, qt.astype(BF16)) * q_rinv
        s = jnp.concatenate([mask_rows(s[:BLOCK], ok_meta), mask_rows(s[BLOCK:2 * BLOCK], ok_prev),
                             mask_rows(s[2 * BLOCK:], ok_cur)], axis=0)
        sink = sink_ref[g]
        mx = jnp.maximum(jnp.max(s, axis=0, keepdims=True), sink)
        p = jnp.exp(s - mx)
        den = jnp.sum(p, axis=0, keepdims=True) + jnp.exp(sink - mx)
        o_t = _dot(v_t[lo:hi].astype(BF16), p.astype(BF16)) / den
        out_t.extend(o_t[:, h * BLOCK:(h + 1) * BLOCK] for h in range(SWA_GROUP))
    for a in range(SWA_HEADS // 2):
        pair = jnp.concatenate([out_t[2 * a], out_t[2 * a + 1]], axis=0)
        o_ref[b, :, a * BLOCK:(a + 1) * BLOCK] = pair.T.astype(BF16)


def _swa(z, q_norm, k_norm, sink_col, nb, batch):
    m = z.shape[0]
    z = z.reshape(batch, nb * BLOCK, z.shape[1])
    kcol = A_SWA_K // SWA_KV_W
    vcol = A_SWA_V // SWA_KV_W

    def cur(c):
        return pl.BlockSpec((batch, BLOCK, SWA_KV_W), lambda n: (0, n, c))

    def prev(c):
        return pl.BlockSpec((batch, BLOCK, SWA_KV_W), lambda n: (0, jnp.maximum(n - 1, 0), c))

    def meta(c):
        return pl.BlockSpec((batch, BLOCK, SWA_KV_W), lambda n: (0, 0, c))

    out = pl.pallas_call(
        _swa_body,
        grid=(nb,),
        in_specs=[
            pl.BlockSpec((batch, BLOCK, SWA_OUT), lambda n: (0, n, A_SWA_Q // SWA_OUT)),
            cur(kcol), prev(kcol), meta(kcol),
            cur(vcol), prev(vcol), meta(vcol),
            pl.BlockSpec((1, SWA_HEAD_DIM), lambda n: (0, 0)),
            pl.BlockSpec((1, SWA_HEAD_DIM), lambda n: (0, 0)),
            pl.BlockSpec((SWA_KV_HEADS, 1, SWA_GROUP * BLOCK), lambda n: (0, 0, 0)),
        ],
        out_specs=pl.BlockSpec((batch, BLOCK, SWA_OUT), lambda n: (0, n, 0)),
        out_shape=jax.ShapeDtypeStruct((batch, nb * BLOCK, SWA_OUT), BF16),
        compiler_params=_params(("arbitrary",)),
        name="swa",
    )(z, z, z, z, z, z, z, q_norm.reshape(1, -1), k_norm.reshape(1, -1), sink_col)
    return out.reshape(m, SWA_OUT)


def _rope_rows(x, cos, sin, half):
    width = x.shape[-1]
    lane = lax.broadcasted_iota(jnp.int32, x.shape, 1)
    rot = jnp.where(lane < half, pltpu.roll(x, width - half, 1), pltpu.roll(x, half, 1))
    return x * cos + rot * sin


def _rope_key_norm(r, gain):
    return r * lax.rsqrt(jnp.sum(r * r, axis=-1, keepdims=True) * (1.0 / MLA_ROPE_DIM) + EPS) * gain


def _mla_up_body(cq0_ref, cq1_ref, ckv0_ref, ckv1_ref, kr_ref, qa_ref, kva_ref, wq_ref, wkv_ref, qn_ref, qr_ref,
                 kn_ref, krn_ref, cos_ref, sin_ref, q_ref, k_ref, v_ref):
    cos = cos_ref[...]
    sin = sin_ref[...]
    half = MLA_ROPE_DIM // 2
    scale = (MLA_NOPE_DIM + MLA_ROPE_DIM) ** -0.5 * LOG2_E
    c_q = jnp.concatenate([cq0_ref[...], cq1_ref[...]], axis=1).astype(F32)
    c_kv = jnp.concatenate([ckv0_ref[...], ckv1_ref[...]], axis=1).astype(F32)
    yq = _dot(_rms(c_q, qa_ref[...]).astype(BF16), wq_ref[...])
    ykv = _dot(_rms(c_kv, kva_ref[...]).astype(BF16), wkv_ref[...].astype(BF16))
    k_rope = _rope_rows(_rope_key_norm(kr_ref[...].astype(F32), krn_ref[...]), cos, sin, half).astype(BF16)
    for h in range(MLA_HEADS):
        lo = h * MLA_QK_PAD
        nope = _rms(yq[:, lo:lo + MLA_NOPE_DIM], qn_ref[...])
        r = _rope_rows(_rope_key_norm(yq[:, lo + MLA_NOPE_DIM:lo + MLA_QK_PAD], qr_ref[...]), cos, sin, half)
        q_ref[:, lo:lo + MLA_NOPE_DIM] = (nope * scale).astype(BF16)
        q_ref[:, lo + MLA_NOPE_DIM:lo + MLA_QK_PAD] = (r * scale).astype(BF16)
        k_ref[:, lo:lo + MLA_NOPE_DIM] = _rms(ykv[:, lo:lo + MLA_NOPE_DIM], kn_ref[...]).astype(BF16)
        k_ref[:, lo + MLA_NOPE_DIM:lo + MLA_QK_PAD] = k_rope
        v_ref[:, h * MLA_V_DIM:(h + 1) * MLA_V_DIM] = ykv[:, lo + MLA_NOPE_DIM:lo + MLA_QK_PAD].astype(BF16)


def _mla_up(z_a, z_kr, qa_norm, kva_norm, w_uq_pad, w_ukv, qn_norm, qr_norm_pad, kn_norm, kr_norm_pad, cos_t, sin_t,
            layer, seq, *, tm=528):
    m = z_a.shape[0]
    seq_tiles = seq // tm
    kv_w = MLA_NOPE_DIM + MLA_V_DIM
    half_rank = MLA_Q_RANK // 2

    def row(width, col):
        return pl.BlockSpec((tm, width), lambda i: (i, col))

    def vec(width):
        return pl.BlockSpec((1, width), lambda i: (0, 0))

    return pl.pallas_call(
        _mla_up_body,
        grid=(m // tm,),
        in_specs=[
            row(half_rank, A_MLA_CQ // half_rank), row(half_rank, A_MLA_CQ // half_rank + 1),
            row(half_rank, A_MLA_CKV // half_rank), row(half_rank, A_MLA_CKV // half_rank + 1),
            row(BLOCK, 0),
            vec(MLA_Q_RANK), vec(MLA_KV_RANK),
            pl.BlockSpec((None, MLA_Q_RANK, MLA_HEADS * MLA_QK_PAD), lambda i: (layer, 0, 0)),
            pl.BlockSpec((None, MLA_KV_RANK, MLA_HEADS * kv_w), lambda i: (layer, 0, 0)),
            vec(MLA_NOPE_DIM), vec(BLOCK), vec(MLA_NOPE_DIM), vec(BLOCK),
            pl.BlockSpec((tm, BLOCK), lambda i: (i % seq_tiles, 0)),
            pl.BlockSpec((tm, BLOCK), lambda i: (i % seq_tiles, 0)),
        ],
        out_specs=[row(MLA_HEADS * MLA_QK_PAD, 0), row(MLA_HEADS * MLA_QK_PAD, 0), row(MLA_OUT, 0)],
        out_shape=[
            jax.ShapeDtypeStruct((m, MLA_HEADS * MLA_QK_PAD), BF16),
            jax.ShapeDtypeStruct((m, MLA_HEADS * MLA_QK_PAD), BF16),
            jax.ShapeDtypeStruct((m, MLA_OUT), BF16),
        ],
        compiler_params=_params(("parallel",)),
        name="mla_up",
    )(z_a, z_a, z_a, z_a, z_kr, qa_norm.reshape(1, -1), kva_norm.reshape(1, -1), w_uq_pad, w_ukv,
      qn_norm.reshape(1, -1), qr_norm_pad, kn_norm.reshape(1, -1), kr_norm_pad, cos_t, sin_t)


def _mla_attn_body(q_ref, k_ref, v_ref, o_ref, m_ref, acc_ref, *, t, hb):
    i = pl.program_id(2)
    m_ref[...] = jnp.full(m_ref.shape, NEG_INF, F32)
    acc_ref[...] = jnp.zeros(acc_ref.shape, F32)

    def tile(blk, nblk, masked):
        width = nblk * t
        off = blk * t if isinstance(blk, int) else pl.multiple_of(blk * t, t)
        if masked:
            row = i * t + lax.broadcasted_iota(jnp.int32, (t, width), 0)
            col = blk * t + lax.broadcasted_iota(jnp.int32, (t, width), 1)
            ok = (col <= row) & (col >= PAD)
        for h in range(hb):
            q = q_ref[:, h * MLA_QK_PAD:(h + 1) * MLA_QK_PAD]
            kb = k_ref[pl.ds(off, width), h * MLA_QK_PAD:(h + 1) * MLA_QK_PAD]
            vb = v_ref[pl.ds(off, width), h * MLA_V_DIM:(h + 1) * MLA_V_DIM]
            s = _dot_nt(q, kb)
            if masked:
                s = jnp.where(ok, s, NEG_INF)
            m_run = m_ref[h]
            m_new = jnp.maximum(m_run, jnp.max(s, axis=-1, keepdims=True))
            alpha = jnp.exp2(m_run - m_new)
            p = jnp.concatenate(
                [jnp.exp2(s[:, c * BLOCK:(c + 1) * BLOCK] - m_new) for c in range(width // BLOCK)], axis=1)
            v_aug = jnp.concatenate([vb, jnp.ones_like(vb)], axis=1)
            acc_ref[h] = jnp.concatenate([alpha, alpha], axis=1) * acc_ref[h] + _dot(p.astype(BF16), v_aug)
            m_ref[h] = m_new

    @pl.when(i >= 2)
    def _():
        tile(0, 2, True)

    def body(jj, carry):
        tile(2 * jj, 2, False)
        return carry

    lax.fori_loop(1, i // 2, body, 0)

    @pl.when(i % 2 == 1)
    def _():
        tile(i - 1, 1, True)

    tile(i, 1, True)

    for h in range(hb):
        acc = acc_ref[h]
        o_ref[:, h * MLA_V_DIM:(h + 1) * MLA_V_DIM] = (acc[:, :MLA_V_DIM] / acc[:, MLA_V_DIM:]).astype(BF16)


def _mla_attn(q, k, v, seq, batch, *, t=384, hb=4):
    m = q.shape[0]
    nq = seq // t
    return pl.pallas_call(
        functools.partial(_mla_attn_body, t=t, hb=hb),
        grid=(batch, MLA_HEADS // hb, nq),
        in_specs=[
            pl.BlockSpec((t, hb * MLA_QK_PAD), lambda b, h, i: (b * nq + i, h)),
            pl.BlockSpec((seq, hb * MLA_QK_PAD), lambda b, h, i: (b, h)),
            pl.BlockSpec((seq, hb * MLA_V_DIM), lambda b, h, i: (b, h)),
        ],
        out_specs=pl.BlockSpec((t, hb * MLA_V_DIM), lambda b, h, i: (b * nq + i, h)),
        out_shape=jax.ShapeDtypeStruct((m, MLA_OUT), BF16),
        scratch_shapes=[pltpu.VMEM((hb, t, BLOCK), F32), pltpu.VMEM((hb, t, 2 * MLA_V_DIM), F32)],
        compiler_params=_params(("parallel", "parallel", "arbitrary")),
        name="mla_attn",
    )(q, k, v)


def _ret_body(q_ref, k_ref, v_ref, g_ref, cos_ref, sin_ref, dec_ref, zeta_ref, xi_ref, gn_ref, o_ref,
              state_ref, *, chunk_decay):
    n = pl.program_id(0)

    @pl.when(n == 0)
    def _():
        state_ref[...] = jnp.zeros_like(state_ref)

    cos = cos_ref[...]
    sin = sin_ref[...]
    valid = (n * BLOCK + lax.broadcasted_iota(jnp.int32, (BLOCK, 1), 0) >= PAD).astype(F32)
    for b in range(q_ref.shape[0]):
        for h in range(RET_HEADS):
            ks = slice(h * RET_KEY_DIM, (h + 1) * RET_KEY_DIM)
            vs = slice(h * RET_VAL_DIM, (h + 1) * RET_VAL_DIM)
            qh = _rope_rows(q_ref[b, :, ks].astype(F32), cos, sin, RET_KEY_DIM // 2)
            kh = (_rope_rows(k_ref[b, :, ks].astype(F32), cos, sin, RET_KEY_DIM // 2)
                  * (RET_KEY_DIM ** -0.5) * valid)
            vb = v_ref[b, :, vs]
            s = _dot_nt(qh.astype(BF16), kh.astype(BF16)) * dec_ref[h]
            inner = _dot(s.astype(BF16), vb)
            prev = state_ref[b, h]
            cross = _dot((qh * xi_ref[h]).astype(BF16), prev.astype(BF16))
            kz_t = (kh * zeta_ref[h]).T.astype(BF16)
            state_ref[b, h] = prev * chunk_decay[h] + _dot(kz_t, vb)
            o = inner + cross
            mu = jnp.mean(o, axis=-1, keepdims=True)
            d = o - mu
            y = d * lax.rsqrt(jnp.mean(d * d, axis=-1, keepdims=True) + EPS)
            gate = g_ref[b, :, vs].astype(F32)
            o_ref[b, :, vs] = (y * gn_ref[:, vs] * (gate * _sigmoid(gate))).astype(BF16)


def _retention(z, cos_t, sin_t, decay, zeta_b, xi_b, gn, chunk_decay, nb, batch):
    m = z.shape[0]
    z = z.reshape(batch, nb * BLOCK, z.shape[1])
    tab = pl.BlockSpec((RET_HEADS, BLOCK, BLOCK), lambda n: (0, 0, 0))

    def rows(width, col):
        return pl.BlockSpec((batch, BLOCK, width), lambda n: (0, n, col // width))

    out = pl.pallas_call(
        functools.partial(_ret_body, chunk_decay=chunk_decay),
        grid=(nb,),
        in_specs=[
            rows(RET_QK_W, B_RET_Q), rows(RET_QK_W, B_RET_K), rows(RET_OUT, B_RET_V), rows(RET_OUT, B_RET_G),
            pl.BlockSpec((BLOCK, BLOCK), lambda n: (n, 0)),
            pl.BlockSpec((BLOCK, BLOCK), lambda n: (n, 0)),
            tab, tab, tab,
            pl.BlockSpec((1, RET_OUT), lambda n: (0, 0)),
        ],
        out_specs=pl.BlockSpec((batch, BLOCK, RET_OUT), lambda n: (0, n, 0)),
        out_shape=jax.ShapeDtypeStruct((batch, nb * BLOCK, RET_OUT), BF16),
        scratch_shapes=[pltpu.VMEM((batch, RET_HEADS, RET_KEY_DIM, RET_VAL_DIM), F32)],
        compiler_params=_params(("arbitrary",)),
        name="retention",
    )(z, z, z, z, cos_t, sin_t, decay, zeta_b, xi_b, gn.reshape(1, -1))
    return out.reshape(m, RET_OUT)


def _merge_body(oa_ref, ob_ref, oc_ref, wa_ref, wb_ref, wc_ref, ga_ref, gb_ref, gc_ref, o_ref):
    def branch(o, w, g):
        return _sigmoid(g[...].astype(F32)) * _dot(o[...], w[...].astype(BF16))

    o_ref[...] = (branch(oa_ref, wa_ref, ga_ref) + branch(ob_ref, wb_ref, gb_ref)
                  + branch(oc_ref, wc_ref, gc_ref)).astype(BF16)


def _merge(z, o_a, o_b, o_c, w_a, w_b, w_c, layer, *, tm=2112, tn=256):
    m = z.shape[0]
    nt = D_MODEL // tn

    def act(width):
        return pl.BlockSpec((tm, width), lambda i, j: (i, 0))

    def wgt(width):
        return pl.BlockSpec((None, width, tn), lambda i, j: (layer, 0, j))

    def gate(br):
        return pl.BlockSpec((tm, tn), lambda i, j: (i, B_GATE // tn + br * nt + j))

    return pl.pallas_call(
        _merge_body,
        grid=(m // tm, nt),
        in_specs=[act(SWA_OUT), act(MLA_OUT), act(RET_OUT), wgt(SWA_OUT), wgt(MLA_OUT), wgt(RET_OUT),
                  gate(0), gate(1), gate(2)],
        out_specs=pl.BlockSpec((tm, tn), lambda i, j: (i, j)),
        out_shape=jax.ShapeDtypeStruct((m, D_MODEL), BF16),
        compiler_params=_params(("parallel", "arbitrary")),
        name="merge",
    )(o_a, o_b, o_c, w_a, w_b, w_c, z, z, z)


def _outproj_body(x_ref, w_ref, r_ref, o_ref):
    o_ref[...] = r_ref[...] + _dot(x_ref[...], w_ref[...].astype(BF16))


def _outproj(x, w, res, layer, *, tm=2112, tn=512):
    m = x.shape[0]
    return pl.pallas_call(
        _outproj_body,
        grid=(m // tm, D_MODEL // tn),
        in_specs=[
            pl.BlockSpec((tm, D_MODEL), lambda i, j: (i, 0)),
            pl.BlockSpec((None, D_MODEL, tn), lambda i, j: (layer, 0, j)),
            pl.BlockSpec((tm, tn), lambda i, j: (i, j)),
        ],
        out_specs=pl.BlockSpec((tm, tn), lambda i, j: (i, j)),
        out_shape=jax.ShapeDtypeStruct((m, D_MODEL), F32),
        compiler_params=_params(("parallel", "arbitrary")),
        name="outproj",
    )(x, w, res)


def _rope_tables(pos, dim, width):
    half = dim // 2
    inv_freq = ROPE_BASE ** (-jnp.arange(half, dtype=F32) / half)
    ang = pos[:, None] * inv_freq[None, :]
    cos = jnp.cos(ang)
    sin = jnp.sin(ang)
    fill = jnp.zeros((pos.shape[0], width - dim), F32)
    return (jnp.concatenate([cos, cos, fill], axis=1), jnp.concatenate([-sin, sin, fill], axis=1))


def kernel(x, meta_tokens, ffn1_norm, ffn1_w_gate, ffn1_w_up, ffn1_w_down, mix_norm, w_in, swa_q_norm, swa_k_norm, swa_sinks, mla_q_a_norm, mla_w_uq, mla_kv_a_norm, mla_w_ukv, mla_qn_norm, mla_qr_norm, mla_kn_norm, mla_kr_norm, ret_gn, w_br_swa, w_br_mla, w_br_ret, w_o, ffn2_norm, ffn2_w_gate, ffn2_w_up, ffn2_w_down):
    batch, seq_in, _ = x.shape
    seq = seq_in + BLOCK
    nb = seq // BLOCK

    meta = jnp.broadcast_to(meta_tokens[None].astype(x.dtype), (batch, N_META, D_MODEL))
    h = jnp.concatenate([jnp.zeros((batch, PAD, D_MODEL), x.dtype), meta, x], axis=1)
    h = h.reshape(batch * seq, D_MODEL)
    pos = (jnp.arange(seq) - PAD).astype(F32)

    w_in_t = jnp.swapaxes(w_in, 1, 2)

    qk_dim = MLA_NOPE_DIM + MLA_ROPE_DIM
    w_uq_pad = jnp.pad(mla_w_uq.reshape(DEPTH, MLA_Q_RANK, MLA_HEADS, qk_dim),
                       ((0, 0), (0, 0), (0, 0), (0, MLA_QK_PAD - qk_dim)))
    w_uq_pad = w_uq_pad.reshape(DEPTH, MLA_Q_RANK, MLA_HEADS * MLA_QK_PAD).astype(BF16)
    rope_fill = jnp.zeros((DEPTH, BLOCK - MLA_ROPE_DIM), F32)
    qr_norm_pad = jnp.concatenate([mla_qr_norm, rope_fill], axis=1)
    kr_norm_pad = jnp.concatenate([mla_kr_norm, rope_fill], axis=1)

    cos_m, sin_m = _rope_tables(pos, MLA_ROPE_DIM, BLOCK)
    cos_r, sin_r = _rope_tables(pos, RET_KEY_DIM, BLOCK)

    log_gamma = jnp.log(1.0 - 2.0 ** (-5.0 - jnp.arange(RET_HEADS, dtype=F32)))
    idx = jnp.arange(BLOCK, dtype=F32)
    diff = idx[:, None] - idx[None, :]
    decay = jnp.where(diff[None] >= 0, jnp.exp(jnp.maximum(diff, 0.0)[None] * log_gamma[:, None, None]), 0.0)
    zeta = jnp.exp((BLOCK - 1.0 - idx)[None, :] * log_gamma[:, None])
    xi = jnp.exp((idx + 1.0)[None, :] * log_gamma[:, None])
    zeta_b = jnp.broadcast_to(zeta[:, :, None], (RET_HEADS, BLOCK, RET_KEY_DIM))
    xi_b = jnp.broadcast_to(xi[:, :, None], (RET_HEADS, BLOCK, RET_KEY_DIM))
    chunk_decay = tuple(float(np.exp(BLOCK * np.log(1.0 - 2.0 ** (-5.0 - hh)))) for hh in range(RET_HEADS))

    for l in range(DEPTH):
        h, hn = _ffn(h, ffn1_norm, ffn1_w_gate, ffn1_w_up, ffn1_w_down, mix_norm, l, True)
        z_a, z_kr = _inproj_a(hn, w_in_t, l)
        z_b = _inproj_b(hn, w_in_t, l)
        sink_col = jnp.repeat(swa_sinks[l].astype(F32).reshape(SWA_KV_HEADS, SWA_GROUP), BLOCK, axis=1)
        o_a = _swa(z_a, swa_q_norm[l], swa_k_norm[l], sink_col[:, None, :], nb, batch)
        q, k, v = _mla_up(z_a, z_kr, mla_q_a_norm[l], mla_kv_a_norm[l], w_uq_pad, mla_w_ukv, mla_qn_norm[l],
                          qr_norm_pad[l:l + 1], mla_kn_norm[l], kr_norm_pad[l:l + 1], cos_m, sin_m, l, seq)
        o_b = _mla_attn(q, k, v, seq, batch)
        o_c = _retention(z_b, cos_r, sin_r, decay, zeta_b, xi_b, ret_gn[l], chunk_decay, nb, batch)
        merged = _merge(z_b, o_a, o_b, o_c, w_br_swa, w_br_mla, w_br_ret, l)
        h = _outproj(merged, w_o, h, l)
        h, _ = _ffn(h, ffn2_norm, ffn2_w_gate, ffn2_w_up, ffn2_w_down, ffn2_norm, l, False)
    return h.reshape(batch, seq, D_MODEL)[:, BLOCK:]
```

```python
import functools

import jax
import jax.numpy as jnp
import numpy as np
from jax import lax
from jax.experimental import pallas as pl
from jax.experimental.pallas import tpu as pltpu

F32 = jnp.float32
BF16 = jnp.bfloat16

D_MODEL = 2048
DEPTH = 4
N_META = 16
BLOCK = 128
PAD = BLOCK - N_META
EPS = 1e-6
NEG_INF = -1e30
ROPE_BASE = 10000.0
HALF_STEP = 0.5
LOG2_E = 1.4426950408889634

SWA_HEADS = 16
SWA_KV_HEADS = 2
SWA_HEAD_DIM = 64
SWA_GROUP = SWA_HEADS // SWA_KV_HEADS
WINDOW = 128

MLA_HEADS = 8
MLA_Q_RANK = 512
MLA_KV_RANK = 512
MLA_NOPE_DIM = 128
MLA_ROPE_DIM = 64
MLA_V_DIM = 128
MLA_QK_PAD = 256

RET_HEADS = 4
RET_KEY_DIM = 128
RET_VAL_DIM = 256

D_FF = 5632
N_BRANCH = 3

SWA_OUT = SWA_HEADS * SWA_HEAD_DIM
SWA_KV_W = SWA_KV_HEADS * SWA_HEAD_DIM
MLA_OUT = MLA_HEADS * MLA_V_DIM
RET_QK_W = RET_HEADS * RET_KEY_DIM
RET_OUT = RET_HEADS * RET_VAL_DIM

A_SWA_Q = 0
A_SWA_K = A_SWA_Q + SWA_OUT
A_SWA_V = A_SWA_K + SWA_KV_W
A_MLA_CQ = A_SWA_V + SWA_KV_W
A_MLA_CKV = A_MLA_CQ + MLA_Q_RANK
IN_KR = A_MLA_CKV + MLA_KV_RANK
IN_B = IN_KR + MLA_ROPE_DIM
B_RET_Q = 0
B_RET_K = B_RET_Q + RET_QK_W
B_RET_V = B_RET_K + RET_QK_W
B_RET_G = B_RET_V + RET_OUT
B_GATE = B_RET_G + RET_OUT
B_WIDTH = B_GATE + N_BRANCH * D_MODEL
IN_WIDTH = IN_B + B_WIDTH

VMEM_LIMIT = 56 * 1024 * 1024
VMEM_LIMIT_FFN = 60 * 1024 * 1024


def _params(semantics, vmem_limit=VMEM_LIMIT):
    return pltpu.CompilerParams(dimension_semantics=semantics, vmem_limit_bytes=vmem_limit)


def _rms(x, g):
    return x * lax.rsqrt(jnp.mean(x * x, axis=-1, keepdims=True) + EPS) * g


def _sigmoid(x):
    return 1.0 / (1.0 + jnp.exp(-x))


def _dot(a, b):
    return jnp.dot(a, b, preferred_element_type=F32)


def _dot_nt(a, b):
    return lax.dot_general(a, b, (((1,), (1,)), ((), ())), preferred_element_type=F32)


FFN_TM = 1056
FFN_TF_HEAD = 256
FFN_TF_TAIL = 512


def _ffn_start(x_ref, g_ref, o_ref, xn_ref):
    x = x_ref[...]
    xn_ref[...] = _rms(x, g_ref[...]).astype(BF16)
    o_ref[...] = x


def _ffn_accumulate(xn_ref, wg, wu, wd, o_ref):
    xn = xn_ref[...]
    gate = _dot(xn, wg)
    up = _dot(xn, wu)
    act = (gate * _sigmoid(gate) * up).astype(BF16)
    o_ref[...] += HALF_STEP * _dot(act, wd)


def _ffn_head_body(x_ref, g_ref, wg_ref, wu_ref, wd_ref, g2_ref, o_ref, wgb_ref, wub_ref, wdb_ref, hn_ref, *,
                   emit_norm):
    f = pl.program_id(0)

    @pl.when(f == 0)
    def _():
        _ffn_start(x_ref, g_ref, o_ref, hn_ref)

    wg = wg_ref[...].astype(BF16)
    wu = wu_ref[...].astype(BF16)
    wd = wd_ref[...].astype(BF16)
    wgb_ref[...] = wg
    wub_ref[...] = wu
    wdb_ref[...] = wd
    _ffn_accumulate(hn_ref, wg, wu, wd, o_ref)

    if emit_norm:
        @pl.when(f == pl.num_programs(0) - 1)
        def _():
            hn_ref[...] = _rms(o_ref[...], g2_ref[...]).astype(BF16)


def _ffn_tail_body(x_ref, g_ref, wg_ref, wu_ref, wd_ref, g2_ref, head_ref, head_n_ref, o_ref, hn_ref, *, emit_norm):
    del head_ref, head_n_ref
    f = pl.program_id(1)

    @pl.when(f == 0)
    def _():
        _ffn_start(x_ref, g_ref, o_ref, hn_ref)

    _ffn_accumulate(hn_ref, wg_ref[...], wu_ref[...], wd_ref[...], o_ref)

    if emit_norm:
        @pl.when(f == pl.num_programs(1) - 1)
        def _():
            hn_ref[...] = _rms(o_ref[...], g2_ref[...]).astype(BF16)


def _ffn(h, gain, w_gate, w_up, w_down, next_gain, layer, emit_norm):
    m = h.shape[0]
    tm, tfh, tft = FFN_TM, FFN_TF_HEAD, FFN_TF_TAIL
    gain = gain.reshape(DEPTH, 1, D_MODEL)
    next_gain = next_gain.reshape(DEPTH, 1, D_MODEL)
    head, wg_b, wu_b, wd_b, head_n = pl.pallas_call(
        functools.partial(_ffn_head_body, emit_norm=emit_norm),
        grid=(D_FF // tfh,),
        in_specs=[
            pl.BlockSpec((tm, D_MODEL), lambda f: (0, 0), pipeline_mode=pl.Buffered(1)),
            pl.BlockSpec((None, 1, D_MODEL), lambda f: (layer, 0, 0)),
            pl.BlockSpec((None, D_MODEL, tfh), lambda f: (layer, 0, f)),
            pl.BlockSpec((None, D_MODEL, tfh), lambda f: (layer, 0, f)),
            pl.BlockSpec((None, tfh, D_MODEL), lambda f: (layer, f, 0)),
            pl.BlockSpec((None, 1, D_MODEL), lambda f: (layer, 0, 0)),
        ],
        out_specs=[
            pl.BlockSpec((tm, D_MODEL), lambda f: (0, 0)),
            pl.BlockSpec((D_MODEL, tfh), lambda f: (0, f)),
            pl.BlockSpec((D_MODEL, tfh), lambda f: (0, f)),
            pl.BlockSpec((tfh, D_MODEL), lambda f: (f, 0)),
            pl.BlockSpec((tm, D_MODEL), lambda f: (0, 0)),
        ],
        out_shape=[
            jax.ShapeDtypeStruct((m, D_MODEL), F32),
            jax.ShapeDtypeStruct((D_MODEL, D_FF), BF16),
            jax.ShapeDtypeStruct((D_MODEL, D_FF), BF16),
            jax.ShapeDtypeStruct((D_FF, D_MODEL), BF16),
            jax.ShapeDtypeStruct((m, D_MODEL), BF16),
        ],
        compiler_params=_params(("arbitrary",), VMEM_LIMIT_FFN),
        name="ffn_head",
    )(h, gain, w_gate, w_up, w_down, next_gain)
    return pl.pallas_call(
        functools.partial(_ffn_tail_body, emit_norm=emit_norm),
        grid=(m // tm - 1, D_FF // tft),
        in_specs=[
            pl.BlockSpec((tm, D_MODEL), lambda i, f: (i + 1, 0)),
            pl.BlockSpec((None, 1, D_MODEL), lambda i, f: (layer, 0, 0)),
            pl.BlockSpec((D_MODEL, tft), lambda i, f: (0, f)),
            pl.BlockSpec((D_MODEL, tft), lambda i, f: (0, f)),
            pl.BlockSpec((tft, D_MODEL), lambda i, f: (f, 0)),
            pl.BlockSpec((None, 1, D_MODEL), lambda i, f: (layer, 0, 0)),
            pl.BlockSpec(memory_space=pl.ANY),
            pl.BlockSpec(memory_space=pl.ANY),
        ],
        out_specs=[
            pl.BlockSpec((tm, D_MODEL), lambda i, f: (i + 1, 0)),
            pl.BlockSpec((tm, D_MODEL), lambda i, f: (i + 1, 0)),
        ],
        out_shape=[
            jax.ShapeDtypeStruct((m, D_MODEL), F32),
            jax.ShapeDtypeStruct((m, D_MODEL), BF16),
        ],
        input_output_aliases={6: 0, 7: 1},
        compiler_params=_params(("parallel", "arbitrary"), VMEM_LIMIT_FFN),
        name="ffn_tail",
    )(h, gain, wg_b, wu_b, wd_b, next_gain, head, head_n)


def _inproj_a_body(xn_ref, w_ref, wkr_ref, za_ref, zk_ref):
    @pl.when(pl.program_id(1) == 0)
    def _():
        kr = _dot_nt(xn_ref[...], wkr_ref[...].astype(BF16))
        zk_ref[...] = jnp.concatenate([kr, jnp.zeros_like(kr)], axis=1).astype(BF16)

    za_ref[...] = _dot_nt(xn_ref[...], w_ref[...].astype(BF16)).astype(BF16)


def _inproj_a(xn, w_in_t, layer, *, tm=2112, tn=768):
    m = xn.shape[0]
    return pl.pallas_call(
        _inproj_a_body,
        grid=(m // tm, IN_KR // tn),
        in_specs=[
            pl.BlockSpec((tm, D_MODEL), lambda i, j: (i, 0)),
            pl.BlockSpec((None, tn, D_MODEL), lambda i, j: (layer, j, 0)),
            pl.BlockSpec((None, MLA_ROPE_DIM, D_MODEL), lambda i, j: (layer, IN_KR // MLA_ROPE_DIM, 0)),
        ],
        out_specs=[
            pl.BlockSpec((tm, tn), lambda i, j: (i, j)),
            pl.BlockSpec((tm, BLOCK), lambda i, j: (i, 0)),
        ],
        out_shape=[
            jax.ShapeDtypeStruct((m, IN_KR), BF16),
            jax.ShapeDtypeStruct((m, BLOCK), BF16),
        ],
        compiler_params=_params(("parallel", "arbitrary")),
        name="inproj_a",
    )(xn, w_in_t, w_in_t)


def _inproj_b_body(xn_ref, w_ref, o_ref):
    o_ref[...] = _dot_nt(xn_ref[...], w_ref[0].astype(BF16)).astype(BF16)


def _inproj_b(xn, w_in_t, layer, *, tm=2112, tn=768):
    m = xn.shape[0]
    return pl.pallas_call(
        _inproj_b_body,
        grid=(m // tm, B_WIDTH // tn),
        in_specs=[
            pl.BlockSpec((tm, D_MODEL), lambda i, j: (i, 0)),
            pl.BlockSpec((pl.Element(1), pl.Element(tn), pl.Element(D_MODEL)),
                         lambda i, j: (layer, pl.multiple_of(IN_B + j * tn, 8), 0)),
        ],
        out_specs=pl.BlockSpec((tm, tn), lambda i, j: (i, j)),
        out_shape=jax.ShapeDtypeStruct((m, B_WIDTH), BF16),
        compiler_params=_params(("parallel", "arbitrary")),
        name="inproj_b",
    )(xn, w_in_t)


def _swa_body(q_ref, kc_ref, kp_ref, km_ref, vc_ref, vp_ref, vm_ref, qn_ref, kn_ref, sink_ref, o_ref):
    n = pl.program_id(0)
    j = lax.broadcasted_iota(jnp.int32, (BLOCK, BLOCK), 0)
    i_loc = lax.broadcasted_iota(jnp.int32, (BLOCK, BLOCK), 1)
    ok_cur = (j <= i_loc) & (n * BLOCK + j >= PAD)
    ok_prev = (j > i_loc) & ((n - 1) * BLOCK + j >= PAD)
    ok_meta = (j >= PAD) & (n * BLOCK + i_loc - j >= WINDOW)

    def mask_rows(s_blk, ok):
        return jnp.concatenate(
            [jnp.where(ok, s_blk[:, h * BLOCK:(h + 1) * BLOCK], NEG_INF) for h in range(SWA_GROUP)], axis=1)

    k_gain = kn_ref[...] * qn_ref[...] * (SWA_HEAD_DIM ** -0.5)
    for b in range(q_ref.shape[0]):
        _swa_block(b, q_ref, kc_ref, kp_ref, km_ref, vc_ref, vp_ref, vm_ref, sink_ref, o_ref, k_gain,
                   (ok_meta, ok_prev, ok_cur), mask_rows)


def _swa_block(b, q_ref, kc_ref, kp_ref, km_ref, vc_ref, vp_ref, vm_ref, sink_ref, o_ref, k_gain, oks, mask_rows):
    ok_meta, ok_prev, ok_cur = oks
    q_t = [q_ref[b, :, a * BLOCK:(a + 1) * BLOCK].astype(F32).T for a in range(SWA_HEADS // 2)]
    v_t = jnp.concatenate([r[b].astype(F32).T for r in (vm_ref, vp_ref, vc_ref)], axis=1)

    out_t = []
    for g in range(SWA_KV_HEADS):
        lo = g * SWA_HEAD_DIM
        hi = lo + SWA_HEAD_DIM
        heads = [g * SWA_GROUP + h for h in range(SWA_GROUP)]
        qt = jnp.concatenate(
            [q_t[hd // 2][(hd % 2) * SWA_HEAD_DIM:(hd % 2 + 1) * SWA_HEAD_DIM, :] for hd in heads], axis=1)
        q_rinv = lax.rsqrt(jnp.sum(qt * qt, axis=0, keepdims=True) * (1.0 / SWA_HEAD_DIM) + EPS)
        k_all = jnp.concatenate(
            [_rms(r[b, :, lo:hi].astype(F32), k_gain) for r in (km_ref, kp_ref, kc_ref)], axis=0).astype(BF16)
        s = _dot(k_all, qt.astype(BF16)) * q_rinv
        s = jnp.concatenate([mask_rows(s[:BLOCK], ok_meta), mask_rows(s[BLOCK:2 * BLOCK], ok_prev),
                             mask_rows(s[2 * BLOCK:], ok_cur)], axis=0)
        sink = sink_ref[g]
        mx = jnp.maximum(jnp.max(s, axis=0, keepdims=True), sink)
        p = jnp.exp(s - mx)
        den = jnp.sum(p, axis=0, keepdims=True) + jnp.exp(sink - mx)
        o_t = _dot(v_t[lo:hi].astype(BF16), p.astype(BF16)) / den
        out_t.extend(o_t[:, h * BLOCK:(h + 1) * BLOCK] for h in range(SWA_GROUP))
    for a in range(SWA_HEADS // 2):
        pair = jnp.concatenate([out_t[2 * a], out_t[2 * a + 1]], axis=0)
        o_ref[b, :, a * BLOCK:(a + 1) * BLOCK] = pair.T.astype(BF16)


def _swa(z, q_norm, k_norm, sink_col, nb, batch):
    m = z.shape[0]
    z = z.reshape(batch, nb * BLOCK, z.shape[1])
    kcol = A_SWA_K // SWA_KV_W
    vcol = A_SWA_V // SWA_KV_W

    def cur(c):
        return pl.BlockSpec((batch, BLOCK, SWA_KV_W), lambda n: (0, n, c))

    def prev(c):
        return pl.BlockSpec((batch, BLOCK, SWA_KV_W), lambda n: (0, jnp.maximum(n - 1, 0), c))

    def meta(c):
        return pl.BlockSpec((batch, BLOCK, SWA_KV_W), lambda n: (0, 0, c))

    out = pl.pallas_call(
        _swa_body,
        grid=(nb,),
        in_specs=[
            pl.BlockSpec((batch, BLOCK, SWA_OUT), lambda n: (0, n, A_SWA_Q // SWA_OUT)),
            cur(kcol), prev(kcol), meta(kcol),
            cur(vcol), prev(vcol), meta(vcol),
            pl.BlockSpec((1, SWA_HEAD_DIM), lambda n: (0, 0)),
            pl.BlockSpec((1, SWA_HEAD_DIM), lambda n: (0, 0)),
            pl.BlockSpec((SWA_KV_HEADS, 1, SWA_GROUP * BLOCK), lambda n: (0, 0, 0)),
        ],
        out_specs=pl.BlockSpec((batch, BLOCK, SWA_OUT), lambda n: (0, n, 0)),
        out_shape=jax.ShapeDtypeStruct((batch, nb * BLOCK, SWA_OUT), BF16),
        compiler_params=_params(("arbitrary",)),
        name="swa",
    )(z, z, z, z, z, z, z, q_norm.reshape(1, -1), k_norm.reshape(1, -1), sink_col)
    return out.reshape(m, SWA_OUT)


def _rope_rows(x, cos, sin, half):
    width = x.shape[-1]
    lane = lax.broadcasted_iota(jnp.int32, x.shape, 1)
    rot = jnp.where(lane < half, pltpu.roll(x, width - half, 1), pltpu.roll(x, half, 1))
    return x * cos + rot * sin


def _rope_key_norm(r, gain):
    return r * lax.rsqrt(jnp.sum(r * r, axis=-1, keepdims=True) * (1.0 / MLA_ROPE_DIM) + EPS) * gain


def _mla_up_body(cq0_ref, cq1_ref, ckv0_ref, ckv1_ref, kr_ref, qa_ref, kva_ref, wq_ref, wkv_ref, qn_ref, qr_ref,
                 kn_ref, krn_ref, cos_ref, sin_ref, q_ref, k_ref, v_ref):
    cos = cos_ref[...]
    sin = sin_ref[...]
    half = MLA_ROPE_DIM // 2
    scale = (MLA_NOPE_DIM + MLA_ROPE_DIM) ** -0.5 * LOG2_E
    c_q = jnp.concatenate([cq0_ref[...], cq1_ref[...]], axis=1).astype(F32)
    c_kv = jnp.concatenate([ckv0_ref[...], ckv1_ref[...]], axis=1).astype(F32)
    yq = _dot(_rms(c_q, qa_ref[...]).astype(BF16), wq_ref[...])
    ykv = _dot(_rms(c_kv, kva_ref[...]).astype(BF16), wkv_ref[...].astype(BF16))
    k_rope = _rope_rows(_rope_key_norm(kr_ref[...].astype(F32), krn_ref[...]), cos, sin, half).astype(BF16)
    for h in range(MLA_HEADS):
        lo = h * MLA_QK_PAD
        nope = _rms(yq[:, lo:lo + MLA_NOPE_DIM], qn_ref[...])
        r = _rope_rows(_rope_key_norm(yq[:, lo + MLA_NOPE_DIM:lo + MLA_QK_PAD], qr_ref[...]), cos, sin, half)
        q_ref[:, lo:lo + MLA_NOPE_DIM] = (nope * scale).astype(BF16)
        q_ref[:, lo + MLA_NOPE_DIM:lo + MLA_QK_PAD] = (r * scale).astype(BF16)
        k_ref[:, lo:lo + MLA_NOPE_DIM] = _rms(ykv[:, lo:lo + MLA_NOPE_DIM], kn_ref[...]).astype(BF16)
        k_ref[:, lo + MLA_NOPE_DIM:lo + MLA_QK_PAD] = k_rope
        v_ref[:, h * MLA_V_DIM:(h + 1) * MLA_V_DIM] = ykv[:, lo + MLA_NOPE_DIM:lo + MLA_QK_PAD].astype(BF16)


def _mla_up(z_a, z_kr, qa_norm, kva_norm, w_uq_pad, w_ukv, qn_norm, qr_norm_pad, kn_norm, kr_norm_pad, cos_t, sin_t,
            layer, seq, *, tm=528):
    m = z_a.shape[0]
    seq_tiles = seq // tm
    kv_w = MLA_NOPE_DIM + MLA_V_DIM
    half_rank = MLA_Q_RANK // 2

    def row(width, col):
        return pl.BlockSpec((tm, width), lambda i: (i, col))

    def vec(width):
        return pl.BlockSpec((1, width), lambda i: (0, 0))

    return pl.pallas_call(
        _mla_up_body,
        grid=(m // tm,),
        in_specs=[
            row(half_rank, A_MLA_CQ // half_rank), row(half_rank, A_MLA_CQ // half_rank + 1),
            row(half_rank, A_MLA_CKV // half_rank), row(half_rank, A_MLA_CKV // half_rank + 1),
            row(BLOCK, 0),
            vec(MLA_Q_RANK), vec(MLA_KV_RANK),
            pl.BlockSpec((None, MLA_Q_RANK, MLA_HEADS * MLA_QK_PAD), lambda i: (layer, 0, 0)),
            pl.BlockSpec((None, MLA_KV_RANK, MLA_HEADS * kv_w), lambda i: (layer, 0, 0)),
            vec(MLA_NOPE_DIM), vec(BLOCK), vec(MLA_NOPE_DIM), vec(BLOCK),
            pl.BlockSpec((tm, BLOCK), lambda i: (i % seq_tiles, 0)),
            pl.BlockSpec((tm, BLOCK), lambda i: (i % seq_tiles, 0)),
        ],
        out_specs=[row(MLA_HEADS * MLA_QK_PAD, 0), row(MLA_HEADS * MLA_QK_PAD, 0), row(MLA_OUT, 0)],
        out_shape=[
            jax.ShapeDtypeStruct((m, MLA_HEADS * MLA_QK_PAD), BF16),
            jax.ShapeDtypeStruct((m, MLA_HEADS * MLA_QK_PAD), BF16),
            jax.ShapeDtypeStruct((m, MLA_OUT), BF16),
        ],
        compiler_params=_params(("parallel",)),
        name="mla_up",
    )(z_a, z_a, z_a, z_a, z_kr, qa_norm.reshape(1, -1), kva_norm.reshape(1, -1), w_uq_pad, w_ukv,
      qn_norm.reshape(1, -1), qr_norm_pad, kn_norm.reshape(1, -1), kr_norm_pad, cos_t, sin_t)


def _mla_attn_body(q_ref, k_ref, v_ref, o_ref, m_ref, acc_ref, *, t, hb):
    i = pl.program_id(2)
    m_ref[...] = jnp.full(m_ref.shape, NEG_INF, F32)
    acc_ref[...] = jnp.zeros(acc_ref.shape, F32)

    def tile(blk, nblk, masked):
        width = nblk * t
        off = blk * t if isinstance(blk, int) else pl.multiple_of(blk * t, t)
        if masked:
            row = i * t + lax.broadcasted_iota(jnp.int32, (t, width), 0)
            col = blk * t + lax.broadcasted_iota(jnp.int32, (t, width), 1)
            ok = (col <= row) & (col >= PAD)
        for h in range(hb):
            q = q_ref[:, h * MLA_QK_PAD:(h + 1) * MLA_QK_PAD]
            kb = k_ref[pl.ds(off, width), h * MLA_QK_PAD:(h + 1) * MLA_QK_PAD]
            vb = v_ref[pl.ds(off, width), h * MLA_V_DIM:(h + 1) * MLA_V_DIM]
            s = _dot_nt(q, kb)
            if masked:
                s = jnp.where(ok, s, NEG_INF)
            m_run = m_ref[h]
            m_new = jnp.maximum(m_run, jnp.max(s, axis=-1, keepdims=True))
            alpha = jnp.exp2(m_run - m_new)
            p = jnp.concatenate(
                [jnp.exp2(s[:, c * BLOCK:(c + 1) * BLOCK] - m_new) for c in range(width // BLOCK)], axis=1)
            v_aug = jnp.concatenate([vb, jnp.ones_like(vb)], axis=1)
            acc_ref[h] = jnp.concatenate([alpha, alpha], axis=1) * acc_ref[h] + _dot(p.astype(BF16), v_aug)
            m_ref[h] = m_new

    @pl.when(i >= 2)
    def _():
        tile(0, 2, True)

    def body(jj, carry):
        tile(2 * jj, 2, False)
        return carry

    lax.fori_loop(1, i // 2, body, 0)

    @pl.when(i % 2 == 1)
    def _():
        tile(i - 1, 2, True)

    @pl.when(i % 2 == 0)
    def _():
        tile(i, 1, True)

    for h in range(hb):
        acc = acc_ref[h]
        o_ref[:, h * MLA_V_DIM:(h + 1) * MLA_V_DIM] = (acc[:, :MLA_V_DIM] / acc[:, MLA_V_DIM:]).astype(BF16)


def _mla_attn(q, k, v, seq, batch, *, t=384, hb=4):
    m = q.shape[0]
    nq = seq // t
    return pl.pallas_call(
        functools.partial(_mla_attn_body, t=t, hb=hb),
        grid=(batch, MLA_HEADS // hb, nq),
        in_specs=[
            pl.BlockSpec((t, hb * MLA_QK_PAD), lambda b, h, i: (b * nq + i, h)),
            pl.BlockSpec((seq, hb * MLA_QK_PAD), lambda b, h, i: (b, h)),
            pl.BlockSpec((seq, hb * MLA_V_DIM), lambda b, h, i: (b, h)),
        ],
        out_specs=pl.BlockSpec((t, hb * MLA_V_DIM), lambda b, h, i: (b * nq + i, h)),
        out_shape=jax.ShapeDtypeStruct((m, MLA_OUT), BF16),
        scratch_shapes=[pltpu.VMEM((hb, t, BLOCK), F32), pltpu.VMEM((hb, t, 2 * MLA_V_DIM), F32)],
        compiler_params=_params(("parallel", "parallel", "arbitrary")),
        name="mla_attn",
    )(q, k, v)


def _ret_body(q_ref, k_ref, v_ref, g_ref, cos_ref, sin_ref, dec_ref, zeta_ref, xi_ref, gn_ref, o_ref,
              state_ref, *, chunk_decay):
    n = pl.program_id(0)

    @pl.when(n == 0)
    def _():
        state_ref[...] = jnp.zeros_like(state_ref)

    cos = cos_ref[...]
    sin = sin_ref[...]
    valid = (n * BLOCK + lax.broadcasted_iota(jnp.int32, (BLOCK, 1), 0) >= PAD).astype(F32)
    for b in range(q_ref.shape[0]):
        for h in range(RET_HEADS):
            ks = slice(h * RET_KEY_DIM, (h + 1) * RET_KEY_DIM)
            vs = slice(h * RET_VAL_DIM, (h + 1) * RET_VAL_DIM)
            qh = _rope_rows(q_ref[b, :, ks].astype(F32), cos, sin, RET_KEY_DIM // 2)
            kh = (_rope_rows(k_ref[b, :, ks].astype(F32), cos, sin, RET_KEY_DIM // 2)
                  * (RET_KEY_DIM ** -0.5) * valid)
            vb = v_ref[b, :, vs]
            s = _dot_nt(qh.astype(BF16), kh.astype(BF16)) * dec_ref[h]
            inner = _dot(s.astype(BF16), vb)
            prev = state_ref[b, h]
            cross = _dot((qh * xi_ref[h]).astype(BF16), prev.astype(BF16))
            kz_t = (kh * zeta_ref[h]).T.astype(BF16)
            state_ref[b, h] = prev * chunk_decay[h] + _dot(kz_t, vb)
            o = inner + cross
            mu = jnp.mean(o, axis=-1, keepdims=True)
            d = o - mu
            y = d * lax.rsqrt(jnp.mean(d * d, axis=-1, keepdims=True) + EPS)
            gate = g_ref[b, :, vs].astype(F32)
            o_ref[b, :, vs] = (y * gn_ref[:, vs] * (gate * _sigmoid(gate))).astype(BF16)


def _retention(z, cos_t, sin_t, decay, zeta_b, xi_b, gn, chunk_decay, nb, batch):
    m = z.shape[0]
    z = z.reshape(batch, nb * BLOCK, z.shape[1])
    tab = pl.BlockSpec((RET_HEADS, BLOCK, BLOCK), lambda n: (0, 0, 0))

    def rows(width, col):
        return pl.BlockSpec((batch, BLOCK, width), lambda n: (0, n, col // width))

    out = pl.pallas_call(
        functools.partial(_ret_body, chunk_decay=chunk_decay),
        grid=(nb,),
        in_specs=[
            rows(RET_QK_W, B_RET_Q), rows(RET_QK_W, B_RET_K), rows(RET_OUT, B_RET_V), rows(RET_OUT, B_RET_G),
            pl.BlockSpec((BLOCK, BLOCK), lambda n: (n, 0)),
            pl.BlockSpec((BLOCK, BLOCK), lambda n: (n, 0)),
            tab, tab, tab,
            pl.BlockSpec((1, RET_OUT), lambda n: (0, 0)),
        ],
        out_specs=pl.BlockSpec((batch, BLOCK, RET_OUT), lambda n: (0, n, 0)),
        out_shape=jax.ShapeDtypeStruct((batch, nb * BLOCK, RET_OUT), BF16),
        scratch_shapes=[pltpu.VMEM((batch, RET_HEADS, RET_KEY_DIM, RET_VAL_DIM), F32)],
        compiler_params=_params(("arbitrary",)),
        name="retention",
    )(z, z, z, z, cos_t, sin_t, decay, zeta_b, xi_b, gn.reshape(1, -1))
    return out.reshape(m, RET_OUT)


def _merge_body(oa_ref, ob_ref, oc_ref, wa_ref, wb_ref, wc_ref, ga_ref, gb_ref, gc_ref, o_ref):
    def branch(o, w, g):
        return _sigmoid(g[...].astype(F32)) * _dot(o[...], w[...].astype(BF16))

    o_ref[...] = (branch(oa_ref, wa_ref, ga_ref) + branch(ob_ref, wb_ref, gb_ref)
                  + branch(oc_ref, wc_ref, gc_ref)).astype(BF16)


def _merge(z, o_a, o_b, o_c, w_a, w_b, w_c, layer, *, tm=1056, tn=512):
    m = z.shape[0]
    nt = D_MODEL // tn

    def act(width):
        return pl.BlockSpec((tm, width), lambda i, j: (i, 0))

    def wgt(width):
        return pl.BlockSpec((None, width, tn), lambda i, j: (layer, 0, j))

    def gate(br):
        return pl.BlockSpec((tm, tn), lambda i, j: (i, B_GATE // tn + br * nt + j))

    return pl.pallas_call(
        _merge_body,
        grid=(m // tm, nt),
        in_specs=[act(SWA_OUT), act(MLA_OUT), act(RET_OUT), wgt(SWA_OUT), wgt(MLA_OUT), wgt(RET_OUT),
                  gate(0), gate(1), gate(2)],
        out_specs=pl.BlockSpec((tm, tn), lambda i, j: (i, j)),
        out_shape=jax.ShapeDtypeStruct((m, D_MODEL), BF16),
        compiler_params=_params(("parallel", "arbitrary")),
        name="merge",
    )(o_a, o_b, o_c, w_a, w_b, w_c, z, z, z)


def _outproj_body(x_ref, w_ref, r_ref, o_ref):
    o_ref[...] = r_ref[...] + _dot(x_ref[...], w_ref[...].astype(BF16))


def _outproj(x, w, res, layer, *, tm=2112, tn=512):
    m = x.shape[0]
    return pl.pallas_call(
        _outproj_body,
        grid=(m // tm, D_MODEL // tn),
        in_specs=[
            pl.BlockSpec((tm, D_MODEL), lambda i, j: (i, 0)),
            pl.BlockSpec((None, D_MODEL, tn), lambda i, j: (layer, 0, j)),
            pl.BlockSpec((tm, tn), lambda i, j: (i, j)),
        ],
        out_specs=pl.BlockSpec((tm, tn), lambda i, j: (i, j)),
        out_shape=jax.ShapeDtypeStruct((m, D_MODEL), F32),
        compiler_params=_params(("parallel", "arbitrary")),
        name="outproj",
    )(x, w, res)


def _rope_tables(pos, dim, width):
    half = dim // 2
    inv_freq = ROPE_BASE ** (-jnp.arange(half, dtype=F32) / half)
    ang = pos[:, None] * inv_freq[None, :]
    cos = jnp.cos(ang)
    sin = jnp.sin(ang)
    fill = jnp.zeros((pos.shape[0], width - dim), F32)
    return (jnp.concatenate([cos, cos, fill], axis=1), jnp.concatenate([-sin, sin, fill], axis=1))


def kernel(x, meta_tokens, ffn1_norm, ffn1_w_gate, ffn1_w_up, ffn1_w_down, mix_norm, w_in, swa_q_norm, swa_k_norm, swa_sinks, mla_q_a_norm, mla_w_uq, mla_kv_a_norm, mla_w_ukv, mla_qn_norm, mla_qr_norm, mla_kn_norm, mla_kr_norm, ret_gn, w_br_swa, w_br_mla, w_br_ret, w_o, ffn2_norm, ffn2_w_gate, ffn2_w_up, ffn2_w_down):
    batch, seq_in, _ = x.shape
    seq = seq_in + BLOCK
    nb = seq // BLOCK

    meta = jnp.broadcast_to(meta_tokens[None].astype(x.dtype), (batch, N_META, D_MODEL))
    h = jnp.concatenate([jnp.zeros((batch, PAD, D_MODEL), x.dtype), meta, x], axis=1)
    h = h.reshape(batch * seq, D_MODEL)
    pos = (jnp.arange(seq) - PAD).astype(F32)

    w_in_t = jnp.swapaxes(w_in, 1, 2)

    qk_dim = MLA_NOPE_DIM + MLA_ROPE_DIM
    w_uq_pad = jnp.pad(mla_w_uq.reshape(DEPTH, MLA_Q_RANK, MLA_HEADS, qk_dim),
                       ((0, 0), (0, 0), (0, 0), (0, MLA_QK_PAD - qk_dim)))
    w_uq_pad = w_uq_pad.reshape(DEPTH, MLA_Q_RANK, MLA_HEADS * MLA_QK_PAD).astype(BF16)
    rope_fill = jnp.zeros((DEPTH, BLOCK - MLA_ROPE_DIM), F32)
    qr_norm_pad = jnp.concatenate([mla_qr_norm, rope_fill], axis=1)
    kr_norm_pad = jnp.concatenate([mla_kr_norm, rope_fill], axis=1)

    cos_m, sin_m = _rope_tables(pos, MLA_ROPE_DIM, BLOCK)
    cos_r, sin_r = _rope_tables(pos, RET_KEY_DIM, BLOCK)

    log_gamma = jnp.log(1.0 - 2.0 ** (-5.0 - jnp.arange(RET_HEADS, dtype=F32)))
    idx = jnp.arange(BLOCK, dtype=F32)
    diff = idx[:, None] - idx[None, :]
    decay = jnp.where(diff[None] >= 0, jnp.exp(jnp.maximum(diff, 0.0)[None] * log_gamma[:, None, None]), 0.0)
    zeta = jnp.exp((BLOCK - 1.0 - idx)[None, :] * log_gamma[:, None])
    xi = jnp.exp((idx + 1.0)[None, :] * log_gamma[:, None])
    zeta_b = jnp.broadcast_to(zeta[:, :, None], (RET_HEADS, BLOCK, RET_KEY_DIM))
    xi_b = jnp.broadcast_to(xi[:, :, None], (RET_HEADS, BLOCK, RET_KEY_DIM))
    chunk_decay = tuple(float(np.exp(BLOCK * np.log(1.0 - 2.0 ** (-5.0 - hh)))) for hh in range(RET_HEADS))

    for l in range(DEPTH):
        h, hn = _ffn(h, ffn1_norm, ffn1_w_gate, ffn1_w_up, ffn1_w_down, mix_norm, l, True)
        z_a, z_kr = _inproj_a(hn, w_in_t, l)
        z_b = _inproj_b(hn, w_in_t, l)
        sink_col = jnp.repeat(swa_sinks[l].astype(F32).reshape(SWA_KV_HEADS, SWA_GROUP), BLOCK, axis=1)
        o_a = _swa(z_a, swa_q_norm[l], swa_k_norm[l], sink_col[:, None, :], nb, batch)
        q, k, v = _mla_up(z_a, z_kr, mla_q_a_norm[l], mla_kv_a_norm[l], w_uq_pad, mla_w_ukv, mla_qn_norm[l],
                          qr_norm_pad[l:l + 1], mla_kn_norm[l], kr_norm_pad[l:l + 1], cos_m, sin_m, l, seq)
        o_b = _mla_attn(q, k, v, seq, batch)
        o_c = _retention(z_b, cos_r, sin_r, decay, zeta_b, xi_b, ret_gn[l], chunk_decay, nb, batch)
        merged = _merge(z_b, o_a, o_b, o_c, w_br_swa, w_br_mla, w_br_ret, l)
        h = _outproj(merged, w_o, h, l)
        h, _ = _ffn(h, ffn2_norm, ffn2_w_gate, ffn2_w_up, ffn2_w_down, ffn2_norm, l, False)
    return h.reshape(batch, seq, D_MODEL)[:, BLOCK:]
```

```python
import functools

import jax
import jax.numpy as jnp
import numpy as np
from jax import lax
from jax.experimental import pallas as pl
from jax.experimental.pallas import tpu as pltpu

F32 = jnp.float32
BF16 = jnp.bfloat16

D_MODEL = 2048
DEPTH = 4
N_META = 16
BLOCK = 128
PAD = BLOCK - N_META
EPS = 1e-6
NEG_INF = -1e30
ROPE_BASE = 10000.0
HALF_STEP = 0.5
LOG2_E = 1.4426950408889634

SWA_HEADS = 16
SWA_KV_HEADS = 2
SWA_HEAD_DIM = 64
SWA_GROUP = SWA_HEADS // SWA_KV_HEADS
WINDOW = 128

MLA_HEADS = 8
MLA_Q_RANK = 512
MLA_KV_RANK = 512
MLA_NOPE_DIM = 128
MLA_ROPE_DIM = 64
MLA_V_DIM = 128
MLA_QK_PAD = 256

RET_HEADS = 4
RET_KEY_DIM = 128
RET_VAL_DIM = 256

D_FF = 5632
N_BRANCH = 3

SWA_OUT = SWA_HEADS * SWA_HEAD_DIM
SWA_KV_W = SWA_KV_HEADS * SWA_HEAD_DIM
MLA_OUT = MLA_HEADS * MLA_V_DIM
RET_QK_W = RET_HEADS * RET_KEY_DIM
RET_OUT = RET_HEADS * RET_VAL_DIM

A_SWA_Q = 0
A_SWA_K = A_SWA_Q + SWA_OUT
A_SWA_V = A_SWA_K + SWA_KV_W
A_MLA_CQ = A_SWA_V + SWA_KV_W
A_MLA_CKV = A_MLA_CQ + MLA_Q_RANK
IN_KR = A_MLA_CKV + MLA_KV_RANK
IN_B = IN_KR + MLA_ROPE_DIM
B_RET_Q = 0
B_RET_K = B_RET_Q + RET_QK_W
B_RET_V = B_RET_K + RET_QK_W
B_RET_G = B_RET_V + RET_OUT
B_GATE = B_RET_G + RET_OUT
B_WIDTH = B_GATE + N_BRANCH * D_MODEL
IN_WIDTH = IN_B + B_WIDTH

VMEM_LIMIT = 56 * 1024 * 1024
VMEM_LIMIT_FFN = 60 * 1024 * 1024


def _params(semantics, vmem_limit=VMEM_LIMIT):
    return pltpu.CompilerParams(dimension_semantics=semantics, vmem_limit_bytes=vmem_limit)


def _rms(x, g):
    return x * lax.rsqrt(jnp.mean(x * x, axis=-1, keepdims=True) + EPS) * g


def _sigmoid(x):
    return 1.0 / (1.0 + jnp.exp(-x))


def _dot(a, b):
    return jnp.dot(a, b, preferred_element_type=F32)


def _dot_nt(a, b):
    return lax.dot_general(a, b, (((1,), (1,)), ((), ())), preferred_element_type=F32)


FFN_TM = 1056
FFN_TM_KEEP = 1024
FFN_TF_HEAD = 256
FFN_TF_TAIL = 512


def _ffn_start(x_ref, g_ref, o_ref, xn_ref):
    x = x_ref[...]
    xn_ref[...] = _rms(x, g_ref[...]).astype(BF16)
    o_ref[...] = x


def _ffn_accumulate(xn_ref, wg, wu, wd, o_ref):
    xn = xn_ref[...]
    gate = _dot(xn, wg)
    up = _dot(xn, wu)
    act = (gate * _sigmoid(gate) * up).astype(BF16)
    o_ref[...] += HALF_STEP * _dot(act, wd)


def _ffn_head_body(x_ref, g_ref, wg_ref, wu_ref, wd_ref, g2_ref, o_ref, wgb_ref, wub_ref, wdb_ref, hn_ref, *,
                   emit_norm):
    f = pl.program_id(0)

    @pl.when(f == 0)
    def _():
        _ffn_start(x_ref, g_ref, o_ref, hn_ref)

    wg = wg_ref[...].astype(BF16)
    wu = wu_ref[...].astype(BF16)
    wd = wd_ref[...].astype(BF16)
    wgb_ref[...] = wg
    wub_ref[...] = wu
    wdb_ref[...] = wd
    _ffn_accumulate(hn_ref, wg, wu, wd, o_ref)

    if emit_norm:
        @pl.when(f == pl.num_programs(0) - 1)
        def _():
            hn_ref[...] = _rms(o_ref[...], g2_ref[...]).astype(BF16)


def _ffn_tail_body(x_ref, g_ref, wg_ref, wu_ref, wd_ref, g2_ref, head_ref, head_n_ref, o_ref, hn_ref, *, emit_norm):
    del head_ref, head_n_ref
    f = pl.program_id(1)

    @pl.when(f == 0)
    def _():
        _ffn_start(x_ref, g_ref, o_ref, hn_ref)

    _ffn_accumulate(hn_ref, wg_ref[...], wu_ref[...], wd_ref[...], o_ref)

    if emit_norm:
        @pl.when(f == pl.num_programs(1) - 1)
        def _():
            hn_ref[...] = _rms(o_ref[...], g2_ref[...]).astype(BF16)


def _ffn(h, gain, w_gate, w_up, w_down, next_gain, layer, emit_norm, keep=None):
    m = h.shape[0]
    tfh, tft = FFN_TF_HEAD, FFN_TF_TAIL
    if keep is None:
        tm, n_tiles = FFN_TM, m // FFN_TM

        def x_row(j):
            return j * tm
    else:
        seq, lead = keep
        tm = FFN_TM_KEEP
        per_seq = (seq - lead) // tm
        n_tiles = (m // seq) * per_seq

        def x_row(j):
            return (j // per_seq) * seq + lead + (j % per_seq) * tm

    m_out = n_tiles * tm
    x_block = (pl.Element(tm), pl.Element(D_MODEL))
    gain = gain.reshape(DEPTH, 1, D_MODEL)
    next_gain = next_gain.reshape(DEPTH, 1, D_MODEL)
    head, wg_b, wu_b, wd_b, head_n = pl.pallas_call(
        functools.partial(_ffn_head_body, emit_norm=emit_norm),
        grid=(D_FF // tfh,),
        in_specs=[
            pl.BlockSpec(x_block, lambda f: (x_row(0), 0), pipeline_mode=pl.Buffered(1)),
            pl.BlockSpec((None, 1, D_MODEL), lambda f: (layer, 0, 0)),
            pl.BlockSpec((None, D_MODEL, tfh), lambda f: (layer, 0, f)),
            pl.BlockSpec((None, D_MODEL, tfh), lambda f: (layer, 0, f)),
            pl.BlockSpec((None, tfh, D_MODEL), lambda f: (layer, f, 0)),
            pl.BlockSpec((None, 1, D_MODEL), lambda f: (layer, 0, 0)),
        ],
        out_specs=[
            pl.BlockSpec((tm, D_MODEL), lambda f: (0, 0)),
            pl.BlockSpec((D_MODEL, tfh), lambda f: (0, f)),
            pl.BlockSpec((D_MODEL, tfh), lambda f: (0, f)),
            pl.BlockSpec((tfh, D_MODEL), lambda f: (f, 0)),
            pl.BlockSpec((tm, D_MODEL), lambda f: (0, 0)),
        ],
        out_shape=[
            jax.ShapeDtypeStruct((m_out, D_MODEL), F32),
            jax.ShapeDtypeStruct((D_MODEL, D_FF), BF16),
            jax.ShapeDtypeStruct((D_MODEL, D_FF), BF16),
            jax.ShapeDtypeStruct((D_FF, D_MODEL), BF16),
            jax.ShapeDtypeStruct((m_out, D_MODEL), BF16),
        ],
        compiler_params=_params(("arbitrary",), VMEM_LIMIT_FFN),
        name="ffn_head",
    )(h, gain, w_gate, w_up, w_down, next_gain)
    return pl.pallas_call(
        functools.partial(_ffn_tail_body, emit_norm=emit_norm),
        grid=(n_tiles - 1, D_FF // tft),
        in_specs=[
            pl.BlockSpec(x_block, lambda i, f: (pl.multiple_of(x_row(i + 1), 8), 0)),
            pl.BlockSpec((None, 1, D_MODEL), lambda i, f: (layer, 0, 0)),
            pl.BlockSpec((D_MODEL, tft), lambda i, f: (0, f)),
            pl.BlockSpec((D_MODEL, tft), lambda i, f: (0, f)),
            pl.BlockSpec((tft, D_MODEL), lambda i, f: (f, 0)),
            pl.BlockSpec((None, 1, D_MODEL), lambda i, f: (layer, 0, 0)),
            pl.BlockSpec(memory_space=pl.ANY),
            pl.BlockSpec(memory_space=pl.ANY),
        ],
        out_specs=[
            pl.BlockSpec((tm, D_MODEL), lambda i, f: (i + 1, 0)),
            pl.BlockSpec((tm, D_MODEL), lambda i, f: (i + 1, 0)),
        ],
        out_shape=[
            jax.ShapeDtypeStruct((m_out, D_MODEL), F32),
            jax.ShapeDtypeStruct((m_out, D_MODEL), BF16),
        ],
        input_output_aliases={6: 0, 7: 1},
        compiler_params=_params(("parallel", "arbitrary"), VMEM_LIMIT_FFN),
        name="ffn_tail",
    )(h, gain, wg_b, wu_b, wd_b, next_gain, head, head_n)


def _inproj_a_body(xn_ref, w_ref, wkr_ref, za_ref, zk_ref):
    @pl.when(pl.program_id(1) == 0)
    def _():
        kr = _dot_nt(xn_ref[...], wkr_ref[...].astype(BF16))
        zk_ref[...] = jnp.concatenate([kr, jnp.zeros_like(kr)], axis=1).astype(BF16)

    za_ref[...] = _dot_nt(xn_ref[...], w_ref[...].astype(BF16)).astype(BF16)


def _inproj_a(xn, w_in_t, layer, *, tm=2112, tn=768):
    m = xn.shape[0]
    return pl.pallas_call(
        _inproj_a_body,
        grid=(m // tm, IN_KR // tn),
        in_specs=[
            pl.BlockSpec((tm, D_MODEL), lambda i, j: (i, 0)),
            pl.BlockSpec((None, tn, D_MODEL), lambda i, j: (layer, j, 0)),
            pl.BlockSpec((None, MLA_ROPE_DIM, D_MODEL), lambda i, j: (layer, IN_KR // MLA_ROPE_DIM, 0)),
        ],
        out_specs=[
            pl.BlockSpec((tm, tn), lambda i, j: (i, j)),
            pl.BlockSpec((tm, BLOCK), lambda i, j: (i, 0)),
        ],
        out_shape=[
            jax.ShapeDtypeStruct((m, IN_KR), BF16),
            jax.ShapeDtypeStruct((m, BLOCK), BF16),
        ],
        compiler_params=_params(("parallel", "arbitrary")),
        name="inproj_a",
    )(xn, w_in_t, w_in_t)


def _inproj_b_body(xn_ref, w_ref, o_ref):
    o_ref[...] = _dot_nt(xn_ref[...], w_ref[0].astype(BF16)).astype(BF16)


def _inproj_b(xn, w_in_t, layer, *, tm=2112, tn=768):
    m = xn.shape[0]
    return pl.pallas_call(
        _inproj_b_body,
        grid=(m // tm, B_WIDTH // tn),
        in_specs=[
            pl.BlockSpec((tm, D_MODEL), lambda i, j: (i, 0)),
            pl.BlockSpec((pl.Element(1), pl.Element(tn), pl.Element(D_MODEL)),
                         lambda i, j: (layer, pl.multiple_of(IN_B + j * tn, 8), 0)),
        ],
        out_specs=pl.BlockSpec((tm, tn), lambda i, j: (i, j)),
        out_shape=jax.ShapeDtypeStruct((m, B_WIDTH), BF16),
        compiler_params=_params(("parallel", "arbitrary")),
        name="inproj_b",
    )(xn, w_in_t)


def _swa_body(q_ref, kc_ref, kp_ref, km_ref, vc_ref, vp_ref, vm_ref, qn_ref, kn_ref, sink_ref, o_ref):
    n = pl.program_id(0)
    j = lax.broadcasted_iota(jnp.int32, (BLOCK, BLOCK), 0)
    i_loc = lax.broadcasted_iota(jnp.int32, (BLOCK, BLOCK), 1)
    ok_cur = (j <= i_loc) & (n * BLOCK + j >= PAD)
    ok_prev = (j > i_loc) & ((n - 1) * BLOCK + j >= PAD)
    ok_meta = (j >= PAD) & (n * BLOCK + i_loc - j >= WINDOW)

    def mask_rows(s_blk, ok):
        return jnp.concatenate(
            [jnp.where(ok, s_blk[:, h * BLOCK:(h + 1) * BLOCK], NEG_INF) for h in range(SWA_GROUP)], axis=1)

    k_gain = kn_ref[...] * qn_ref[...] * (SWA_HEAD_DIM ** -0.5)
    for b in range(q_ref.shape[0]):
        _swa_block(b, q_ref, kc_ref, kp_ref, km_ref, vc_ref, vp_ref, vm_ref, sink_ref, o_ref, k_gain,
                   (ok_meta, ok_prev, ok_cur), mask_rows)


def _swa_block(b, q_ref, kc_ref, kp_ref, km_ref, vc_ref, vp_ref, vm_ref, sink_ref, o_ref, k_gain, oks, mask_rows):
    ok_meta, ok_prev, ok_cur = oks
    q_t = [q_ref[b, :, a * BLOCK:(a + 1) * BLOCK].astype(F32).T for a in range(SWA_HEADS // 2)]
    v_t = jnp.concatenate([r[b].astype(F32).T for r in (vm_ref, vp_ref, vc_ref)], axis=1)

    out_t = []
    for g in range(SWA_KV_HEADS):
        lo = g * SWA_HEAD_DIM
        hi = lo + SWA_HEAD_DIM
        heads = [g * SWA_GROUP + h for h in range(SWA_GROUP)]
        qt = jnp.concatenate(
            [q_t[hd // 2][(hd % 2) * SWA_HEAD_DIM:(hd % 2 + 1) * SWA_HEAD_DIM, :] for hd in heads], axis=1)
        q_rinv = lax.rsqrt(jnp.sum(qt * qt, axis=0, keepdims=True) * (1.0 / SWA_HEAD_DIM) + EPS)
        k_all = jnp.concatenate(
            [_rms(r[b, :, lo:hi].astype(F32), k_gain) for r in (km_ref, kp_ref, kc_ref)], axis=0).astype(BF16)
        s = _dot(k_all, qt.astype(BF16)) * q_rinv
        s = jnp.concatenate([mask_rows(s[:BLOCK], ok_meta), mask_rows(s[BLOCK:2 * BLOCK], ok_prev),
                             mask_rows(s[2 * BLOCK:], ok_cur)], axis=0)
        sink = sink_ref[g]
        mx = jnp.maximum(jnp.max(s, axis=0, keepdims=True), sink)
        p = jnp.exp(s - mx)
        den = jnp.sum(p, axis=0, keepdims=True) + jnp.exp(sink - mx)
        o_t = _dot(v_t[lo:hi].astype(BF16), p.astype(BF16)) / den
        out_t.extend(o_t[:, h * BLOCK:(h + 1) * BLOCK] for h in range(SWA_GROUP))
    for a in range(SWA_HEADS // 2):
        pair = jnp.concatenate([out_t[2 * a], out_t[2 * a + 1]], axis=0)
        o_ref[b, :, a * BLOCK:(a + 1) * BLOCK] = pair.T.astype(BF16)


def _swa(z, q_norm, k_norm, sink_col, nb, batch):
    m = z.shape[0]
    z = z.reshape(batch, nb * BLOCK, z.shape[1])
    kcol = A_SWA_K // SWA_KV_W
    vcol = A_SWA_V // SWA_KV_W

    def cur(c):
        return pl.BlockSpec((batch, BLOCK, SWA_KV_W), lambda n: (0, n, c))

    def prev(c):
        return pl.BlockSpec((batch, BLOCK, SWA_KV_W), lambda n: (0, jnp.maximum(n - 1, 0), c))

    def meta(c):
        return pl.BlockSpec((batch, BLOCK, SWA_KV_W), lambda n: (0, 0, c))

    out = pl.pallas_call(
        _swa_body,
        grid=(nb,),
        in_specs=[
            pl.BlockSpec((batch, BLOCK, SWA_OUT), lambda n: (0, n, A_SWA_Q // SWA_OUT)),
            cur(kcol), prev(kcol), meta(kcol),
            cur(vcol), prev(vcol), meta(vcol),
            pl.BlockSpec((1, SWA_HEAD_DIM), lambda n: (0, 0)),
            pl.BlockSpec((1, SWA_HEAD_DIM), lambda n: (0, 0)),
            pl.BlockSpec((SWA_KV_HEADS, 1, SWA_GROUP * BLOCK), lambda n: (0, 0, 0)),
        ],
        out_specs=pl.BlockSpec((batch, BLOCK, SWA_OUT), lambda n: (0, n, 0)),
        out_shape=jax.ShapeDtypeStruct((batch, nb * BLOCK, SWA_OUT), BF16),
        compiler_params=_params(("arbitrary",)),
        name="swa",
    )(z, z, z, z, z, z, z, q_norm.reshape(1, -1), k_norm.reshape(1, -1), sink_col)
    return out.reshape(m, SWA_OUT)


def _rope_rows(x, cos, sin, half):
    width = x.shape[-1]
    lane = lax.broadcasted_iota(jnp.int32, x.shape, 1)
    rot = jnp.where(lane < half, pltpu.roll(x, width - half, 1), pltpu.roll(x, half, 1))
    return x * cos + rot * sin


def _rope_key_norm(r, gain):
    return r * lax.rsqrt(jnp.sum(r * r, axis=-1, keepdims=True) * (1.0 / MLA_ROPE_DIM) + EPS) * gain


def _mla_up_body(cq0_ref, cq1_ref, ckv0_ref, ckv1_ref, kr_ref, qa_ref, kva_ref, wq_ref, wkv_ref, qn_ref, qr_ref,
                 kn_ref, krn_ref, cos_ref, sin_ref, q_ref, k_ref, v_ref):
    cos = cos_ref[...]
    sin = sin_ref[...]
    half = MLA_ROPE_DIM // 2
    scale = (MLA_NOPE_DIM + MLA_ROPE_DIM) ** -0.5 * LOG2_E
    c_q = jnp.concatenate([cq0_ref[...], cq1_ref[...]], axis=1).astype(F32)
    c_kv = jnp.concatenate([ckv0_ref[...], ckv1_ref[...]], axis=1).astype(F32)
    yq = _dot(_rms(c_q, qa_ref[...]).astype(BF16), wq_ref[...])
    ykv = _dot(_rms(c_kv, kva_ref[...]).astype(BF16), wkv_ref[...].astype(BF16))
    k_rope = _rope_rows(_rope_key_norm(kr_ref[...].astype(F32), krn_ref[...]), cos, sin, half).astype(BF16)
    for h in range(MLA_HEADS):
        lo = h * MLA_QK_PAD
        nope = _rms(yq[:, lo:lo + MLA_NOPE_DIM], qn_ref[...])
        r = _rope_rows(_rope_key_norm(yq[:, lo + MLA_NOPE_DIM:lo + MLA_QK_PAD], qr_ref[...]), cos, sin, half)
        q_ref[:, lo:lo + MLA_NOPE_DIM] = (nope * scale).astype(BF16)
        q_ref[:, lo + MLA_NOPE_DIM:lo + MLA_QK_PAD] = (r * scale).astype(BF16)
        k_ref[:, lo:lo + MLA_NOPE_DIM] = _rms(ykv[:, lo:lo + MLA_NOPE_DIM], kn_ref[...]).astype(BF16)
        k_ref[:, lo + MLA_NOPE_DIM:lo + MLA_QK_PAD] = k_rope
        v_ref[:, h * MLA_V_DIM:(h + 1) * MLA_V_DIM] = ykv[:, lo + MLA_NOPE_DIM:lo + MLA_QK_PAD].astype(BF16)


def _mla_up(z_a, z_kr, qa_norm, kva_norm, w_uq_pad, w_ukv, qn_norm, qr_norm_pad, kn_norm, kr_norm_pad, cos_t, sin_t,
            layer, seq, *, tm=528):
    m = z_a.shape[0]
    seq_tiles = seq // tm
    kv_w = MLA_NOPE_DIM + MLA_V_DIM
    half_rank = MLA_Q_RANK // 2

    def row(width, col):
        return pl.BlockSpec((tm, width), lambda i: (i, col))

    def vec(width):
        return pl.BlockSpec((1, width), lambda i: (0, 0))

    return pl.pallas_call(
        _mla_up_body,
        grid=(m // tm,),
        in_specs=[
            row(half_rank, A_MLA_CQ // half_rank), row(half_rank, A_MLA_CQ // half_rank + 1),
            row(half_rank, A_MLA_CKV // half_rank), row(half_rank, A_MLA_CKV // half_rank + 1),
            row(BLOCK, 0),
            vec(MLA_Q_RANK), vec(MLA_KV_RANK),
            pl.BlockSpec((None, MLA_Q_RANK, MLA_HEADS * MLA_QK_PAD), lambda i: (layer, 0, 0)),
            pl.BlockSpec((None, MLA_KV_RANK, MLA_HEADS * kv_w), lambda i: (layer, 0, 0)),
            vec(MLA_NOPE_DIM), vec(BLOCK), vec(MLA_NOPE_DIM), vec(BLOCK),
            pl.BlockSpec((tm, BLOCK), lambda i: (i % seq_tiles, 0)),
            pl.BlockSpec((tm, BLOCK), lambda i: (i % seq_tiles, 0)),
        ],
        out_specs=[row(MLA_HEADS * MLA_QK_PAD, 0), row(MLA_HEADS * MLA_QK_PAD, 0), row(MLA_OUT, 0)],
        out_shape=[
            jax.ShapeDtypeStruct((m, MLA_HEADS * MLA_QK_PAD), BF16),
            jax.ShapeDtypeStruct((m, MLA_HEADS * MLA_QK_PAD), BF16),
            jax.ShapeDtypeStruct((m, MLA_OUT), BF16),
        ],
        compiler_params=_params(("parallel",)),
        name="mla_up",
    )(z_a, z_a, z_a, z_a, z_kr, qa_norm.reshape(1, -1), kva_norm.reshape(1, -1), w_uq_pad, w_ukv,
      qn_norm.reshape(1, -1), qr_norm_pad, kn_norm.reshape(1, -1), kr_norm_pad, cos_t, sin_t)


def _mla_attn_body(q_ref, k_ref, v_ref, o_ref, m_ref, acc_ref, *, t, hb):
    i = pl.program_id(2)
    m_ref[...] = jnp.full(m_ref.shape, NEG_INF, F32)
    acc_ref[...] = jnp.zeros(acc_ref.shape, F32)

    def tile(blk, nblk, masked):
        width = nblk * t
        off = blk * t if isinstance(blk, int) else pl.multiple_of(blk * t, t)
        if masked:
            row = i * t + lax.broadcasted_iota(jnp.int32, (t, width), 0)
            col = blk * t + lax.broadcasted_iota(jnp.int32, (t, width), 1)
            ok = (col <= row) & (col >= PAD)
        for h in range(hb):
            q = q_ref[:, h * MLA_QK_PAD:(h + 1) * MLA_QK_PAD]
            kb = k_ref[pl.ds(off, width), h * MLA_QK_PAD:(h + 1) * MLA_QK_PAD]
            vb = v_ref[pl.ds(off, width), h * MLA_V_DIM:(h + 1) * MLA_V_DIM]
            s = _dot_nt(q, kb)
            if masked:
                s = jnp.where(ok, s, NEG_INF)
            m_run = m_ref[h]
            m_new = jnp.maximum(m_run, jnp.max(s, axis=-1, keepdims=True))
            alpha = jnp.exp2(m_run - m_new)
            p = jnp.concatenate(
                [jnp.exp2(s[:, c * BLOCK:(c + 1) * BLOCK] - m_new) for c in range(width // BLOCK)], axis=1)
            v_aug = jnp.concatenate([vb, jnp.ones_like(vb)], axis=1)
            acc_ref[h] = jnp.concatenate([alpha, alpha], axis=1) * acc_ref[h] + _dot(p.astype(BF16), v_aug)
            m_ref[h] = m_new

    @pl.when(i >= 2)
    def _():
        tile(0, 2, True)

    def body(jj, carry):
        tile(2 * jj, 2, False)
        return carry

    lax.fori_loop(1, i // 2, body, 0)

    @pl.when(i % 2 == 1)
    def _():
        tile(i - 1, 2, True)

    @pl.when(i % 2 == 0)
    def _():
        tile(i, 1, True)

    for h in range(hb):
        acc = acc_ref[h]
        o_ref[:, h * MLA_V_DIM:(h + 1) * MLA_V_DIM] = (acc[:, :MLA_V_DIM] / acc[:, MLA_V_DIM:]).astype(BF16)


def _mla_attn(q, k, v, seq, batch, *, t=384, hb=MLA_HEADS):
    m = q.shape[0]
    nq = seq // t
    return pl.pallas_call(
        functools.partial(_mla_attn_body, t=t, hb=hb),
        grid=(batch, MLA_HEADS // hb, nq),
        in_specs=[
            pl.BlockSpec((t, hb * MLA_QK_PAD), lambda b, h, i: (b * nq + i, h)),
            pl.BlockSpec((seq, hb * MLA_QK_PAD), lambda b, h, i: (b, h), pipeline_mode=pl.Buffered(1)),
            pl.BlockSpec((seq, hb * MLA_V_DIM), lambda b, h, i: (b, h), pipeline_mode=pl.Buffered(1)),
        ],
        out_specs=pl.BlockSpec((t, hb * MLA_V_DIM), lambda b, h, i: (b * nq + i, h)),
        out_shape=jax.ShapeDtypeStruct((m, MLA_OUT), BF16),
        scratch_shapes=[pltpu.VMEM((hb, t, BLOCK), F32), pltpu.VMEM((hb, t, 2 * MLA_V_DIM), F32)],
        compiler_params=_params(("parallel", "parallel", "arbitrary")),
        name="mla_attn",
    )(q, k, v)


def _ret_body(q_ref, k_ref, v_ref, g_ref, cos_ref, sin_ref, dec_ref, zeta_ref, xi_ref, gn_ref, o_ref,
              state_ref, *, chunk_decay):
    n = pl.program_id(0)

    @pl.when(n == 0)
    def _():
        state_ref[...] = jnp.zeros_like(state_ref)

    cos = cos_ref[...]
    sin = sin_ref[...]
    valid = (n * BLOCK + lax.broadcasted_iota(jnp.int32, (BLOCK, 1), 0) >= PAD).astype(F32)
    for b in range(q_ref.shape[0]):
        for h in range(RET_HEADS):
            ks = slice(h * RET_KEY_DIM, (h + 1) * RET_KEY_DIM)
            vs = slice(h * RET_VAL_DIM, (h + 1) * RET_VAL_DIM)
            qh = _rope_rows(q_ref[b, :, ks].astype(F32), cos, sin, RET_KEY_DIM // 2)
            kh = (_rope_rows(k_ref[b, :, ks].astype(F32), cos, sin, RET_KEY_DIM // 2)
                  * (RET_KEY_DIM ** -0.5) * valid)
            vb = v_ref[b, :, vs]
            s = _dot_nt(qh.astype(BF16), kh.astype(BF16)) * dec_ref[h]
            inner = _dot(s.astype(BF16), vb)
            prev = state_ref[b, h]
            cross = _dot((qh * xi_ref[h]).astype(BF16), prev.astype(BF16))
            kz_t = (kh * zeta_ref[h]).T.astype(BF16)
            state_ref[b, h] = prev * chunk_decay[h] + _dot(kz_t, vb)
            o = inner + cross
            mu = jnp.mean(o, axis=-1, keepdims=True)
            d = o - mu
            y = d * lax.rsqrt(jnp.mean(d * d, axis=-1, keepdims=True) + EPS)
            gate = g_ref[b, :, vs].astype(F32)
            o_ref[b, :, vs] = (y * gn_ref[:, vs] * (gate * _sigmoid(gate))).astype(BF16)


def _retention(z, cos_t, sin_t, decay, zeta_b, xi_b, gn, chunk_decay, nb, batch):
    m = z.shape[0]
    z = z.reshape(batch, nb * BLOCK, z.shape[1])
    tab = pl.BlockSpec((RET_HEADS, BLOCK, BLOCK), lambda n: (0, 0, 0))

    def rows(width, col):
        return pl.BlockSpec((batch, BLOCK, width), lambda n: (0, n, col // width))

    out = pl.pallas_call(
        functools.partial(_ret_body, chunk_decay=chunk_decay),
        grid=(nb,),
        in_specs=[
            rows(RET_QK_W, B_RET_Q), rows(RET_QK_W, B_RET_K), rows(RET_OUT, B_RET_V), rows(RET_OUT, B_RET_G),
            pl.BlockSpec((BLOCK, BLOCK), lambda n: (n, 0)),
            pl.BlockSpec((BLOCK, BLOCK), lambda n: (n, 0)),
            tab, tab, tab,
            pl.BlockSpec((1, RET_OUT), lambda n: (0, 0)),
        ],
        out_specs=pl.BlockSpec((batch, BLOCK, RET_OUT), lambda n: (0, n, 0)),
        out_shape=jax.ShapeDtypeStruct((batch, nb * BLOCK, RET_OUT), BF16),
        scratch_shapes=[pltpu.VMEM((batch, RET_HEADS, RET_KEY_DIM, RET_VAL_DIM), F32)],
        compiler_params=_params(("arbitrary",)),
        name="retention",
    )(z, z, z, z, cos_t, sin_t, decay, zeta_b, xi_b, gn.reshape(1, -1))
    return out.reshape(m, RET_OUT)


def _merge_body(oa_ref, ob_ref, oc_ref, wa_ref, wb_ref, wc_ref, ga_ref, gb_ref, gc_ref, o_ref):
    def branch(o, w, g):
        return _sigmoid(g[...].astype(F32)) * _dot(o[...], w[...].astype(BF16))

    o_ref[...] = (branch(oa_ref, wa_ref, ga_ref) + branch(ob_ref, wb_ref, gb_ref)
                  + branch(oc_ref, wc_ref, gc_ref)).astype(BF16)


def _merge(z, o_a, o_b, o_c, w_a, w_b, w_c, layer, *, tm=1056, tn=512):
    m = z.shape[0]
    nt = D_MODEL // tn

    def act(width):
        return pl.BlockSpec((tm, width), lambda i, j: (i, 0))

    def wgt(width):
        return pl.BlockSpec((None, width, tn), lambda i, j: (layer, 0, j))

    def gate(br):
        return pl.BlockSpec((tm, tn), lambda i, j: (i, B_GATE // tn + br * nt + j))

    return pl.pallas_call(
        _merge_body,
        grid=(m // tm, nt),
        in_specs=[act(SWA_OUT), act(MLA_OUT), act(RET_OUT), wgt(SWA_OUT), wgt(MLA_OUT), wgt(RET_OUT),
                  gate(0), gate(1), gate(2)],
        out_specs=pl.BlockSpec((tm, tn), lambda i, j: (i, j)),
        out_shape=jax.ShapeDtypeStruct((m, D_MODEL), BF16),
        compiler_params=_params(("parallel", "arbitrary")),
        name="merge",
    )(o_a, o_b, o_c, w_a, w_b, w_c, z, z, z)


def _outproj_body(x_ref, w_ref, r_ref, o_ref):
    o_ref[...] = r_ref[...] + _dot(x_ref[...], w_ref[...].astype(BF16))


def _outproj(x, w, res, layer, *, tm=2112, tn=512):
    m = x.shape[0]
    return pl.pallas_call(
        _outproj_body,
        grid=(m // tm, D_MODEL // tn),
        in_specs=[
            pl.BlockSpec((tm, D_MODEL), lambda i, j: (i, 0)),
            pl.BlockSpec((None, D_MODEL, tn), lambda i, j: (layer, 0, j)),
            pl.BlockSpec((tm, tn), lambda i, j: (i, j)),
        ],
        out_specs=pl.BlockSpec((tm, tn), lambda i, j: (i, j)),
        out_shape=jax.ShapeDtypeStruct((m, D_MODEL), F32),
        compiler_params=_params(("parallel", "arbitrary")),
        name="outproj",
    )(x, w, res)


def _rope_tables(pos, dim, width):
    half = dim // 2
    inv_freq = ROPE_BASE ** (-jnp.arange(half, dtype=F32) / half)
    ang = pos[:, None] * inv_freq[None, :]
    cos = jnp.cos(ang)
    sin = jnp.sin(ang)
    fill = jnp.zeros((pos.shape[0], width - dim), F32)
    return (jnp.concatenate([cos, cos, fill], axis=1), jnp.concatenate([-sin, sin, fill], axis=1))


def kernel(x, meta_tokens, ffn1_norm, ffn1_w_gate, ffn1_w_up, ffn1_w_down, mix_norm, w_in, swa_q_norm, swa_k_norm, swa_sinks, mla_q_a_norm, mla_w_uq, mla_kv_a_norm, mla_w_ukv, mla_qn_norm, mla_qr_norm, mla_kn_norm, mla_kr_norm, ret_gn, w_br_swa, w_br_mla, w_br_ret, w_o, ffn2_norm, ffn2_w_gate, ffn2_w_up, ffn2_w_down):
    batch, seq_in, _ = x.shape
    seq = seq_in + BLOCK
    nb = seq // BLOCK

    meta = jnp.broadcast_to(meta_tokens[None].astype(x.dtype), (batch, N_META, D_MODEL))
    h = jnp.concatenate([jnp.zeros((batch, PAD, D_MODEL), x.dtype), meta, x], axis=1)
    h = h.reshape(batch * seq, D_MODEL)
    pos = (jnp.arange(seq) - PAD).astype(F32)

    w_in_t = jnp.swapaxes(w_in, 1, 2)

    qk_dim = MLA_NOPE_DIM + MLA_ROPE_DIM
    w_uq_pad = jnp.pad(mla_w_uq.reshape(DEPTH, MLA_Q_RANK, MLA_HEADS, qk_dim),
                       ((0, 0), (0, 0), (0, 0), (0, MLA_QK_PAD - qk_dim)))
    w_uq_pad = w_uq_pad.reshape(DEPTH, MLA_Q_RANK, MLA_HEADS * MLA_QK_PAD).astype(BF16)
    rope_fill = jnp.zeros((DEPTH, BLOCK - MLA_ROPE_DIM), F32)
    qr_norm_pad = jnp.concatenate([mla_qr_norm, rope_fill], axis=1)
    kr_norm_pad = jnp.concatenate([mla_kr_norm, rope_fill], axis=1)

    cos_m, sin_m = _rope_tables(pos, MLA_ROPE_DIM, BLOCK)
    cos_r, sin_r = _rope_tables(pos, RET_KEY_DIM, BLOCK)

    log_gamma = jnp.log(1.0 - 2.0 ** (-5.0 - jnp.arange(RET_HEADS, dtype=F32)))
    idx = jnp.arange(BLOCK, dtype=F32)
    diff = idx[:, None] - idx[None, :]
    decay = jnp.where(diff[None] >= 0, jnp.exp(jnp.maximum(diff, 0.0)[None] * log_gamma[:, None, None]), 0.0)
    zeta = jnp.exp((BLOCK - 1.0 - idx)[None, :] * log_gamma[:, None])
    xi = jnp.exp((idx + 1.0)[None, :] * log_gamma[:, None])
    zeta_b = jnp.broadcast_to(zeta[:, :, None], (RET_HEADS, BLOCK, RET_KEY_DIM))
    xi_b = jnp.broadcast_to(xi[:, :, None], (RET_HEADS, BLOCK, RET_KEY_DIM))
    chunk_decay = tuple(float(np.exp(BLOCK * np.log(1.0 - 2.0 ** (-5.0 - hh)))) for hh in range(RET_HEADS))

    for l in range(DEPTH):
        h, hn = _ffn(h, ffn1_norm, ffn1_w_gate, ffn1_w_up, ffn1_w_down, mix_norm, l, True)
        z_a, z_kr = _inproj_a(hn, w_in_t, l)
        z_b = _inproj_b(hn, w_in_t, l)
        sink_col = jnp.repeat(swa_sinks[l].astype(F32).reshape(SWA_KV_HEADS, SWA_GROUP), BLOCK, axis=1)
        o_a = _swa(z_a, swa_q_norm[l], swa_k_norm[l], sink_col[:, None, :], nb, batch)
        q, k, v = _mla_up(z_a, z_kr, mla_q_a_norm[l], mla_kv_a_norm[l], w_uq_pad, mla_w_ukv, mla_qn_norm[l],
                          qr_norm_pad[l:l + 1], mla_kn_norm[l], kr_norm_pad[l:l + 1], cos_m, sin_m, l, seq)
        o_b = _mla_attn(q, k, v, seq, batch)
        o_c = _retention(z_b, cos_r, sin_r, decay, zeta_b, xi_b, ret_gn[l], chunk_decay, nb, batch)
        merged = _merge(z_b, o_a, o_b, o_c, w_br_swa, w_br_mla, w_br_ret, l)
        h = _outproj(merged, w_o, h, l)
        last = l == DEPTH - 1
        h, _ = _ffn(h, ffn2_norm, ffn2_w_gate, ffn2_w_up, ffn2_w_down, ffn2_norm, l, False,
                    keep=(seq, BLOCK) if last else None)
    return h.reshape(batch, seq_in, D_MODEL)
```

```python
import functools

import jax
import jax.numpy as jnp
import numpy as np
from jax import lax
from jax.experimental import pallas as pl
from jax.experimental.pallas import tpu as pltpu

F32 = jnp.float32
BF16 = jnp.bfloat16

D_MODEL = 2048
DEPTH = 4
N_META = 16
BLOCK = 128
PAD = BLOCK - N_META
EPS = 1e-6
NEG_INF = -1e30
ROPE_BASE = 10000.0
HALF_STEP = 0.5
LOG2_E = 1.4426950408889634

SWA_HEADS = 16
SWA_KV_HEADS = 2
SWA_HEAD_DIM = 64
SWA_GROUP = SWA_HEADS // SWA_KV_HEADS
WINDOW = 128

MLA_HEADS = 8
MLA_Q_RANK = 512
MLA_KV_RANK = 512
MLA_NOPE_DIM = 128
MLA_ROPE_DIM = 64
MLA_V_DIM = 128
MLA_QK_PAD = 256

RET_HEADS = 4
RET_KEY_DIM = 128
RET_VAL_DIM = 256

D_FF = 5632
N_BRANCH = 3

SWA_OUT = SWA_HEADS * SWA_HEAD_DIM
SWA_KV_W = SWA_KV_HEADS * SWA_HEAD_DIM
MLA_OUT = MLA_HEADS * MLA_V_DIM
RET_QK_W = RET_HEADS * RET_KEY_DIM
RET_OUT = RET_HEADS * RET_VAL_DIM

A_SWA_Q = 0
A_SWA_K = A_SWA_Q + SWA_OUT
A_SWA_V = A_SWA_K + SWA_KV_W
A_MLA_CQ = A_SWA_V + SWA_KV_W
A_MLA_CKV = A_MLA_CQ + MLA_Q_RANK
IN_KR = A_MLA_CKV + MLA_KV_RANK
IN_B = IN_KR + MLA_ROPE_DIM
B_RET_Q = 0
B_RET_K = B_RET_Q + RET_QK_W
B_RET_V = B_RET_K + RET_QK_W
B_RET_G = B_RET_V + RET_OUT
B_GATE = B_RET_G + RET_OUT
B_WIDTH = B_GATE + N_BRANCH * D_MODEL
IN_WIDTH = IN_B + B_WIDTH

VMEM_LIMIT = 56 * 1024 * 1024
VMEM_LIMIT_FFN = 60 * 1024 * 1024

ROW_TILE = 1056
ROW_TILE_WIDE = 2112
MLA_UP_ROW_TILE = 528
INPROJ_COL_TILE = 768
MERGE_COL_TILE = 512
OUTPROJ_COL_TILE = 512
MLA_BLOCK = 384


def _params(semantics, vmem_limit=VMEM_LIMIT):
    return pltpu.CompilerParams(dimension_semantics=semantics, vmem_limit_bytes=vmem_limit)


def _rms(x, g):
    return x * lax.rsqrt(jnp.mean(x * x, axis=-1, keepdims=True) + EPS) * g


def _sigmoid(x):
    return 1.0 / (1.0 + jnp.exp(-x))


def _dot(a, b):
    return jnp.dot(a, b, preferred_element_type=F32)


def _dot_nt(a, b):
    return lax.dot_general(a, b, (((1,), (1,)), ((), ())), preferred_element_type=F32)


FFN_TM = ROW_TILE
FFN_TM_KEEP = 1024
FFN_TF_HEAD = 256
FFN_TF_TAIL = 512


def _ffn_start(x_ref, g_ref, o_ref, xn_ref):
    x = x_ref[...]
    xn_ref[...] = _rms(x, g_ref[...]).astype(BF16)
    o_ref[...] = x


def _ffn_accumulate(xn_ref, wg, wu, wd, o_ref):
    xn = xn_ref[...]
    gate = _dot(xn, wg)
    up = _dot(xn, wu)
    act = (gate * _sigmoid(gate) * up).astype(BF16)
    o_ref[...] += HALF_STEP * _dot(act, wd)


def _ffn_head_body(x_ref, g_ref, wg_ref, wu_ref, wd_ref, g2_ref, o_ref, wgb_ref, wub_ref, wdb_ref, hn_ref, *,
                   emit_norm):
    f = pl.program_id(0)

    @pl.when(f == 0)
    def _():
        _ffn_start(x_ref, g_ref, o_ref, hn_ref)

    wg = wg_ref[...].astype(BF16)
    wu = wu_ref[...].astype(BF16)
    wd = wd_ref[...].astype(BF16)
    wgb_ref[...] = wg
    wub_ref[...] = wu
    wdb_ref[...] = wd
    _ffn_accumulate(hn_ref, wg, wu, wd, o_ref)

    if emit_norm:
        @pl.when(f == pl.num_programs(0) - 1)
        def _():
            hn_ref[...] = _rms(o_ref[...], g2_ref[...]).astype(BF16)


def _ffn_tail_body(x_ref, g_ref, wg_ref, wu_ref, wd_ref, g2_ref, head_ref, head_n_ref, o_ref, hn_ref, *, emit_norm):
    del head_ref, head_n_ref
    f = pl.program_id(1)

    @pl.when(f == 0)
    def _():
        _ffn_start(x_ref, g_ref, o_ref, hn_ref)

    _ffn_accumulate(hn_ref, wg_ref[...], wu_ref[...], wd_ref[...], o_ref)

    if emit_norm:
        @pl.when(f == pl.num_programs(1) - 1)
        def _():
            hn_ref[...] = _rms(o_ref[...], g2_ref[...]).astype(BF16)


def _ffn(h, gain, w_gate, w_up, w_down, next_gain, layer, emit_norm, keep=None):
    m = h.shape[0]
    tfh, tft = FFN_TF_HEAD, FFN_TF_TAIL
    if keep is None:
        tm, n_tiles = FFN_TM, m // FFN_TM

        def x_row(j):
            return j * tm
    else:
        seq, lead = keep
        tm = FFN_TM_KEEP
        per_seq = (seq - lead) // tm
        n_tiles = (m // seq) * per_seq

        def x_row(j):
            return (j // per_seq) * seq + lead + (j % per_seq) * tm

    m_out = n_tiles * tm
    x_block = (pl.Element(tm), pl.Element(D_MODEL))
    gain = gain.reshape(DEPTH, 1, D_MODEL)
    next_gain = next_gain.reshape(DEPTH, 1, D_MODEL)
    head, wg_b, wu_b, wd_b, head_n = pl.pallas_call(
        functools.partial(_ffn_head_body, emit_norm=emit_norm),
        grid=(D_FF // tfh,),
        in_specs=[
            pl.BlockSpec(x_block, lambda f: (x_row(0), 0), pipeline_mode=pl.Buffered(1)),
            pl.BlockSpec((None, 1, D_MODEL), lambda f: (layer, 0, 0)),
            pl.BlockSpec((None, D_MODEL, tfh), lambda f: (layer, 0, f)),
            pl.BlockSpec((None, D_MODEL, tfh), lambda f: (layer, 0, f)),
            pl.BlockSpec((None, tfh, D_MODEL), lambda f: (layer, f, 0)),
            pl.BlockSpec((None, 1, D_MODEL), lambda f: (layer, 0, 0)),
        ],
        out_specs=[
            pl.BlockSpec((tm, D_MODEL), lambda f: (0, 0)),
            pl.BlockSpec((D_MODEL, tfh), lambda f: (0, f)),
            pl.BlockSpec((D_MODEL, tfh), lambda f: (0, f)),
            pl.BlockSpec((tfh, D_MODEL), lambda f: (f, 0)),
            pl.BlockSpec((tm, D_MODEL), lambda f: (0, 0)),
        ],
        out_shape=[
            jax.ShapeDtypeStruct((m_out, D_MODEL), F32),
            jax.ShapeDtypeStruct((D_MODEL, D_FF), BF16),
            jax.ShapeDtypeStruct((D_MODEL, D_FF), BF16),
            jax.ShapeDtypeStruct((D_FF, D_MODEL), BF16),
            jax.ShapeDtypeStruct((m_out, D_MODEL), BF16),
        ],
        compiler_params=_params(("arbitrary",), VMEM_LIMIT_FFN),
        name="ffn_head",
    )(h, gain, w_gate, w_up, w_down, next_gain)
    return pl.pallas_call(
        functools.partial(_ffn_tail_body, emit_norm=emit_norm),
        grid=(n_tiles - 1, D_FF // tft),
        in_specs=[
            pl.BlockSpec(x_block, lambda i, f: (pl.multiple_of(x_row(i + 1), 8), 0)),
            pl.BlockSpec((None, 1, D_MODEL), lambda i, f: (layer, 0, 0)),
            pl.BlockSpec((D_MODEL, tft), lambda i, f: (0, f)),
            pl.BlockSpec((D_MODEL, tft), lambda i, f: (0, f)),
            pl.BlockSpec((tft, D_MODEL), lambda i, f: (f, 0)),
            pl.BlockSpec((None, 1, D_MODEL), lambda i, f: (layer, 0, 0)),
            pl.BlockSpec(memory_space=pl.ANY),
            pl.BlockSpec(memory_space=pl.ANY),
        ],
        out_specs=[
            pl.BlockSpec((tm, D_MODEL), lambda i, f: (i + 1, 0)),
            pl.BlockSpec((tm, D_MODEL), lambda i, f: (i + 1, 0)),
        ],
        out_shape=[
            jax.ShapeDtypeStruct((m_out, D_MODEL), F32),
            jax.ShapeDtypeStruct((m_out, D_MODEL), BF16),
        ],
        input_output_aliases={6: 0, 7: 1},
        compiler_params=_params(("parallel", "arbitrary"), VMEM_LIMIT_FFN),
        name="ffn_tail",
    )(h, gain, wg_b, wu_b, wd_b, next_gain, head, head_n)


def _inproj_a_body(xn_ref, w_ref, wkr_ref, za_ref, zk_ref):
    @pl.when(pl.program_id(1) == 0)
    def _():
        kr = _dot_nt(xn_ref[...], wkr_ref[...].astype(BF16))
        zk_ref[...] = jnp.concatenate([kr, jnp.zeros_like(kr)], axis=1).astype(BF16)

    za_ref[...] = _dot_nt(xn_ref[...], w_ref[...].astype(BF16)).astype(BF16)


def _inproj_a(xn, w_in_t, layer, *, tm=ROW_TILE_WIDE, tn=INPROJ_COL_TILE):
    m = xn.shape[0]
    return pl.pallas_call(
        _inproj_a_body,
        grid=(m // tm, IN_KR // tn),
        in_specs=[
            pl.BlockSpec((tm, D_MODEL), lambda i, j: (i, 0)),
            pl.BlockSpec((None, tn, D_MODEL), lambda i, j: (layer, j, 0)),
            pl.BlockSpec((None, MLA_ROPE_DIM, D_MODEL), lambda i, j: (layer, IN_KR // MLA_ROPE_DIM, 0)),
        ],
        out_specs=[
            pl.BlockSpec((tm, tn), lambda i, j: (i, j)),
            pl.BlockSpec((tm, BLOCK), lambda i, j: (i, 0)),
        ],
        out_shape=[
            jax.ShapeDtypeStruct((m, IN_KR), BF16),
            jax.ShapeDtypeStruct((m, BLOCK), BF16),
        ],
        compiler_params=_params(("parallel", "arbitrary")),
        name="inproj_a",
    )(xn, w_in_t, w_in_t)


def _inproj_b_body(xn_ref, w_ref, o_ref):
    o_ref[...] = _dot_nt(xn_ref[...], w_ref[0].astype(BF16)).astype(BF16)


def _inproj_b(xn, w_in_t, layer, *, tm=ROW_TILE_WIDE, tn=INPROJ_COL_TILE):
    m = xn.shape[0]
    return pl.pallas_call(
        _inproj_b_body,
        grid=(m // tm, B_WIDTH // tn),
        in_specs=[
            pl.BlockSpec((tm, D_MODEL), lambda i, j: (i, 0)),
            pl.BlockSpec((pl.Element(1), pl.Element(tn), pl.Element(D_MODEL)),
                         lambda i, j: (layer, pl.multiple_of(IN_B + j * tn, 8), 0)),
        ],
        out_specs=pl.BlockSpec((tm, tn), lambda i, j: (i, j)),
        out_shape=jax.ShapeDtypeStruct((m, B_WIDTH), BF16),
        compiler_params=_params(("parallel", "arbitrary")),
        name="inproj_b",
    )(xn, w_in_t)


def _swa_body(q_ref, kc_ref, kp_ref, km_ref, vc_ref, vp_ref, vm_ref, qn_ref, kn_ref, sink_ref, o_ref):
    n = pl.program_id(0)
    j = lax.broadcasted_iota(jnp.int32, (BLOCK, BLOCK), 0)
    i_loc = lax.broadcasted_iota(jnp.int32, (BLOCK, BLOCK), 1)
    ok_cur = (j <= i_loc) & (n * BLOCK + j >= PAD)
    ok_prev = (j > i_loc) & ((n - 1) * BLOCK + j >= PAD)
    ok_meta = (j >= PAD) & (n * BLOCK + i_loc - j >= WINDOW)

    def mask_rows(s_blk, ok):
        return jnp.concatenate(
            [jnp.where(ok, s_blk[:, h * BLOCK:(h + 1) * BLOCK], NEG_INF) for h in range(SWA_GROUP)], axis=1)

    k_gain = kn_ref[...] * qn_ref[...] * (SWA_HEAD_DIM ** -0.5 * LOG2_E)
    for b in range(q_ref.shape[0]):
        _swa_block(b, q_ref, kc_ref, kp_ref, km_ref, vc_ref, vp_ref, vm_ref, sink_ref, o_ref, k_gain,
                   (ok_meta, ok_prev, ok_cur), mask_rows)


def _swa_block(b, q_ref, kc_ref, kp_ref, km_ref, vc_ref, vp_ref, vm_ref, sink_ref, o_ref, k_gain, oks, mask_rows):
    ok_meta, ok_prev, ok_cur = oks
    q_t = [q_ref[b, :, a * BLOCK:(a + 1) * BLOCK].astype(F32).T for a in range(SWA_HEADS // 2)]
    v_t = jnp.concatenate([r[b].astype(F32).T for r in (vm_ref, vp_ref, vc_ref)], axis=1)

    out_t = []
    for g in range(SWA_KV_HEADS):
        lo = g * SWA_HEAD_DIM
        hi = lo + SWA_HEAD_DIM
        heads = [g * SWA_GROUP + h for h in range(SWA_GROUP)]
        qt = jnp.concatenate(
            [q_t[hd // 2][(hd % 2) * SWA_HEAD_DIM:(hd % 2 + 1) * SWA_HEAD_DIM, :] for hd in heads], axis=1)
        q_rinv = lax.rsqrt(jnp.sum(qt * qt, axis=0, keepdims=True) * (1.0 / SWA_HEAD_DIM) + EPS)
        k_all = jnp.concatenate(
            [_rms(r[b, :, lo:hi].astype(F32), k_gain) for r in (km_ref, kp_ref, kc_ref)], axis=0).astype(BF16)
        s = _dot(k_all, qt.astype(BF16)) * q_rinv
        s = jnp.concatenate([mask_rows(s[:BLOCK], ok_meta), mask_rows(s[BLOCK:2 * BLOCK], ok_prev),
                             mask_rows(s[2 * BLOCK:], ok_cur)], axis=0)
        sink = sink_ref[g] * LOG2_E
        mx = jnp.maximum(jnp.max(s, axis=0, keepdims=True), sink)
        p = jnp.exp2(s - mx)
        den = jnp.sum(p, axis=0, keepdims=True) + jnp.exp2(sink - mx)
        o_t = _dot(v_t[lo:hi].astype(BF16), p.astype(BF16)) / den
        out_t.extend(o_t[:, h * BLOCK:(h + 1) * BLOCK] for h in range(SWA_GROUP))
    for a in range(SWA_HEADS // 2):
        pair = jnp.concatenate([out_t[2 * a], out_t[2 * a + 1]], axis=0)
        o_ref[b, :, a * BLOCK:(a + 1) * BLOCK] = pair.T.astype(BF16)


def _swa(z, q_norm, k_norm, sink_col, nb, batch):
    m = z.shape[0]
    z = z.reshape(batch, nb * BLOCK, z.shape[1])
    kcol = A_SWA_K // SWA_KV_W
    vcol = A_SWA_V // SWA_KV_W

    def cur(c):
        return pl.BlockSpec((batch, BLOCK, SWA_KV_W), lambda n: (0, n, c))

    def prev(c):
        return pl.BlockSpec((batch, BLOCK, SWA_KV_W), lambda n: (0, jnp.maximum(n - 1, 0), c))

    def meta(c):
        return pl.BlockSpec((batch, BLOCK, SWA_KV_W), lambda n: (0, 0, c))

    out = pl.pallas_call(
        _swa_body,
        grid=(nb,),
        in_specs=[
            pl.BlockSpec((batch, BLOCK, SWA_OUT), lambda n: (0, n, A_SWA_Q // SWA_OUT)),
            cur(kcol), prev(kcol), meta(kcol),
            cur(vcol), prev(vcol), meta(vcol),
            pl.BlockSpec((1, SWA_HEAD_DIM), lambda n: (0, 0)),
            pl.BlockSpec((1, SWA_HEAD_DIM), lambda n: (0, 0)),
            pl.BlockSpec((SWA_KV_HEADS, 1, SWA_GROUP * BLOCK), lambda n: (0, 0, 0)),
        ],
        out_specs=pl.BlockSpec((batch, BLOCK, SWA_OUT), lambda n: (0, n, 0)),
        out_shape=jax.ShapeDtypeStruct((batch, nb * BLOCK, SWA_OUT), BF16),
        compiler_params=_params(("arbitrary",)),
        name="swa",
    )(z, z, z, z, z, z, z, q_norm.reshape(1, -1), k_norm.reshape(1, -1), sink_col)
    return out.reshape(m, SWA_OUT)


def _rope_rows(x, cos, sin, half):
    width = x.shape[-1]
    lane = lax.broadcasted_iota(jnp.int32, x.shape, 1)
    rot = jnp.where(lane < half, pltpu.roll(x, width - half, 1), pltpu.roll(x, half, 1))
    return x * cos + rot * sin


def _rope_key_norm(r, gain):
    return r * lax.rsqrt(jnp.sum(r * r, axis=-1, keepdims=True) * (1.0 / MLA_ROPE_DIM) + EPS) * gain


def _mla_up_body(cq0_ref, cq1_ref, ckv0_ref, ckv1_ref, kr_ref, qa_ref, kva_ref, wq_ref, wkv_ref, qn_ref, qr_ref,
                 kn_ref, krn_ref, cos_ref, sin_ref, q_ref, k_ref, v_ref):
    cos = cos_ref[...]
    sin = sin_ref[...]
    half = MLA_ROPE_DIM // 2
    scale = (MLA_NOPE_DIM + MLA_ROPE_DIM) ** -0.5 * LOG2_E
    qn_gain = qn_ref[...] * scale
    qr_gain = qr_ref[...] * scale
    c_q = jnp.concatenate([cq0_ref[...], cq1_ref[...]], axis=1).astype(F32)
    c_kv = jnp.concatenate([ckv0_ref[...], ckv1_ref[...]], axis=1).astype(F32)
    yq = _dot(_rms(c_q, qa_ref[...]).astype(BF16), wq_ref[...])
    ykv = _dot(_rms(c_kv, kva_ref[...]).astype(BF16), wkv_ref[...].astype(BF16))
    k_rope = _rope_rows(_rope_key_norm(kr_ref[...].astype(F32), krn_ref[...]), cos, sin, half).astype(BF16)
    for h in range(MLA_HEADS):
        lo = h * MLA_QK_PAD
        nope = _rms(yq[:, lo:lo + MLA_NOPE_DIM], qn_gain)
        r = _rope_rows(_rope_key_norm(yq[:, lo + MLA_NOPE_DIM:lo + MLA_QK_PAD], qr_gain), cos, sin, half)
        q_ref[:, lo:lo + MLA_NOPE_DIM] = nope.astype(BF16)
        q_ref[:, lo + MLA_NOPE_DIM:lo + MLA_QK_PAD] = r.astype(BF16)
        k_ref[:, lo:lo + MLA_NOPE_DIM] = _rms(ykv[:, lo:lo + MLA_NOPE_DIM], kn_ref[...]).astype(BF16)
        k_ref[:, lo + MLA_NOPE_DIM:lo + MLA_QK_PAD] = k_rope
        v_ref[:, h * MLA_V_DIM:(h + 1) * MLA_V_DIM] = ykv[:, lo + MLA_NOPE_DIM:lo + MLA_QK_PAD].astype(BF16)


def _mla_up(z_a, z_kr, qa_norm, kva_norm, w_uq_pad, w_ukv, qn_norm, qr_norm_pad, kn_norm, kr_norm_pad, cos_t, sin_t,
            layer, seq, *, tm=MLA_UP_ROW_TILE):
    m = z_a.shape[0]
    seq_tiles = seq // tm
    kv_w = MLA_NOPE_DIM + MLA_V_DIM
    half_rank = MLA_Q_RANK // 2

    def row(width, col):
        return pl.BlockSpec((tm, width), lambda i: (i, col))

    def vec(width):
        return pl.BlockSpec((1, width), lambda i: (0, 0))

    return pl.pallas_call(
        _mla_up_body,
        grid=(m // tm,),
        in_specs=[
            row(half_rank, A_MLA_CQ // half_rank), row(half_rank, A_MLA_CQ // half_rank + 1),
            row(half_rank, A_MLA_CKV // half_rank), row(half_rank, A_MLA_CKV // half_rank + 1),
            row(BLOCK, 0),
            vec(MLA_Q_RANK), vec(MLA_KV_RANK),
            pl.BlockSpec((None, MLA_Q_RANK, MLA_HEADS * MLA_QK_PAD), lambda i: (layer, 0, 0)),
            pl.BlockSpec((None, MLA_KV_RANK, MLA_HEADS * kv_w), lambda i: (layer, 0, 0)),
            vec(MLA_NOPE_DIM), vec(BLOCK), vec(MLA_NOPE_DIM), vec(BLOCK),
            pl.BlockSpec((tm, BLOCK), lambda i: (i % seq_tiles, 0)),
            pl.BlockSpec((tm, BLOCK), lambda i: (i % seq_tiles, 0)),
        ],
        out_specs=[row(MLA_HEADS * MLA_QK_PAD, 0), row(MLA_HEADS * MLA_QK_PAD, 0), row(MLA_OUT, 0)],
        out_shape=[
            jax.ShapeDtypeStruct((m, MLA_HEADS * MLA_QK_PAD), BF16),
            jax.ShapeDtypeStruct((m, MLA_HEADS * MLA_QK_PAD), BF16),
            jax.ShapeDtypeStruct((m, MLA_OUT), BF16),
        ],
        compiler_params=_params(("parallel",)),
        name="mla_up",
    )(z_a, z_a, z_a, z_a, z_kr, qa_norm.reshape(1, -1), kva_norm.reshape(1, -1), w_uq_pad, w_ukv,
      qn_norm.reshape(1, -1), qr_norm_pad, kn_norm.reshape(1, -1), kr_norm_pad, cos_t, sin_t)


def _mla_attn_body(q_ref, k_ref, v_ref, o_ref, m_ref, acc_ref, *, t, hb):
    i = pl.program_id(2)
    m_ref[...] = jnp.full(m_ref.shape, NEG_INF, F32)
    acc_ref[...] = jnp.zeros(acc_ref.shape, F32)

    def tile(blk, nblk, masked):
        width = nblk * t
        off = blk * t if isinstance(blk, int) else pl.multiple_of(blk * t, t)
        if masked:
            row = i * t + lax.broadcasted_iota(jnp.int32, (t, width), 0)
            col = blk * t + lax.broadcasted_iota(jnp.int32, (t, width), 1)
            ok = (col <= row) & (col >= PAD)
        for h in range(hb):
            q = q_ref[:, h * MLA_QK_PAD:(h + 1) * MLA_QK_PAD]
            kb = k_ref[pl.ds(off, width), h * MLA_QK_PAD:(h + 1) * MLA_QK_PAD]
            vb = v_ref[pl.ds(off, width), h * MLA_V_DIM:(h + 1) * MLA_V_DIM]
            s = _dot_nt(q, kb)
            if masked:
                s = jnp.where(ok, s, NEG_INF)
            m_run = m_ref[h]
            m_new = jnp.maximum(m_run, jnp.max(s, axis=-1, keepdims=True))
            alpha = jnp.exp2(m_run - m_new)
            p = jnp.concatenate(
                [jnp.exp2(s[:, c * BLOCK:(c + 1) * BLOCK] - m_new) for c in range(width // BLOCK)], axis=1)
            v_aug = jnp.concatenate([vb, jnp.ones_like(vb)], axis=1)
            acc_ref[h] = jnp.concatenate([alpha, alpha], axis=1) * acc_ref[h] + _dot(p.astype(BF16), v_aug)
            m_ref[h] = m_new

    @pl.when(i >= 2)
    def _():
        tile(0, 2, True)

    def body(jj, carry):
        tile(2 * jj, 2, False)
        return carry

    lax.fori_loop(1, i // 2, body, 0)

    @pl.when(i % 2 == 1)
    def _():
        tile(i - 1, 2, True)

    @pl.when(i % 2 == 0)
    def _():
        tile(i, 1, True)

    for h in range(hb):
        acc = acc_ref[h]
        o_ref[:, h * MLA_V_DIM:(h + 1) * MLA_V_DIM] = (acc[:, :MLA_V_DIM] / acc[:, MLA_V_DIM:]).astype(BF16)


def _mla_attn(q, k, v, seq, batch, *, t=MLA_BLOCK, hb=MLA_HEADS):
    m = q.shape[0]
    nq = seq // t
    return pl.pallas_call(
        functools.partial(_mla_attn_body, t=t, hb=hb),
        grid=(batch, MLA_HEADS // hb, nq),
        in_specs=[
            pl.BlockSpec((t, hb * MLA_QK_PAD), lambda b, h, i: (b * nq + i, h)),
            pl.BlockSpec((seq, hb * MLA_QK_PAD), lambda b, h, i: (b, h), pipeline_mode=pl.Buffered(1)),
            pl.BlockSpec((seq, hb * MLA_V_DIM), lambda b, h, i: (b, h), pipeline_mode=pl.Buffered(1)),
        ],
        out_specs=pl.BlockSpec((t, hb * MLA_V_DIM), lambda b, h, i: (b * nq + i, h)),
        out_shape=jax.ShapeDtypeStruct((m, MLA_OUT), BF16),
        scratch_shapes=[pltpu.VMEM((hb, t, BLOCK), F32), pltpu.VMEM((hb, t, 2 * MLA_V_DIM), F32)],
        compiler_params=_params(("parallel", "parallel", "arbitrary")),
        name="mla_attn",
    )(q, k, v)


def _ret_body(q_ref, k_ref, v_ref, g_ref, cos_ref, sin_ref, dec_ref, zeta_ref, xi_ref, gn_ref, o_ref,
              state_ref, *, chunk_decay):
    n = pl.program_id(0)

    @pl.when(n == 0)
    def _():
        state_ref[...] = jnp.zeros_like(state_ref)

    cos = cos_ref[...]
    sin = sin_ref[...]
    valid = (n * BLOCK + lax.broadcasted_iota(jnp.int32, (BLOCK, 1), 0) >= PAD).astype(F32)
    for b in range(q_ref.shape[0]):
        for h in range(RET_HEADS):
            ks = slice(h * RET_KEY_DIM, (h + 1) * RET_KEY_DIM)
            vs = slice(h * RET_VAL_DIM, (h + 1) * RET_VAL_DIM)
            qh = _rope_rows(q_ref[b, :, ks].astype(F32), cos, sin, RET_KEY_DIM // 2)
            kh = (_rope_rows(k_ref[b, :, ks].astype(F32), cos, sin, RET_KEY_DIM // 2)
                  * (RET_KEY_DIM ** -0.5) * valid)
            vb = v_ref[b, :, vs]
            s = _dot_nt(qh.astype(BF16), kh.astype(BF16)) * dec_ref[h]
            inner = _dot(s.astype(BF16), vb)
            prev = state_ref[b, h]
            cross = _dot((qh * xi_ref[h]).astype(BF16), prev.astype(BF16))
            kz_t = (kh * zeta_ref[h]).T.astype(BF16)
            state_ref[b, h] = prev * chunk_decay[h] + _dot(kz_t, vb)
            o = inner + cross
            mu = jnp.mean(o, axis=-1, keepdims=True)
            d = o - mu
            y = d * lax.rsqrt(jnp.mean(d * d, axis=-1, keepdims=True) + EPS)
            gate = g_ref[b, :, vs].astype(F32)
            o_ref[b, :, vs] = (y * gn_ref[:, vs] * (gate * _sigmoid(gate))).astype(BF16)


def _retention(z, cos_t, sin_t, decay, zeta_b, xi_b, gn, chunk_decay, nb, batch):
    m = z.shape[0]
    z = z.reshape(batch, nb * BLOCK, z.shape[1])
    tab = pl.BlockSpec((RET_HEADS, BLOCK, BLOCK), lambda n: (0, 0, 0))

    def rows(width, col):
        return pl.BlockSpec((batch, BLOCK, width), lambda n: (0, n, col // width))

    out = pl.pallas_call(
        functools.partial(_ret_body, chunk_decay=chunk_decay),
        grid=(nb,),
        in_specs=[
            rows(RET_QK_W, B_RET_Q), rows(RET_QK_W, B_RET_K), rows(RET_OUT, B_RET_V), rows(RET_OUT, B_RET_G),
            pl.BlockSpec((BLOCK, BLOCK), lambda n: (n, 0)),
            pl.BlockSpec((BLOCK, BLOCK), lambda n: (n, 0)),
            tab, tab, tab,
            pl.BlockSpec((1, RET_OUT), lambda n: (0, 0)),
        ],
        out_specs=pl.BlockSpec((batch, BLOCK, RET_OUT), lambda n: (0, n, 0)),
        out_shape=jax.ShapeDtypeStruct((batch, nb * BLOCK, RET_OUT), BF16),
        scratch_shapes=[pltpu.VMEM((batch, RET_HEADS, RET_KEY_DIM, RET_VAL_DIM), F32)],
        compiler_params=_params(("arbitrary",)),
        name="retention",
    )(z, z, z, z, cos_t, sin_t, decay, zeta_b, xi_b, gn.reshape(1, -1))
    return out.reshape(m, RET_OUT)


def _merge_body(oa_ref, ob_ref, oc_ref, wa_ref, wb_ref, wc_ref, ga_ref, gb_ref, gc_ref, o_ref):
    def branch(o, w, g):
        return _sigmoid(g[...].astype(F32)) * _dot(o[...], w[...].astype(BF16))

    o_ref[...] = (branch(oa_ref, wa_ref, ga_ref) + branch(ob_ref, wb_ref, gb_ref)
                  + branch(oc_ref, wc_ref, gc_ref)).astype(BF16)


def _merge(z, o_a, o_b, o_c, w_a, w_b, w_c, layer, *, tm=ROW_TILE, tn=MERGE_COL_TILE):
    m = z.shape[0]
    nt = D_MODEL // tn

    def act(width):
        return pl.BlockSpec((tm, width), lambda i, j: (i, 0))

    def wgt(width):
        return pl.BlockSpec((None, width, tn), lambda i, j: (layer, 0, j))

    def gate(br):
        return pl.BlockSpec((tm, tn), lambda i, j: (i, B_GATE // tn + br * nt + j))

    return pl.pallas_call(
        _merge_body,
        grid=(m // tm, nt),
        in_specs=[act(SWA_OUT), act(MLA_OUT), act(RET_OUT), wgt(SWA_OUT), wgt(MLA_OUT), wgt(RET_OUT),
                  gate(0), gate(1), gate(2)],
        out_specs=pl.BlockSpec((tm, tn), lambda i, j: (i, j)),
        out_shape=jax.ShapeDtypeStruct((m, D_MODEL), BF16),
        compiler_params=_params(("parallel", "arbitrary")),
        name="merge",
    )(o_a, o_b, o_c, w_a, w_b, w_c, z, z, z)


def _outproj_body(x_ref, w_ref, r_ref, o_ref):
    o_ref[...] = r_ref[...] + _dot(x_ref[...], w_ref[...].astype(BF16))


def _outproj(x, w, res, layer, *, tm=ROW_TILE_WIDE, tn=OUTPROJ_COL_TILE):
    m = x.shape[0]
    return pl.pallas_call(
        _outproj_body,
        grid=(m // tm, D_MODEL // tn),
        in_specs=[
            pl.BlockSpec((tm, D_MODEL), lambda i, j: (i, 0)),
            pl.BlockSpec((None, D_MODEL, tn), lambda i, j: (layer, 0, j)),
            pl.BlockSpec((tm, tn), lambda i, j: (i, j)),
        ],
        out_specs=pl.BlockSpec((tm, tn), lambda i, j: (i, j)),
        out_shape=jax.ShapeDtypeStruct((m, D_MODEL), F32),
        compiler_params=_params(("parallel", "arbitrary")),
        name="outproj",
    )(x, w, res)


def _rope_tables(pos, dim, width):
    half = dim // 2
    inv_freq = ROPE_BASE ** (-jnp.arange(half, dtype=F32) / half)
    ang = pos[:, None] * inv_freq[None, :]
    cos = jnp.cos(ang)
    sin = jnp.sin(ang)
    fill = jnp.zeros((pos.shape[0], width - dim), F32)
    return (jnp.concatenate([cos, cos, fill], axis=1), jnp.concatenate([-sin, sin, fill], axis=1))


def kernel(x, meta_tokens, ffn1_norm, ffn1_w_gate, ffn1_w_up, ffn1_w_down, mix_norm, w_in, swa_q_norm, swa_k_norm, swa_sinks, mla_q_a_norm, mla_w_uq, mla_kv_a_norm, mla_w_ukv, mla_qn_norm, mla_qr_norm, mla_kn_norm, mla_kr_norm, ret_gn, w_br_swa, w_br_mla, w_br_ret, w_o, ffn2_norm, ffn2_w_gate, ffn2_w_up, ffn2_w_down):
    batch, seq_in, _ = x.shape
    seq = seq_in + BLOCK
    nb = seq // BLOCK
    assert x.shape[2] == D_MODEL and seq % BLOCK == 0 and seq % MLA_BLOCK == 0 and seq % MLA_UP_ROW_TILE == 0
    assert (batch * seq) % ROW_TILE_WIDE == 0 and seq_in % FFN_TM_KEEP == 0

    meta = jnp.broadcast_to(meta_tokens[None].astype(x.dtype), (batch, N_META, D_MODEL))
    h = jnp.concatenate([jnp.zeros((batch, PAD, D_MODEL), x.dtype), meta, x], axis=1)
    h = h.reshape(batch * seq, D_MODEL)
    pos = (jnp.arange(seq) - PAD).astype(F32)

    w_in_t = jnp.swapaxes(w_in, 1, 2)

    qk_dim = MLA_NOPE_DIM + MLA_ROPE_DIM
    w_uq_pad = jnp.pad(mla_w_uq.reshape(DEPTH, MLA_Q_RANK, MLA_HEADS, qk_dim),
                       ((0, 0), (0, 0), (0, 0), (0, MLA_QK_PAD - qk_dim)))
    w_uq_pad = w_uq_pad.reshape(DEPTH, MLA_Q_RANK, MLA_HEADS * MLA_QK_PAD).astype(BF16)
    rope_fill = jnp.zeros((DEPTH, BLOCK - MLA_ROPE_DIM), F32)
    qr_norm_pad = jnp.concatenate([mla_qr_norm, rope_fill], axis=1)
    kr_norm_pad = jnp.concatenate([mla_kr_norm, rope_fill], axis=1)

    cos_m, sin_m = _rope_tables(pos, MLA_ROPE_DIM, BLOCK)
    cos_r, sin_r = _rope_tables(pos, RET_KEY_DIM, BLOCK)

    log_gamma = jnp.log(1.0 - 2.0 ** (-5.0 - jnp.arange(RET_HEADS, dtype=F32)))
    idx = jnp.arange(BLOCK, dtype=F32)
    diff = idx[:, None] - idx[None, :]
    decay = jnp.where(diff[None] >= 0, jnp.exp(jnp.maximum(diff, 0.0)[None] * log_gamma[:, None, None]), 0.0)
    zeta = jnp.exp((BLOCK - 1.0 - idx)[None, :] * log_gamma[:, None])
    xi = jnp.exp((idx + 1.0)[None, :] * log_gamma[:, None])
    zeta_b = jnp.broadcast_to(zeta[:, :, None], (RET_HEADS, BLOCK, RET_KEY_DIM))
    xi_b = jnp.broadcast_to(xi[:, :, None], (RET_HEADS, BLOCK, RET_KEY_DIM))
    chunk_decay = tuple(float(np.exp(BLOCK * np.log(1.0 - 2.0 ** (-5.0 - hh)))) for hh in range(RET_HEADS))

    for l in range(DEPTH):
        h, hn = _ffn(h, ffn1_norm, ffn1_w_gate, ffn1_w_up, ffn1_w_down, mix_norm, l, True)
        z_a, z_kr = _inproj_a(hn, w_in_t, l)
        z_b = _inproj_b(hn, w_in_t, l)
        sink_col = jnp.repeat(swa_sinks[l].astype(F32).reshape(SWA_KV_HEADS, SWA_GROUP), BLOCK, axis=1)
        o_a = _swa(z_a, swa_q_norm[l], swa_k_norm[l], sink_col[:, None, :], nb, batch)
        q, k, v = _mla_up(z_a, z_kr, mla_q_a_norm[l], mla_kv_a_norm[l], w_uq_pad, mla_w_ukv, mla_qn_norm[l],
                          qr_norm_pad[l:l + 1], mla_kn_norm[l], kr_norm_pad[l:l + 1], cos_m, sin_m, l, seq)
        o_b = _mla_attn(q, k, v, seq, batch)
        o_c = _retention(z_b, cos_r, sin_r, decay, zeta_b, xi_b, ret_gn[l], chunk_decay, nb, batch)
        merged = _merge(z_b, o_a, o_b, o_c, w_br_swa, w_br_mla, w_br_ret, l)
        h = _outproj(merged, w_o, h, l)
        last = l == DEPTH - 1
        h, _ = _ffn(h, ffn2_norm, ffn2_w_gate, ffn2_w_up, ffn2_w_down, ffn2_norm, l, False,
                    keep=(seq, BLOCK) if last else None)
    return h.reshape(batch, seq_in, D_MODEL)
```

```python
import functools

import jax
import jax.numpy as jnp
import numpy as np
from jax import lax
from jax.experimental import pallas as pl
from jax.experimental.pallas import tpu as pltpu

F32 = jnp.float32
BF16 = jnp.bfloat16

D_MODEL = 2048
DEPTH = 4
N_META = 16
BLOCK = 128
PAD = BLOCK - N_META
EPS = 1e-6
NEG_INF = -1e30
ROPE_BASE = 10000.0
HALF_STEP = 0.5
LOG2_E = 1.4426950408889634

SWA_HEADS = 16
SWA_KV_HEADS = 2
SWA_HEAD_DIM = 64
SWA_GROUP = SWA_HEADS // SWA_KV_HEADS
WINDOW = 128

MLA_HEADS = 8
MLA_Q_RANK = 512
MLA_KV_RANK = 512
MLA_NOPE_DIM = 128
MLA_ROPE_DIM = 64
MLA_V_DIM = 128
MLA_QK_PAD = 256

RET_HEADS = 4
RET_KEY_DIM = 128
RET_VAL_DIM = 256

D_FF = 5632
N_BRANCH = 3

SWA_OUT = SWA_HEADS * SWA_HEAD_DIM
SWA_KV_W = SWA_KV_HEADS * SWA_HEAD_DIM
MLA_OUT = MLA_HEADS * MLA_V_DIM
RET_QK_W = RET_HEADS * RET_KEY_DIM
RET_OUT = RET_HEADS * RET_VAL_DIM

A_SWA_Q = 0
A_SWA_K = A_SWA_Q + SWA_OUT
A_SWA_V = A_SWA_K + SWA_KV_W
A_MLA_CQ = A_SWA_V + SWA_KV_W
A_MLA_CKV = A_MLA_CQ + MLA_Q_RANK
IN_KR = A_MLA_CKV + MLA_KV_RANK
IN_B = IN_KR + MLA_ROPE_DIM
B_RET_Q = 0
B_RET_K = B_RET_Q + RET_QK_W
B_RET_V = B_RET_K + RET_QK_W
B_RET_G = B_RET_V + RET_OUT
B_GATE = B_RET_G + RET_OUT
B_WIDTH = B_GATE + N_BRANCH * D_MODEL
IN_WIDTH = IN_B + B_WIDTH

VMEM_LIMIT = 56 * 1024 * 1024
VMEM_LIMIT_FFN = 60 * 1024 * 1024

ROW_TILE = 1056
ROW_TILE_WIDE = 2112
MLA_UP_ROW_TILE = 528
INPROJ_COL_TILE = 768
MERGE_COL_TILE = 512
OUTPROJ_COL_TILE = 512
MLA_BLOCK = 384


def _params(semantics, vmem_limit=VMEM_LIMIT):
    return pltpu.CompilerParams(dimension_semantics=semantics, vmem_limit_bytes=vmem_limit)


def _rms(x, g):
    return x * lax.rsqrt(jnp.mean(x * x, axis=-1, keepdims=True) + EPS) * g


def _sigmoid(x):
    return 1.0 / (1.0 + jnp.exp(-x))


def _dot(a, b):
    return jnp.dot(a, b, preferred_element_type=F32)


def _dot_nt(a, b):
    return lax.dot_general(a, b, (((1,), (1,)), ((), ())), preferred_element_type=F32)


FFN_TM = ROW_TILE
FFN_TM_KEEP = 1024
FFN_TF_HEAD = 256
FFN_TF_TAIL = 512


def _ffn_start(x_ref, g_ref, o_ref, xn_ref):
    x = x_ref[...]
    xn_ref[...] = _rms(x, g_ref[...]).astype(BF16)
    o_ref[...] = x


def _ffn_accumulate(xn_ref, wg, wu, wd, o_ref):
    xn = xn_ref[...]
    gate = _dot(xn, wg)
    up = _dot(xn, wu)
    act = (gate * _sigmoid(gate) * up).astype(BF16)
    o_ref[...] += HALF_STEP * _dot(act, wd)


def _ffn_head_body(x_ref, g_ref, wg_ref, wu_ref, wd_ref, g2_ref, o_ref, wgb_ref, wub_ref, wdb_ref, hn_ref, *,
                   emit_norm):
    f = pl.program_id(0)

    @pl.when(f == 0)
    def _():
        _ffn_start(x_ref, g_ref, o_ref, hn_ref)

    wg = wg_ref[...].astype(BF16)
    wu = wu_ref[...].astype(BF16)
    wd = wd_ref[...].astype(BF16)
    wgb_ref[...] = wg
    wub_ref[...] = wu
    wdb_ref[...] = wd
    _ffn_accumulate(hn_ref, wg, wu, wd, o_ref)

    if emit_norm:
        @pl.when(f == pl.num_programs(0) - 1)
        def _():
            hn_ref[...] = _rms(o_ref[...], g2_ref[...]).astype(BF16)


def _ffn_tail_body(x_ref, g_ref, wg_ref, wu_ref, wd_ref, g2_ref, head_ref, head_n_ref, o_ref, hn_ref, *, emit_norm):
    del head_ref, head_n_ref
    f = pl.program_id(1)

    @pl.when(f == 0)
    def _():
        _ffn_start(x_ref, g_ref, o_ref, hn_ref)

    _ffn_accumulate(hn_ref, wg_ref[...], wu_ref[...], wd_ref[...], o_ref)

    if emit_norm:
        @pl.when(f == pl.num_programs(1) - 1)
        def _():
            hn_ref[...] = _rms(o_ref[...], g2_ref[...]).astype(BF16)


def _ffn(h, gain, w_gate, w_up, w_down, next_gain, layer, emit_norm, keep=None):
    m = h.shape[0]
    tfh, tft = FFN_TF_HEAD, FFN_TF_TAIL
    if keep is None:
        tm, n_tiles = FFN_TM, m // FFN_TM

        def x_row(j):
            return j * tm
    else:
        seq, lead = keep
        tm = FFN_TM_KEEP
        per_seq = (seq - lead) // tm
        n_tiles = (m // seq) * per_seq

        def x_row(j):
            return (j // per_seq) * seq + lead + (j % per_seq) * tm

    m_out = n_tiles * tm
    x_block = (pl.Element(tm), pl.Element(D_MODEL))
    gain = gain.reshape(DEPTH, 1, D_MODEL)
    next_gain = next_gain.reshape(DEPTH, 1, D_MODEL)
    head, wg_b, wu_b, wd_b, head_n = pl.pallas_call(
        functools.partial(_ffn_head_body, emit_norm=emit_norm),
        grid=(D_FF // tfh,),
        in_specs=[
            pl.BlockSpec(x_block, lambda f: (x_row(0), 0), pipeline_mode=pl.Buffered(1)),
            pl.BlockSpec((None, 1, D_MODEL), lambda f: (layer, 0, 0)),
            pl.BlockSpec((None, D_MODEL, tfh), lambda f: (layer, 0, f)),
            pl.BlockSpec((None, D_MODEL, tfh), lambda f: (layer, 0, f)),
            pl.BlockSpec((None, tfh, D_MODEL), lambda f: (layer, f, 0)),
            pl.BlockSpec((None, 1, D_MODEL), lambda f: (layer, 0, 0)),
        ],
        out_specs=[
            pl.BlockSpec((tm, D_MODEL), lambda f: (0, 0)),
            pl.BlockSpec((D_MODEL, tfh), lambda f: (0, f)),
            pl.BlockSpec((D_MODEL, tfh), lambda f: (0, f)),
            pl.BlockSpec((tfh, D_MODEL), lambda f: (f, 0)),
            pl.BlockSpec((tm, D_MODEL), lambda f: (0, 0)),
        ],
        out_shape=[
            jax.ShapeDtypeStruct((m_out, D_MODEL), F32),
            jax.ShapeDtypeStruct((D_MODEL, D_FF), BF16),
            jax.ShapeDtypeStruct((D_MODEL, D_FF), BF16),
            jax.ShapeDtypeStruct((D_FF, D_MODEL), BF16),
            jax.ShapeDtypeStruct((m_out, D_MODEL), BF16),
        ],
        compiler_params=_params(("arbitrary",), VMEM_LIMIT_FFN),
        name="ffn_head",
    )(h, gain, w_gate, w_up, w_down, next_gain)
    return pl.pallas_call(
        functools.partial(_ffn_tail_body, emit_norm=emit_norm),
        grid=(n_tiles - 1, D_FF // tft),
        in_specs=[
            pl.BlockSpec(x_block, lambda i, f: (pl.multiple_of(x_row(i + 1), 8), 0)),
            pl.BlockSpec((None, 1, D_MODEL), lambda i, f: (layer, 0, 0)),
            pl.BlockSpec((D_MODEL, tft), lambda i, f: (0, f)),
            pl.BlockSpec((D_MODEL, tft), lambda i, f: (0, f)),
            pl.BlockSpec((tft, D_MODEL), lambda i, f: (f, 0)),
            pl.BlockSpec((None, 1, D_MODEL), lambda i, f: (layer, 0, 0)),
            pl.BlockSpec(memory_space=pl.ANY),
            pl.BlockSpec(memory_space=pl.ANY),
        ],
        out_specs=[
            pl.BlockSpec((tm, D_MODEL), lambda i, f: (i + 1, 0)),
            pl.BlockSpec((tm, D_MODEL), lambda i, f: (i + 1, 0)),
        ],
        out_shape=[
            jax.ShapeDtypeStruct((m_out, D_MODEL), F32),
            jax.ShapeDtypeStruct((m_out, D_MODEL), BF16),
        ],
        input_output_aliases={6: 0, 7: 1},
        compiler_params=_params(("parallel", "arbitrary"), VMEM_LIMIT_FFN),
        name="ffn_tail",
    )(h, gain, wg_b, wu_b, wd_b, next_gain, head, head_n)


def _inproj_a_body(xn_ref, w_ref, wkr_ref, za_ref, zk_ref):
    @pl.when(pl.program_id(1) == 0)
    def _():
        kr = _dot_nt(xn_ref[...], wkr_ref[...].astype(BF16))
        zk_ref[...] = jnp.concatenate([kr, jnp.zeros_like(kr)], axis=1).astype(BF16)

    za_ref[...] = _dot_nt(xn_ref[...], w_ref[...].astype(BF16)).astype(BF16)


def _inproj_a(xn, w_in_t, layer, *, tm=ROW_TILE_WIDE, tn=INPROJ_COL_TILE):
    m = xn.shape[0]
    return pl.pallas_call(
        _inproj_a_body,
        grid=(m // tm, IN_KR // tn),
        in_specs=[
            pl.BlockSpec((tm, D_MODEL), lambda i, j: (i, 0)),
            pl.BlockSpec((None, tn, D_MODEL), lambda i, j: (layer, j, 0)),
            pl.BlockSpec((None, MLA_ROPE_DIM, D_MODEL), lambda i, j: (layer, IN_KR // MLA_ROPE_DIM, 0)),
        ],
        out_specs=[
            pl.BlockSpec((tm, tn), lambda i, j: (i, j)),
            pl.BlockSpec((tm, BLOCK), lambda i, j: (i, 0)),
        ],
        out_shape=[
            jax.ShapeDtypeStruct((m, IN_KR), BF16),
            jax.ShapeDtypeStruct((m, BLOCK), BF16),
        ],
        compiler_params=_params(("parallel", "arbitrary")),
        name="inproj_a",
    )(xn, w_in_t, w_in_t)


def _inproj_b_body(xn_ref, w_ref, o_ref):
    o_ref[...] = _dot_nt(xn_ref[...], w_ref[0].astype(BF16)).astype(BF16)


def _inproj_b(xn, w_in_t, layer, *, tm=ROW_TILE_WIDE, tn=INPROJ_COL_TILE):
    m = xn.shape[0]
    return pl.pallas_call(
        _inproj_b_body,
        grid=(m // tm, B_WIDTH // tn),
        in_specs=[
            pl.BlockSpec((tm, D_MODEL), lambda i, j: (i, 0)),
            pl.BlockSpec((pl.Element(1), pl.Element(tn), pl.Element(D_MODEL)),
                         lambda i, j: (layer, pl.multiple_of(IN_B + j * tn, 8), 0)),
        ],
        out_specs=pl.BlockSpec((tm, tn), lambda i, j: (i, j)),
        out_shape=jax.ShapeDtypeStruct((m, B_WIDTH), BF16),
        compiler_params=_params(("parallel", "arbitrary")),
        name="inproj_b",
    )(xn, w_in_t)


def _swa_body(q_ref, kc_ref, kp_ref, km_ref, vc_ref, vp_ref, vm_ref, qn_ref, kn_ref, sink_ref, o_ref):
    n = pl.program_id(0)
    j = lax.broadcasted_iota(jnp.int32, (BLOCK, BLOCK), 0)
    i_loc = lax.broadcasted_iota(jnp.int32, (BLOCK, BLOCK), 1)
    ok_cur = (j <= i_loc) & (n * BLOCK + j >= PAD)
    ok_prev = (j > i_loc) & ((n - 1) * BLOCK + j >= PAD)
    ok_meta = (j >= PAD) & (n * BLOCK + i_loc - j >= WINDOW)

    def mask_rows(s_blk, ok):
        return jnp.concatenate(
            [jnp.where(ok, s_blk[:, h * BLOCK:(h + 1) * BLOCK], NEG_INF) for h in range(SWA_GROUP)], axis=1)

    k_gain = kn_ref[...] * qn_ref[...] * (SWA_HEAD_DIM ** -0.5 * LOG2_E)
    for b in range(q_ref.shape[0]):
        _swa_block(b, q_ref, kc_ref, kp_ref, km_ref, vc_ref, vp_ref, vm_ref, sink_ref, o_ref, k_gain,
                   (ok_meta, ok_prev, ok_cur), mask_rows)


def _swa_block(b, q_ref, kc_ref, kp_ref, km_ref, vc_ref, vp_ref, vm_ref, sink_ref, o_ref, k_gain, oks, mask_rows):
    ok_meta, ok_prev, ok_cur = oks
    q_t = [q_ref[b, :, a * BLOCK:(a + 1) * BLOCK].astype(F32).T for a in range(SWA_HEADS // 2)]
    v_t = jnp.concatenate([r[b].astype(F32).T for r in (vm_ref, vp_ref, vc_ref)], axis=1)

    out_t = []
    for g in range(SWA_KV_HEADS):
        lo = g * SWA_HEAD_DIM
        hi = lo + SWA_HEAD_DIM
        heads = [g * SWA_GROUP + h for h in range(SWA_GROUP)]
        qt = jnp.concatenate(
            [q_t[hd // 2][(hd % 2) * SWA_HEAD_DIM:(hd % 2 + 1) * SWA_HEAD_DIM, :] for hd in heads], axis=1)
        q_rinv = lax.rsqrt(jnp.sum(qt * qt, axis=0, keepdims=True) * (1.0 / SWA_HEAD_DIM) + EPS)
        k_all = jnp.concatenate(
            [_rms(r[b, :, lo:hi].astype(F32), k_gain) for r in (km_ref, kp_ref, kc_ref)], axis=0).astype(BF16)
        s = _dot(k_all, qt.astype(BF16)) * q_rinv
        s = jnp.concatenate([mask_rows(s[:BLOCK], ok_meta), mask_rows(s[BLOCK:2 * BLOCK], ok_prev),
                             mask_rows(s[2 * BLOCK:], ok_cur)], axis=0)
        sink = sink_ref[g] * LOG2_E
        mx = jnp.maximum(jnp.max(s, axis=0, keepdims=True), sink)
        p = jnp.exp2(s - mx)
        den = jnp.sum(p, axis=0, keepdims=True) + jnp.exp2(sink - mx)
        o_t = _dot(v_t[lo:hi].astype(BF16), p.astype(BF16)) / den
        out_t.extend(o_t[:, h * BLOCK:(h + 1) * BLOCK] for h in range(SWA_GROUP))
    for a in range(SWA_HEADS // 2):
        pair = jnp.concatenate([out_t[2 * a], out_t[2 * a + 1]], axis=0)
        o_ref[b, :, a * BLOCK:(a + 1) * BLOCK] = pair.T.astype(BF16)


def _swa(z, q_norm, k_norm, sink_col, nb, batch):
    m = z.shape[0]
    z = z.reshape(batch, nb * BLOCK, z.shape[1])
    kcol = A_SWA_K // SWA_KV_W
    vcol = A_SWA_V // SWA_KV_W

    def cur(c):
        return pl.BlockSpec((batch, BLOCK, SWA_KV_W), lambda n: (0, n, c))

    def prev(c):
        return pl.BlockSpec((batch, BLOCK, SWA_KV_W), lambda n: (0, jnp.maximum(n - 1, 0), c))

    def meta(c):
        return pl.BlockSpec((batch, BLOCK, SWA_KV_W), lambda n: (0, 0, c))

    out = pl.pallas_call(
        _swa_body,
        grid=(nb,),
        in_specs=[
            pl.BlockSpec((batch, BLOCK, SWA_OUT), lambda n: (0, n, A_SWA_Q // SWA_OUT)),
            cur(kcol), prev(kcol), meta(kcol),
            cur(vcol), prev(vcol), meta(vcol),
            pl.BlockSpec((1, SWA_HEAD_DIM), lambda n: (0, 0)),
            pl.BlockSpec((1, SWA_HEAD_DIM), lambda n: (0, 0)),
            pl.BlockSpec((SWA_KV_HEADS, 1, SWA_GROUP * BLOCK), lambda n: (0, 0, 0)),
        ],
        out_specs=pl.BlockSpec((batch, BLOCK, SWA_OUT), lambda n: (0, n, 0)),
        out_shape=jax.ShapeDtypeStruct((batch, nb * BLOCK, SWA_OUT), BF16),
        compiler_params=_params(("arbitrary",)),
        name="swa",
    )(z, z, z, z, z, z, z, q_norm.reshape(1, -1), k_norm.reshape(1, -1), sink_col)
    return out.reshape(m, SWA_OUT)


def _rope_rows(x, cos, sin, half):
    width = x.shape[-1]
    lane = lax.broadcasted_iota(jnp.int32, x.shape, 1)
    rot = jnp.where(lane < half, pltpu.roll(x, width - half, 1), pltpu.roll(x, half, 1))
    return x * cos + rot * sin


def _rope_key_norm(r, gain):
    return r * lax.rsqrt(jnp.sum(r * r, axis=-1, keepdims=True) * (1.0 / MLA_ROPE_DIM) + EPS) * gain


def _mla_up_body(cq0_ref, cq1_ref, ckv0_ref, ckv1_ref, kr_ref, qa_ref, kva_ref, wq_ref, wkv_ref, qn_ref, qr_ref,
                 kn_ref, krn_ref, cos_ref, sin_ref, q_ref, k_ref, v_ref):
    cos = cos_ref[...]
    sin = sin_ref[...]
    half = MLA_ROPE_DIM // 2
    scale = (MLA_NOPE_DIM + MLA_ROPE_DIM) ** -0.5 * LOG2_E
    qn_gain = qn_ref[...] * scale
    qr_gain = qr_ref[...] * scale
    c_q = jnp.concatenate([cq0_ref[...], cq1_ref[...]], axis=1).astype(F32)
    c_kv = jnp.concatenate([ckv0_ref[...], ckv1_ref[...]], axis=1).astype(F32)
    yq = _dot(_rms(c_q, qa_ref[...]).astype(BF16), wq_ref[...])
    ykv = _dot(_rms(c_kv, kva_ref[...]).astype(BF16), wkv_ref[...].astype(BF16))
    k_rope = _rope_rows(_rope_key_norm(kr_ref[...].astype(F32), krn_ref[...]), cos, sin, half).astype(BF16)
    for h in range(MLA_HEADS):
        lo = h * MLA_QK_PAD
        nope = _rms(yq[:, lo:lo + MLA_NOPE_DIM], qn_gain)
        r = _rope_rows(_rope_key_norm(yq[:, lo + MLA_NOPE_DIM:lo + MLA_QK_PAD], qr_gain), cos, sin, half)
        q_ref[:, lo:lo + MLA_NOPE_DIM] = nope.astype(BF16)
        q_ref[:, lo + MLA_NOPE_DIM:lo + MLA_QK_PAD] = r.astype(BF16)
        k_ref[:, lo:lo + MLA_NOPE_DIM] = _rms(ykv[:, lo:lo + MLA_NOPE_DIM], kn_ref[...]).astype(BF16)
        k_ref[:, lo + MLA_NOPE_DIM:lo + MLA_QK_PAD] = k_rope
        v_ref[:, h * MLA_V_DIM:(h + 1) * MLA_V_DIM] = ykv[:, lo + MLA_NOPE_DIM:lo + MLA_QK_PAD].astype(BF16)


def _mla_up(z_a, z_kr, qa_norm, kva_norm, w_uq_pad, w_ukv, qn_norm, qr_norm_pad, kn_norm, kr_norm_pad, cos_t, sin_t,
            layer, seq, *, tm=MLA_UP_ROW_TILE):
    m = z_a.shape[0]
    seq_tiles = seq // tm
    kv_w = MLA_NOPE_DIM + MLA_V_DIM
    half_rank = MLA_Q_RANK // 2

    def row(width, col):
        return pl.BlockSpec((tm, width), lambda i: (i, col))

    def vec(width):
        return pl.BlockSpec((1, width), lambda i: (0, 0))

    return pl.pallas_call(
        _mla_up_body,
        grid=(m // tm,),
        in_specs=[
            row(half_rank, A_MLA_CQ // half_rank), row(half_rank, A_MLA_CQ // half_rank + 1),
            row(half_rank, A_MLA_CKV // half_rank), row(half_rank, A_MLA_CKV // half_rank + 1),
            row(BLOCK, 0),
            vec(MLA_Q_RANK), vec(MLA_KV_RANK),
            pl.BlockSpec((None, MLA_Q_RANK, MLA_HEADS * MLA_QK_PAD), lambda i: (layer, 0, 0)),
            pl.BlockSpec((None, MLA_KV_RANK, MLA_HEADS * kv_w), lambda i: (layer, 0, 0)),
            vec(MLA_NOPE_DIM), vec(BLOCK), vec(MLA_NOPE_DIM), vec(BLOCK),
            pl.BlockSpec((tm, BLOCK), lambda i: (i % seq_tiles, 0)),
            pl.BlockSpec((tm, BLOCK), lambda i: (i % seq_tiles, 0)),
        ],
        out_specs=[row(MLA_HEADS * MLA_QK_PAD, 0), row(MLA_HEADS * MLA_QK_PAD, 0), row(MLA_OUT, 0)],
        out_shape=[
            jax.ShapeDtypeStruct((m, MLA_HEADS * MLA_QK_PAD), BF16),
            jax.ShapeDtypeStruct((m, MLA_HEADS * MLA_QK_PAD), BF16),
            jax.ShapeDtypeStruct((m, MLA_OUT), BF16),
        ],
        compiler_params=_params(("parallel",)),
        name="mla_up",
    )(z_a, z_a, z_a, z_a, z_kr, qa_norm.reshape(1, -1), kva_norm.reshape(1, -1), w_uq_pad, w_ukv,
      qn_norm.reshape(1, -1), qr_norm_pad, kn_norm.reshape(1, -1), kr_norm_pad, cos_t, sin_t)


def _mla_attn_body(q_ref, k_ref, v_ref, o_ref, m_ref, acc_ref, *, t, hb):
    i = pl.program_id(2)
    m_ref[...] = jnp.full(m_ref.shape, NEG_INF, F32)
    acc_ref[...] = jnp.zeros(acc_ref.shape, F32)

    def tile(blk, nblk, masked):
        width = nblk * t
        off = blk * t if isinstance(blk, int) else pl.multiple_of(blk * t, t)
        if masked:
            row = i * t + lax.broadcasted_iota(jnp.int32, (t, width), 0)
            col = blk * t + lax.broadcasted_iota(jnp.int32, (t, width), 1)
            ok = (col <= row) & (col >= PAD)
        for h in range(hb):
            q = q_ref[:, h * MLA_QK_PAD:(h + 1) * MLA_QK_PAD]
            kb = k_ref[pl.ds(off, width), h * MLA_QK_PAD:(h + 1) * MLA_QK_PAD]
            vb = v_ref[pl.ds(off, width), h * MLA_V_DIM:(h + 1) * MLA_V_DIM]
            s = _dot_nt(q, kb)
            if masked:
                s = jnp.where(ok, s, NEG_INF)
            m_run = m_ref[h]
            m_new = jnp.maximum(m_run, jnp.max(s, axis=-1, keepdims=True))
            alpha = jnp.exp2(m_run - m_new)
            p = jnp.concatenate(
                [jnp.exp2(s[:, c * BLOCK:(c + 1) * BLOCK] - m_new) for c in range(width // BLOCK)], axis=1)
            v_aug = jnp.concatenate([vb, jnp.ones_like(vb)], axis=1)
            acc_ref[h] = jnp.concatenate([alpha, alpha], axis=1) * acc_ref[h] + _dot(p.astype(BF16), v_aug)
            m_ref[h] = m_new

    @pl.when(i >= 2)
    def _():
        tile(0, 2, True)

    def body(jj, carry):
        tile(2 * jj, 2, False)
        return carry

    lax.fori_loop(1, i // 2, body, 0)

    @pl.when(i % 2 == 1)
    def _():
        tile(i - 1, 2, True)

    @pl.when(i % 2 == 0)
    def _():
        tile(i, 1, True)

    for h in range(hb):
        acc = acc_ref[h]
        o_ref[:, h * MLA_V_DIM:(h + 1) * MLA_V_DIM] = (acc[:, :MLA_V_DIM] / acc[:, MLA_V_DIM:]).astype(BF16)


def _mla_attn(q, k, v, seq, batch, *, t=MLA_BLOCK, hb=MLA_HEADS):
    m = q.shape[0]
    nq = seq // t
    return pl.pallas_call(
        functools.partial(_mla_attn_body, t=t, hb=hb),
        grid=(batch, MLA_HEADS // hb, nq),
        in_specs=[
            pl.BlockSpec((t, hb * MLA_QK_PAD), lambda b, h, i: (b * nq + i, h)),
            pl.BlockSpec((seq, hb * MLA_QK_PAD), lambda b, h, i: (b, h), pipeline_mode=pl.Buffered(1)),
            pl.BlockSpec((seq, hb * MLA_V_DIM), lambda b, h, i: (b, h), pipeline_mode=pl.Buffered(1)),
        ],
        out_specs=pl.BlockSpec((t, hb * MLA_V_DIM), lambda b, h, i: (b * nq + i, h)),
        out_shape=jax.ShapeDtypeStruct((m, MLA_OUT), BF16),
        scratch_shapes=[pltpu.VMEM((hb, t, BLOCK), F32), pltpu.VMEM((hb, t, 2 * MLA_V_DIM), F32)],
        compiler_params=_params(("parallel", "parallel", "arbitrary")),
        name="mla_attn",
    )(q, k, v)


def _ret_body(q_ref, k_ref, v_ref, g_ref, cos_ref, sin_ref, dec_ref, zeta_ref, xi_ref, gn_ref, o_ref,
              state_ref, *, chunk_decay):
    n = pl.program_id(0)

    @pl.when(n == 0)
    def _():
        state_ref[...] = jnp.zeros_like(state_ref)

    cos = cos_ref[...]
    sin = sin_ref[...]
    valid = (n * BLOCK + lax.broadcasted_iota(jnp.int32, (BLOCK, 1), 0) >= PAD).astype(F32)
    for b in range(q_ref.shape[0]):
        for h in range(RET_HEADS):
            ks = slice(h * RET_KEY_DIM, (h + 1) * RET_KEY_DIM)
            vs = slice(h * RET_VAL_DIM, (h + 1) * RET_VAL_DIM)
            qh = _rope_rows(q_ref[b, :, ks].astype(F32), cos, sin, RET_KEY_DIM // 2)
            kh = (_rope_rows(k_ref[b, :, ks].astype(F32), cos, sin, RET_KEY_DIM // 2)
                  * (RET_KEY_DIM ** -0.5) * valid)
            vb = v_ref[b, :, vs]
            s = _dot_nt(qh.astype(BF16), kh.astype(BF16)) * dec_ref[h]
            inner = _dot(s.astype(BF16), vb)
            prev = state_ref[b, h]
            cross = _dot((qh * xi_ref[h]).astype(BF16), prev.astype(BF16))
            kz_t = (kh * zeta_ref[h]).T.astype(BF16)
            state_ref[b, h] = prev * chunk_decay[h] + _dot(kz_t, vb)
            o = inner + cross
            mu = jnp.mean(o, axis=-1, keepdims=True)
            d = o - mu
            y = d * lax.rsqrt(jnp.mean(d * d, axis=-1, keepdims=True) + EPS)
            gate = g_ref[b, :, vs].astype(F32)
            o_ref[b, :, vs] = (y * gn_ref[:, vs] * (gate * _sigmoid(gate))).astype(BF16)


def _retention(z, cos_t, sin_t, decay, zeta_b, xi_b, gn, chunk_decay, nb, batch):
    m = z.shape[0]
    z = z.reshape(batch, nb * BLOCK, z.shape[1])
    tab = pl.BlockSpec((RET_HEADS, BLOCK, BLOCK), lambda n: (0, 0, 0))

    def rows(width, col):
        return pl.BlockSpec((batch, BLOCK, width), lambda n: (0, n, col // width))

    out = pl.pallas_call(
        functools.partial(_ret_body, chunk_decay=chunk_decay),
        grid=(nb,),
        in_specs=[
            rows(RET_QK_W, B_RET_Q), rows(RET_QK_W, B_RET_K), rows(RET_OUT, B_RET_V), rows(RET_OUT, B_RET_G),
            pl.BlockSpec((BLOCK, BLOCK), lambda n: (n, 0)),
            pl.BlockSpec((BLOCK, BLOCK), lambda n: (n, 0)),
            tab, tab, tab,
            pl.BlockSpec((1, RET_OUT), lambda n: (0, 0)),
        ],
        out_specs=pl.BlockSpec((batch, BLOCK, RET_OUT), lambda n: (0, n, 0)),
        out_shape=jax.ShapeDtypeStruct((batch, nb * BLOCK, RET_OUT), BF16),
        scratch_shapes=[pltpu.VMEM((batch, RET_HEADS, RET_KEY_DIM, RET_VAL_DIM), F32)],
        compiler_params=_params(("arbitrary",)),
        name="retention",
    )(z, z, z, z, cos_t, sin_t, decay, zeta_b, xi_b, gn.reshape(1, -1))
    return out.reshape(m, RET_OUT)


def _merge_body(oa_ref, ob_ref, oc_ref, wa_ref, wb_ref, wc_ref, ga_ref, gb_ref, gc_ref, o_ref):
    def branch(o, w, g):
        return _sigmoid(g[...].astype(F32)) * _dot(o[...], w[...].astype(BF16))

    o_ref[...] = (branch(oa_ref, wa_ref, ga_ref) + branch(ob_ref, wb_ref, gb_ref)
                  + branch(oc_ref, wc_ref, gc_ref)).astype(BF16)


def _merge(z, o_a, o_b, o_c, w_a, w_b, w_c, layer, *, tm=ROW_TILE, tn=MERGE_COL_TILE):
    m = z.shape[0]
    nt = D_MODEL // tn

    def act(width):
        return pl.BlockSpec((tm, width), lambda i, j: (i, 0))

    def wgt(width):
        return pl.BlockSpec((None, width, tn), lambda i, j: (layer, 0, j))

    def gate(br):
        return pl.BlockSpec((tm, tn), lambda i, j: (i, B_GATE // tn + br * nt + j))

    return pl.pallas_call(
        _merge_body,
        grid=(m // tm, nt),
        in_specs=[act(SWA_OUT), act(MLA_OUT), act(RET_OUT), wgt(SWA_OUT), wgt(MLA_OUT), wgt(RET_OUT),
                  gate(0), gate(1), gate(2)],
        out_specs=pl.BlockSpec((tm, tn), lambda i, j: (i, j)),
        out_shape=jax.ShapeDtypeStruct((m, D_MODEL), BF16),
        compiler_params=_params(("parallel", "arbitrary")),
        name="merge",
    )(o_a, o_b, o_c, w_a, w_b, w_c, z, z, z)


def _outproj_body(x_ref, w_ref, r_ref, o_ref):
    o_ref[...] = r_ref[...] + _dot(x_ref[...], w_ref[...].astype(BF16))


def _outproj(x, w, res, layer, *, tm=ROW_TILE_WIDE, tn=OUTPROJ_COL_TILE):
    m = x.shape[0]
    return pl.pallas_call(
        _outproj_body,
        grid=(m // tm, D_MODEL // tn),
        in_specs=[
            pl.BlockSpec((tm, D_MODEL), lambda i, j: (i, 0)),
            pl.BlockSpec((None, D_MODEL, tn), lambda i, j: (layer, 0, j)),
            pl.BlockSpec((tm, tn), lambda i, j: (i, j)),
        ],
        out_specs=pl.BlockSpec((tm, tn), lambda i, j: (i, j)),
        out_shape=jax.ShapeDtypeStruct((m, D_MODEL), F32),
        compiler_params=_params(("parallel", "arbitrary")),
        name="outproj",
    )(x, w, res)


def _rope_tables(pos, dim, width):
    half = dim // 2
    inv_freq = ROPE_BASE ** (-jnp.arange(half, dtype=F32) / half)
    ang = pos[:, None] * inv_freq[None, :]
    cos = jnp.cos(ang)
    sin = jnp.sin(ang)
    fill = jnp.zeros((pos.shape[0], width - dim), F32)
    return (jnp.concatenate([cos, cos, fill], axis=1), jnp.concatenate([-sin, sin, fill], axis=1))


def kernel(x, meta_tokens, ffn1_norm, ffn1_w_gate, ffn1_w_up, ffn1_w_down, mix_norm, w_in, swa_q_norm, swa_k_norm, swa_sinks, mla_q_a_norm, mla_w_uq, mla_kv_a_norm, mla_w_ukv, mla_qn_norm, mla_qr_norm, mla_kn_norm, mla_kr_norm, ret_gn, w_br_swa, w_br_mla, w_br_ret, w_o, ffn2_norm, ffn2_w_gate, ffn2_w_up, ffn2_w_down):
    batch, seq_in, _ = x.shape
    seq = seq_in + BLOCK
    nb = seq // BLOCK
    assert x.shape[2] == D_MODEL and seq % BLOCK == 0 and seq % MLA_BLOCK == 0 and seq % MLA_UP_ROW_TILE == 0
    assert (batch * seq) % ROW_TILE_WIDE == 0 and seq_in % FFN_TM_KEEP == 0

    meta = jnp.broadcast_to(meta_tokens[None].astype(x.dtype), (batch, N_META, D_MODEL))
    h = jnp.concatenate([jnp.zeros((batch, PAD, D_MODEL), x.dtype), meta, x], axis=1)
    h = h.reshape(batch * seq, D_MODEL)
    pos = (jnp.arange(seq) - PAD).astype(F32)

    w_in_t = jnp.swapaxes(w_in, 1, 2)
    w_br_swa, w_br_mla, w_br_ret, w_o, mla_w_ukv = (
        w.astype(BF16) for w in (w_br_swa, w_br_mla, w_br_ret, w_o, mla_w_ukv))

    qk_dim = MLA_NOPE_DIM + MLA_ROPE_DIM
    w_uq_pad = jnp.pad(mla_w_uq.reshape(DEPTH, MLA_Q_RANK, MLA_HEADS, qk_dim),
                       ((0, 0), (0, 0), (0, 0), (0, MLA_QK_PAD - qk_dim)))
    w_uq_pad = w_uq_pad.reshape(DEPTH, MLA_Q_RANK, MLA_HEADS * MLA_QK_PAD).astype(BF16)
    rope_fill = jnp.zeros((DEPTH, BLOCK - MLA_ROPE_DIM), F32)
    qr_norm_pad = jnp.concatenate([mla_qr_norm, rope_fill], axis=1)
    kr_norm_pad = jnp.concatenate([mla_kr_norm, rope_fill], axis=1)

    cos_m, sin_m = _rope_tables(pos, MLA_ROPE_DIM, BLOCK)
    cos_r, sin_r = _rope_tables(pos, RET_KEY_DIM, BLOCK)

    log_gamma = jnp.log(1.0 - 2.0 ** (-5.0 - jnp.arange(RET_HEADS, dtype=F32)))
    idx = jnp.arange(BLOCK, dtype=F32)
    diff = idx[:, None] - idx[None, :]
    decay = jnp.where(diff[None] >= 0, jnp.exp(jnp.maximum(diff, 0.0)[None] * log_gamma[:, None, None]), 0.0)
    zeta = jnp.exp((BLOCK - 1.0 - idx)[None, :] * log_gamma[:, None])
    xi = jnp.exp((idx + 1.0)[None, :] * log_gamma[:, None])
    zeta_b = jnp.broadcast_to(zeta[:, :, None], (RET_HEADS, BLOCK, RET_KEY_DIM))
    xi_b = jnp.broadcast_to(xi[:, :, None], (RET_HEADS, BLOCK, RET_KEY_DIM))
    chunk_decay = tuple(float(np.exp(BLOCK * np.log(1.0 - 2.0 ** (-5.0 - hh)))) for hh in range(RET_HEADS))

    for l in range(DEPTH):
        h, hn = _ffn(h, ffn1_norm, ffn1_w_gate, ffn1_w_up, ffn1_w_down, mix_norm, l, True)
        z_a, z_kr = _inproj_a(hn, w_in_t, l)
        z_b = _inproj_b(hn, w_in_t, l)
        sink_col = jnp.repeat(swa_sinks[l].astype(F32).reshape(SWA_KV_HEADS, SWA_GROUP), BLOCK, axis=1)
        o_a = _swa(z_a, swa_q_norm[l], swa_k_norm[l], sink_col[:, None, :], nb, batch)
        q, k, v = _mla_up(z_a, z_kr, mla_q_a_norm[l], mla_kv_a_norm[l], w_uq_pad, mla_w_ukv, mla_qn_norm[l],
                          qr_norm_pad[l:l + 1], mla_kn_norm[l], kr_norm_pad[l:l + 1], cos_m, sin_m, l, seq)
        o_b = _mla_attn(q, k, v, seq, batch)
        o_c = _retention(z_b, cos_r, sin_r, decay, zeta_b, xi_b, ret_gn[l], chunk_decay, nb, batch)
        merged = _merge(z_b, o_a, o_b, o_c, w_br_swa, w_br_mla, w_br_ret, l)
        h = _outproj(merged, w_o, h, l)
        last = l == DEPTH - 1
        h, _ = _ffn(h, ffn2_norm, ffn2_w_gate, ffn2_w_up, ffn2_w_down, ffn2_norm, l, False,
                    keep=(seq, BLOCK) if last else None)
    return h.reshape(batch, seq_in, D_MODEL)
```

```python
import functools

import jax
import jax.numpy as jnp
import numpy as np
from jax import lax
from jax.experimental import pallas as pl
from jax.experimental.pallas import tpu as pltpu

F32 = jnp.float32
BF16 = jnp.bfloat16

D_MODEL = 2048
DEPTH = 4
N_META = 16
BLOCK = 128
PAD = BLOCK - N_META
EPS = 1e-6
NEG_INF = -1e30
ROPE_BASE = 10000.0
HALF_STEP = 0.5
LOG2_E = 1.4426950408889634

SWA_HEADS = 16
SWA_KV_HEADS = 2
SWA_HEAD_DIM = 64
SWA_GROUP = SWA_HEADS // SWA_KV_HEADS
WINDOW = 128

MLA_HEADS = 8
MLA_Q_RANK = 512
MLA_KV_RANK = 512
MLA_NOPE_DIM = 128
MLA_ROPE_DIM = 64
MLA_V_DIM = 128
MLA_QK_PAD = 256

RET_HEADS = 4
RET_KEY_DIM = 128
RET_VAL_DIM = 256

D_FF = 5632
N_BRANCH = 3

SWA_OUT = SWA_HEADS * SWA_HEAD_DIM
SWA_KV_W = SWA_KV_HEADS * SWA_HEAD_DIM
MLA_OUT = MLA_HEADS * MLA_V_DIM
RET_QK_W = RET_HEADS * RET_KEY_DIM
RET_OUT = RET_HEADS * RET_VAL_DIM

A_SWA_Q = 0
A_SWA_K = A_SWA_Q + SWA_OUT
A_SWA_V = A_SWA_K + SWA_KV_W
A_MLA_CQ = A_SWA_V + SWA_KV_W
A_MLA_CKV = A_MLA_CQ + MLA_Q_RANK
IN_KR = A_MLA_CKV + MLA_KV_RANK
IN_B = IN_KR + MLA_ROPE_DIM
B_RET_Q = 0
B_RET_K = B_RET_Q + RET_QK_W
B_RET_V = B_RET_K + RET_QK_W
B_RET_G = B_RET_V + RET_OUT
B_GATE = B_RET_G + RET_OUT
B_WIDTH = B_GATE + N_BRANCH * D_MODEL
IN_WIDTH = IN_B + B_WIDTH

VMEM_LIMIT = 56 * 1024 * 1024
VMEM_LIMIT_FFN = 60 * 1024 * 1024

ROW_TILE = 1056
ROW_TILE_WIDE = 2112
MLA_UP_ROW_TILE = 528
INPROJ_COL_TILE = 768
MERGE_COL_TILE = 512
OUTPROJ_COL_TILE = 512
MLA_BLOCK = 384


def _params(semantics, vmem_limit=VMEM_LIMIT):
    return pltpu.CompilerParams(dimension_semantics=semantics, vmem_limit_bytes=vmem_limit)


def _rms(x, g):
    return x * lax.rsqrt(jnp.mean(x * x, axis=-1, keepdims=True) + EPS) * g


def _sigmoid(x):
    return 1.0 / (1.0 + jnp.exp(-x))


def _dot(a, b):
    return jnp.dot(a, b, preferred_element_type=F32)


def _dot_nt(a, b):
    return lax.dot_general(a, b, (((1,), (1,)), ((), ())), preferred_element_type=F32)


FFN_TM = ROW_TILE
FFN_TM_KEEP = 1024
FFN_TF_HEAD = 256
FFN_TF_TAIL = 512


def _ffn_start(x_ref, g_ref, o_ref, xn_ref):
    x = x_ref[...]
    xn_ref[...] = _rms(x, g_ref[...]).astype(BF16)
    o_ref[...] = x


def _ffn_accumulate(xn_ref, wg_parts, wu_parts, wd, o_ref):
    xn = xn_ref[...]
    gate = jnp.concatenate([_dot(xn, w) for w in wg_parts], axis=1)
    up = jnp.concatenate([_dot(xn, w) for w in wu_parts], axis=1)
    act = (gate * _sigmoid(gate) * up).astype(BF16)
    o_ref[...] += HALF_STEP * _dot(act, wd)


def _ffn_head_body(x_ref, g_ref, wg_ref, wu_ref, wd_ref, g2_ref, o_ref, wgb_ref, wub_ref, wdb_ref, hn_ref, *,
                   emit_norm):
    f = pl.program_id(0)

    @pl.when(f == 0)
    def _():
        _ffn_start(x_ref, g_ref, o_ref, hn_ref)

    wg = wg_ref[...].astype(BF16)
    wu = wu_ref[...].astype(BF16)
    wd = wd_ref[...].astype(BF16)
    wgb_ref[...] = wg
    wub_ref[...] = wu
    wdb_ref[...] = wd
    _ffn_accumulate(hn_ref, [wg], [wu], wd, o_ref)

    if emit_norm:
        @pl.when(f == pl.num_programs(0) - 1)
        def _():
            hn_ref[...] = _rms(o_ref[...], g2_ref[...]).astype(BF16)


def _ffn_tail_body(x_ref, g_ref, wg_ref, wu_ref, wd_ref, g2_ref, head_ref, head_n_ref, o_ref, hn_ref, *, emit_norm):
    del head_ref, head_n_ref
    f = pl.program_id(1)

    @pl.when(f == 0)
    def _():
        _ffn_start(x_ref, g_ref, o_ref, hn_ref)

    slabs = range(wg_ref.shape[0])
    _ffn_accumulate(hn_ref, [wg_ref[c] for c in slabs], [wu_ref[c] for c in slabs], wd_ref[...], o_ref)

    if emit_norm:
        @pl.when(f == pl.num_programs(1) - 1)
        def _():
            hn_ref[...] = _rms(o_ref[...], g2_ref[...]).astype(BF16)


def _ffn(h, gain, w_gate, w_up, w_down, next_gain, layer, emit_norm, keep=None):
    m = h.shape[0]
    tfh, tft = FFN_TF_HEAD, FFN_TF_TAIL
    if keep is None:
        tm, n_tiles = FFN_TM, m // FFN_TM

        def x_row(j):
            return j * tm
    else:
        seq, lead = keep
        tm = FFN_TM_KEEP
        per_seq = (seq - lead) // tm
        n_tiles = (m // seq) * per_seq

        def x_row(j):
            return (j // per_seq) * seq + lead + (j % per_seq) * tm

    m_out = n_tiles * tm
    x_block = (pl.Element(tm), pl.Element(D_MODEL))
    gain = gain.reshape(DEPTH, 1, D_MODEL)
    next_gain = next_gain.reshape(DEPTH, 1, D_MODEL)
    head, wg_b, wu_b, wd_b, head_n = pl.pallas_call(
        functools.partial(_ffn_head_body, emit_norm=emit_norm),
        grid=(D_FF // tfh,),
        in_specs=[
            pl.BlockSpec(x_block, lambda f: (x_row(0), 0), pipeline_mode=pl.Buffered(1)),
            pl.BlockSpec((None, 1, D_MODEL), lambda f: (layer, 0, 0)),
            pl.BlockSpec((None, D_MODEL, tfh), lambda f: (layer, 0, f)),
            pl.BlockSpec((None, D_MODEL, tfh), lambda f: (layer, 0, f)),
            pl.BlockSpec((None, tfh, D_MODEL), lambda f: (layer, f, 0)),
            pl.BlockSpec((None, 1, D_MODEL), lambda f: (layer, 0, 0)),
        ],
        out_specs=[
            pl.BlockSpec((tm, D_MODEL), lambda f: (0, 0)),
            pl.BlockSpec((None, D_MODEL, tfh), lambda f: (f, 0, 0)),
            pl.BlockSpec((None, D_MODEL, tfh), lambda f: (f, 0, 0)),
            pl.BlockSpec((tfh, D_MODEL), lambda f: (f, 0)),
            pl.BlockSpec((tm, D_MODEL), lambda f: (0, 0)),
        ],
        out_shape=[
            jax.ShapeDtypeStruct((m_out, D_MODEL), F32),
            jax.ShapeDtypeStruct((D_FF // tfh, D_MODEL, tfh), BF16),
            jax.ShapeDtypeStruct((D_FF // tfh, D_MODEL, tfh), BF16),
            jax.ShapeDtypeStruct((D_FF, D_MODEL), BF16),
            jax.ShapeDtypeStruct((m_out, D_MODEL), BF16),
        ],
        compiler_params=_params(("arbitrary",), VMEM_LIMIT_FFN),
        name="ffn_head",
    )(h, gain, w_gate, w_up, w_down, next_gain)
    return pl.pallas_call(
        functools.partial(_ffn_tail_body, emit_norm=emit_norm),
        grid=(n_tiles - 1, D_FF // tft),
        in_specs=[
            pl.BlockSpec(x_block, lambda i, f: (pl.multiple_of(x_row(i + 1), 8), 0)),
            pl.BlockSpec((None, 1, D_MODEL), lambda i, f: (layer, 0, 0)),
            pl.BlockSpec((tft // tfh, D_MODEL, tfh), lambda i, f: (f, 0, 0)),
            pl.BlockSpec((tft // tfh, D_MODEL, tfh), lambda i, f: (f, 0, 0)),
            pl.BlockSpec((tft, D_MODEL), lambda i, f: (f, 0)),
            pl.BlockSpec((None, 1, D_MODEL), lambda i, f: (layer, 0, 0)),
            pl.BlockSpec(memory_space=pl.ANY),
            pl.BlockSpec(memory_space=pl.ANY),
        ],
        out_specs=[
            pl.BlockSpec((tm, D_MODEL), lambda i, f: (i + 1, 0)),
            pl.BlockSpec((tm, D_MODEL), lambda i, f: (i + 1, 0)),
        ],
        out_shape=[
            jax.ShapeDtypeStruct((m_out, D_MODEL), F32),
            jax.ShapeDtypeStruct((m_out, D_MODEL), BF16),
        ],
        input_output_aliases={6: 0, 7: 1},
        compiler_params=_params(("parallel", "arbitrary"), VMEM_LIMIT_FFN),
        name="ffn_tail",
    )(h, gain, wg_b, wu_b, wd_b, next_gain, head, head_n)


def _inproj_a_body(xn_ref, w_ref, wkr_ref, za_ref, zk_ref):
    @pl.when(pl.program_id(1) == 0)
    def _():
        kr = _dot_nt(xn_ref[...], wkr_ref[...].astype(BF16))
        zk_ref[...] = jnp.concatenate([kr, jnp.zeros_like(kr)], axis=1).astype(BF16)

    za_ref[...] = _dot_nt(xn_ref[...], w_ref[...].astype(BF16)).astype(BF16)


def _inproj_a(xn, w_in_t, layer, *, tm=ROW_TILE_WIDE, tn=INPROJ_COL_TILE):
    m = xn.shape[0]
    return pl.pallas_call(
        _inproj_a_body,
        grid=(m // tm, IN_KR // tn),
        in_specs=[
            pl.BlockSpec((tm, D_MODEL), lambda i, j: (i, 0)),
            pl.BlockSpec((None, tn, D_MODEL), lambda i, j: (layer, j, 0)),
            pl.BlockSpec((None, MLA_ROPE_DIM, D_MODEL), lambda i, j: (layer, IN_KR // MLA_ROPE_DIM, 0)),
        ],
        out_specs=[
            pl.BlockSpec((tm, tn), lambda i, j: (i, j)),
            pl.BlockSpec((tm, BLOCK), lambda i, j: (i, 0)),
        ],
        out_shape=[
            jax.ShapeDtypeStruct((m, IN_KR), BF16),
            jax.ShapeDtypeStruct((m, BLOCK), BF16),
        ],
        compiler_params=_params(("parallel", "arbitrary")),
        name="inproj_a",
    )(xn, w_in_t, w_in_t)


def _swa_step(n, q_ref, kc_ref, kp_ref, km_ref, vc_ref, vp_ref, vm_ref, qn_ref, kn_ref, sink_ref, o_ref):
    j = lax.broadcasted_iota(jnp.int32, (BLOCK, BLOCK), 0)
    i_loc = lax.broadcasted_iota(jnp.int32, (BLOCK, BLOCK), 1)
    ok_cur = (j <= i_loc) & (n * BLOCK + j >= PAD)
    ok_prev = (j > i_loc) & ((n - 1) * BLOCK + j >= PAD)
    ok_meta = (j >= PAD) & (n * BLOCK + i_loc - j >= WINDOW)

    def mask_rows(s_blk, ok):
        return jnp.concatenate(
            [jnp.where(ok, s_blk[:, h * BLOCK:(h + 1) * BLOCK], NEG_INF) for h in range(SWA_GROUP)], axis=1)

    k_gain = kn_ref[...] * qn_ref[...] * (SWA_HEAD_DIM ** -0.5 * LOG2_E)
    for b in range(q_ref.shape[0]):
        _swa_block(b, q_ref, kc_ref, kp_ref, km_ref, vc_ref, vp_ref, vm_ref, sink_ref, o_ref, k_gain,
                   (ok_meta, ok_prev, ok_cur), mask_rows)


def _swa_block(b, q_ref, kc_ref, kp_ref, km_ref, vc_ref, vp_ref, vm_ref, sink_ref, o_ref, k_gain, oks, mask_rows):
    ok_meta, ok_prev, ok_cur = oks
    q_t = [q_ref[b, :, a * BLOCK:(a + 1) * BLOCK].astype(F32).T for a in range(SWA_HEADS // 2)]
    v_t = jnp.concatenate([r[b].astype(F32).T for r in (vm_ref, vp_ref, vc_ref)], axis=1)

    out_t = []
    for g in range(SWA_KV_HEADS):
        lo = g * SWA_HEAD_DIM
        hi = lo + SWA_HEAD_DIM
        heads = [g * SWA_GROUP + h for h in range(SWA_GROUP)]
        qt = jnp.concatenate(
            [q_t[hd // 2][(hd % 2) * SWA_HEAD_DIM:(hd % 2 + 1) * SWA_HEAD_DIM, :] for hd in heads], axis=1)
        q_rinv = lax.rsqrt(jnp.sum(qt * qt, axis=0, keepdims=True) * (1.0 / SWA_HEAD_DIM) + EPS)
        k_all = jnp.concatenate(
            [_rms(r[b, :, lo:hi].astype(F32), k_gain) for r in (km_ref, kp_ref, kc_ref)], axis=0).astype(BF16)
        s = _dot(k_all, qt.astype(BF16)) * q_rinv
        s = jnp.concatenate([mask_rows(s[:BLOCK], ok_meta), mask_rows(s[BLOCK:2 * BLOCK], ok_prev),
                             mask_rows(s[2 * BLOCK:], ok_cur)], axis=0)
        sink = sink_ref[g] * LOG2_E
        mx = jnp.maximum(jnp.max(s, axis=0, keepdims=True), sink)
        p = jnp.exp2(s - mx)
        den = jnp.sum(p, axis=0, keepdims=True) + jnp.exp2(sink - mx)
        o_t = _dot(v_t[lo:hi].astype(BF16), p.astype(BF16)) / den
        out_t.extend(o_t[:, h * BLOCK:(h + 1) * BLOCK] for h in range(SWA_GROUP))
    for a in range(SWA_HEADS // 2):
        pair = jnp.concatenate([out_t[2 * a], out_t[2 * a + 1]], axis=0)
        o_ref[b, :, a * BLOCK:(a + 1) * BLOCK] = pair.T.astype(BF16)


def _inproj_b_swa_body(xn_ref, w_ref, q_ref, kc_ref, kp_ref, km_ref, vc_ref, vp_ref, vm_ref, qn_ref, kn_ref,
                       sink_ref, zb_ref, oa_ref, *, n_col, nb):
    step = pl.program_id(0) * n_col + pl.program_id(1)

    def project():
        zb_ref[...] = _dot_nt(xn_ref[...], w_ref[0].astype(BF16)).astype(BF16)

    @pl.when(step < nb)
    def _():
        project()
        _swa_step(step, q_ref, kc_ref, kp_ref, km_ref, vc_ref, vp_ref, vm_ref, qn_ref, kn_ref, sink_ref, oa_ref)

    @pl.when(step >= nb)
    def _():
        project()


def _inproj_b_swa(xn, w_in_t, z_a, q_norm, k_norm, sink_col, layer, nb, batch, *, tm=ROW_TILE_WIDE,
                  tn=INPROJ_COL_TILE):
    m = xn.shape[0]
    n_col = B_WIDTH // tn
    assert (m // tm) * n_col >= nb
    z_a = z_a.reshape(batch, nb * BLOCK, z_a.shape[1])
    kcol = A_SWA_K // SWA_KV_W
    vcol = A_SWA_V // SWA_KV_W

    def blk(i, j):
        return jnp.minimum(i * n_col + j, nb - 1)

    def cur(c):
        return pl.BlockSpec((batch, BLOCK, SWA_KV_W), lambda i, j: (0, blk(i, j), c))

    def prev(c):
        return pl.BlockSpec((batch, BLOCK, SWA_KV_W), lambda i, j: (0, jnp.maximum(blk(i, j) - 1, 0), c))

    def meta(c):
        return pl.BlockSpec((batch, BLOCK, SWA_KV_W), lambda i, j: (0, 0, c))

    z_b, o_a = pl.pallas_call(
        functools.partial(_inproj_b_swa_body, n_col=n_col, nb=nb),
        grid=(m // tm, n_col),
        in_specs=[
            pl.BlockSpec((tm, D_MODEL), lambda i, j: (i, 0)),
            pl.BlockSpec((pl.Element(1), pl.Element(tn), pl.Element(D_MODEL)),
                         lambda i, j: (layer, pl.multiple_of(IN_B + j * tn, 8), 0)),
            pl.BlockSpec((batch, BLOCK, SWA_OUT), lambda i, j: (0, blk(i, j), A_SWA_Q // SWA_OUT)),
            cur(kcol), prev(kcol), meta(kcol),
            cur(vcol), prev(vcol), meta(vcol),
            pl.BlockSpec((1, SWA_HEAD_DIM), lambda i, j: (0, 0)),
            pl.BlockSpec((1, SWA_HEAD_DIM), lambda i, j: (0, 0)),
            pl.BlockSpec((SWA_KV_HEADS, 1, SWA_GROUP * BLOCK), lambda i, j: (0, 0, 0)),
        ],
        out_specs=[
            pl.BlockSpec((tm, tn), lambda i, j: (i, j)),
            pl.BlockSpec((batch, BLOCK, SWA_OUT), lambda i, j: (0, blk(i, j), 0)),
        ],
        out_shape=[
            jax.ShapeDtypeStruct((m, B_WIDTH), BF16),
            jax.ShapeDtypeStruct((batch, nb * BLOCK, SWA_OUT), BF16),
        ],
        compiler_params=_params(("arbitrary", "arbitrary")),
        name="inproj_b_swa",
    )(xn, w_in_t, z_a, z_a, z_a, z_a, z_a, z_a, z_a, q_norm.reshape(1, -1), k_norm.reshape(1, -1), sink_col)
    return z_b, o_a.reshape(m, SWA_OUT)


def _rope_rows(x, cos, sin, half):
    width = x.shape[-1]
    lane = lax.broadcasted_iota(jnp.int32, x.shape, 1)
    rot = jnp.where(lane < half, pltpu.roll(x, width - half, 1), pltpu.roll(x, half, 1))
    return x * cos + rot * sin


def _rope_key_norm(r, gain):
    return r * lax.rsqrt(jnp.sum(r * r, axis=-1, keepdims=True) * (1.0 / MLA_ROPE_DIM) + EPS) * gain


def _mla_up_body(cq0_ref, cq1_ref, ckv0_ref, ckv1_ref, kr_ref, qa_ref, kva_ref, wq_ref, wkv_ref, qn_ref, qr_ref,
                 kn_ref, krn_ref, cos_ref, sin_ref, q_ref, k_ref, v_ref):
    cos = cos_ref[...]
    sin = sin_ref[...]
    half = MLA_ROPE_DIM // 2
    scale = (MLA_NOPE_DIM + MLA_ROPE_DIM) ** -0.5 * LOG2_E
    qn_gain = qn_ref[...] * scale
    qr_gain = qr_ref[...] * scale
    c_q = jnp.concatenate([cq0_ref[...], cq1_ref[...]], axis=1).astype(F32)
    c_kv = jnp.concatenate([ckv0_ref[...], ckv1_ref[...]], axis=1).astype(F32)
    yq = _dot(_rms(c_q, qa_ref[...]).astype(BF16), wq_ref[...])
    ykv = _dot(_rms(c_kv, kva_ref[...]).astype(BF16), wkv_ref[...].astype(BF16))
    k_rope = _rope_rows(_rope_key_norm(kr_ref[...].astype(F32), krn_ref[...]), cos, sin, half).astype(BF16)
    for h in range(MLA_HEADS):
        lo = h * MLA_QK_PAD
        nope = _rms(yq[:, lo:lo + MLA_NOPE_DIM], qn_gain)
        r = _rope_rows(_rope_key_norm(yq[:, lo + MLA_NOPE_DIM:lo + MLA_QK_PAD], qr_gain), cos, sin, half)
        q_ref[:, lo:lo + MLA_NOPE_DIM] = nope.astype(BF16)
        q_ref[:, lo + MLA_NOPE_DIM:lo + MLA_QK_PAD] = r.astype(BF16)
        k_ref[:, lo:lo + MLA_NOPE_DIM] = _rms(ykv[:, lo:lo + MLA_NOPE_DIM], kn_ref[...]).astype(BF16)
        k_ref[:, lo + MLA_NOPE_DIM:lo + MLA_QK_PAD] = k_rope
        v_ref[:, h * MLA_V_DIM:(h + 1) * MLA_V_DIM] = ykv[:, lo + MLA_NOPE_DIM:lo + MLA_QK_PAD].astype(BF16)


def _mla_up(z_a, z_kr, qa_norm, kva_norm, w_uq_pad, w_ukv, qn_norm, qr_norm_pad, kn_norm, kr_norm_pad, cos_t, sin_t,
            layer, seq, *, tm=MLA_UP_ROW_TILE):
    m = z_a.shape[0]
    seq_tiles = seq // tm
    kv_w = MLA_NOPE_DIM + MLA_V_DIM
    half_rank = MLA_Q_RANK // 2

    def row(width, col):
        return pl.BlockSpec((tm, width), lambda i: (i, col))

    def vec(width):
        return pl.BlockSpec((1, width), lambda i: (0, 0))

    return pl.pallas_call(
        _mla_up_body,
        grid=(m // tm,),
        in_specs=[
            row(half_rank, A_MLA_CQ // half_rank), row(half_rank, A_MLA_CQ // half_rank + 1),
            row(half_rank, A_MLA_CKV // half_rank), row(half_rank, A_MLA_CKV // half_rank + 1),
            row(BLOCK, 0),
            vec(MLA_Q_RANK), vec(MLA_KV_RANK),
            pl.BlockSpec((None, MLA_Q_RANK, MLA_HEADS * MLA_QK_PAD), lambda i: (layer, 0, 0)),
            pl.BlockSpec((None, MLA_KV_RANK, MLA_HEADS * kv_w), lambda i: (layer, 0, 0)),
            vec(MLA_NOPE_DIM), vec(BLOCK), vec(MLA_NOPE_DIM), vec(BLOCK),
            pl.BlockSpec((tm, BLOCK), lambda i: (i % seq_tiles, 0)),
            pl.BlockSpec((tm, BLOCK), lambda i: (i % seq_tiles, 0)),
        ],
        out_specs=[row(MLA_HEADS * MLA_QK_PAD, 0), row(MLA_HEADS * MLA_QK_PAD, 0), row(MLA_OUT, 0)],
        out_shape=[
            jax.ShapeDtypeStruct((m, MLA_HEADS * MLA_QK_PAD), BF16),
            jax.ShapeDtypeStruct((m, MLA_HEADS * MLA_QK_PAD), BF16),
            jax.ShapeDtypeStruct((m, MLA_OUT), BF16),
        ],
        compiler_params=_params(("parallel",)),
        name="mla_up",
    )(z_a, z_a, z_a, z_a, z_kr, qa_norm.reshape(1, -1), kva_norm.reshape(1, -1), w_uq_pad, w_ukv,
      qn_norm.reshape(1, -1), qr_norm_pad, kn_norm.reshape(1, -1), kr_norm_pad, cos_t, sin_t)


def _mla_attn_body(q_ref, k_ref, v_ref, o_ref, m_ref, acc_ref, *, t, hb):
    i = pl.program_id(2)
    m_ref[...] = jnp.full(m_ref.shape, NEG_INF, F32)
    acc_ref[...] = jnp.zeros(acc_ref.shape, F32)

    def tile(blk, nblk, masked):
        width = nblk * t
        off = blk * t if isinstance(blk, int) else pl.multiple_of(blk * t, t)
        if masked:
            row = i * t + lax.broadcasted_iota(jnp.int32, (t, width), 0)
            col = blk * t + lax.broadcasted_iota(jnp.int32, (t, width), 1)
            ok = (col <= row) & (col >= PAD)
        for h in range(hb):
            q = q_ref[:, h * MLA_QK_PAD:(h + 1) * MLA_QK_PAD]
            kb = k_ref[pl.ds(off, width), h * MLA_QK_PAD:(h + 1) * MLA_QK_PAD]
            vb = v_ref[pl.ds(off, width), h * MLA_V_DIM:(h + 1) * MLA_V_DIM]
            s = _dot_nt(q, kb)
            if masked:
                s = jnp.where(ok, s, NEG_INF)
            m_run = m_ref[h]
            m_new = jnp.maximum(m_run, jnp.max(s, axis=-1, keepdims=True))
            alpha = jnp.exp2(m_run - m_new)
            p = jnp.concatenate(
                [jnp.exp2(s[:, c * BLOCK:(c + 1) * BLOCK] - m_new) for c in range(width // BLOCK)], axis=1)
            v_aug = jnp.concatenate([vb, jnp.ones_like(vb)], axis=1)
            acc_ref[h] = jnp.concatenate([alpha, alpha], axis=1) * acc_ref[h] + _dot(p.astype(BF16), v_aug)
            m_ref[h] = m_new

    @pl.when(i >= 2)
    def _():
        tile(0, 2, True)

    def body(jj, carry):
        tile(2 * jj, 2, False)
        return carry

    lax.fori_loop(1, i // 2, body, 0)

    @pl.when(i % 2 == 1)
    def _():
        tile(i - 1, 2, True)

    @pl.when(i % 2 == 0)
    def _():
        tile(i, 1, True)

    for h in range(hb):
        acc = acc_ref[h]
        o_ref[:, h * MLA_V_DIM:(h + 1) * MLA_V_DIM] = (acc[:, :MLA_V_DIM] / acc[:, MLA_V_DIM:]).astype(BF16)


def _mla_attn(q, k, v, seq, batch, *, t=MLA_BLOCK, hb=MLA_HEADS):
    m = q.shape[0]
    nq = seq // t
    return pl.pallas_call(
        functools.partial(_mla_attn_body, t=t, hb=hb),
        grid=(batch, MLA_HEADS // hb, nq),
        in_specs=[
            pl.BlockSpec((t, hb * MLA_QK_PAD), lambda b, h, i: (b * nq + i, h)),
            pl.BlockSpec((seq, hb * MLA_QK_PAD), lambda b, h, i: (b, h), pipeline_mode=pl.Buffered(1)),
            pl.BlockSpec((seq, hb * MLA_V_DIM), lambda b, h, i: (b, h), pipeline_mode=pl.Buffered(1)),
        ],
        out_specs=pl.BlockSpec((t, hb * MLA_V_DIM), lambda b, h, i: (b * nq + i, h)),
        out_shape=jax.ShapeDtypeStruct((m, MLA_OUT), BF16),
        scratch_shapes=[pltpu.VMEM((hb, t, BLOCK), F32), pltpu.VMEM((hb, t, 2 * MLA_V_DIM), F32)],
        compiler_params=_params(("parallel", "parallel", "arbitrary")),
        name="mla_attn",
    )(q, k, v)


def _ret_body(q_ref, k_ref, v_ref, g_ref, cos_ref, sin_ref, dec_ref, zeta_ref, xi_ref, gn_ref, o_ref,
              state_ref, *, chunk_decay):
    n = pl.program_id(0)

    @pl.when(n == 0)
    def _():
        state_ref[...] = jnp.zeros_like(state_ref)

    cos = cos_ref[...]
    sin = sin_ref[...]
    valid = (n * BLOCK + lax.broadcasted_iota(jnp.int32, (BLOCK, 1), 0) >= PAD).astype(F32)
    for b in range(q_ref.shape[0]):
        for h in range(RET_HEADS):
            ks = slice(h * RET_KEY_DIM, (h + 1) * RET_KEY_DIM)
            vs = slice(h * RET_VAL_DIM, (h + 1) * RET_VAL_DIM)
            qh = _rope_rows(q_ref[b, :, ks].astype(F32), cos, sin, RET_KEY_DIM // 2)
            kh = (_rope_rows(k_ref[b, :, ks].astype(F32), cos, sin, RET_KEY_DIM // 2)
                  * (RET_KEY_DIM ** -0.5) * valid)
            vb = v_ref[b, :, vs]
            s = _dot_nt(qh.astype(BF16), kh.astype(BF16)) * dec_ref[h]
            inner = _dot(s.astype(BF16), vb)
            prev = state_ref[b, h]
            cross = _dot((qh * xi_ref[h]).astype(BF16), prev.astype(BF16))
            kz_t = (kh * zeta_ref[h]).T.astype(BF16)
            state_ref[b, h] = prev * chunk_decay[h] + _dot(kz_t, vb)
            o = inner + cross
            mu = jnp.mean(o, axis=-1, keepdims=True)
            d = o - mu
            y = d * lax.rsqrt(jnp.mean(d * d, axis=-1, keepdims=True) + EPS)
            gate = g_ref[b, :, vs].astype(F32)
            o_ref[b, :, vs] = (y * gn_ref[:, vs] * (gate * _sigmoid(gate))).astype(BF16)


def _retention(z, cos_t, sin_t, decay, zeta_b, xi_b, gn, chunk_decay, nb, batch):
    m = z.shape[0]
    z = z.reshape(batch, nb * BLOCK, z.shape[1])
    tab = pl.BlockSpec((RET_HEADS, BLOCK, BLOCK), lambda n: (0, 0, 0))

    def rows(width, col):
        return pl.BlockSpec((batch, BLOCK, width), lambda n: (0, n, col // width))

    out = pl.pallas_call(
        functools.partial(_ret_body, chunk_decay=chunk_decay),
        grid=(nb,),
        in_specs=[
            rows(RET_QK_W, B_RET_Q), rows(RET_QK_W, B_RET_K), rows(RET_OUT, B_RET_V), rows(RET_OUT, B_RET_G),
            pl.BlockSpec((BLOCK, BLOCK), lambda n: (n, 0)),
            pl.BlockSpec((BLOCK, BLOCK), lambda n: (n, 0)),
            tab, tab, tab,
            pl.BlockSpec((1, RET_OUT), lambda n: (0, 0)),
        ],
        out_specs=pl.BlockSpec((batch, BLOCK, RET_OUT), lambda n: (0, n, 0)),
        out_shape=jax.ShapeDtypeStruct((batch, nb * BLOCK, RET_OUT), BF16),
        scratch_shapes=[pltpu.VMEM((batch, RET_HEADS, RET_KEY_DIM, RET_VAL_DIM), F32)],
        compiler_params=_params(("arbitrary",)),
        name="retention",
    )(z, z, z, z, cos_t, sin_t, decay, zeta_b, xi_b, gn.reshape(1, -1))
    return out.reshape(m, RET_OUT)


def _merge_body(oa_ref, ob_ref, oc_ref, wa_ref, wb_ref, wc_ref, ga_ref, gb_ref, gc_ref, o_ref):
    def branch(o, w, g):
        return _sigmoid(g[...].astype(F32)) * _dot(o[...], w[...].astype(BF16))

    o_ref[...] = (branch(oa_ref, wa_ref, ga_ref) + branch(ob_ref, wb_ref, gb_ref)
                  + branch(oc_ref, wc_ref, gc_ref)).astype(BF16)


def _merge(z, o_a, o_b, o_c, w_a, w_b, w_c, layer, *, tm=ROW_TILE, tn=MERGE_COL_TILE):
    m = z.shape[0]
    nt = D_MODEL // tn

    def act(width):
        return pl.BlockSpec((tm, width), lambda i, j: (i, 0))

    def wgt(width):
        return pl.BlockSpec((None, width, tn), lambda i, j: (layer, 0, j))

    def gate(br):
        return pl.BlockSpec((tm, tn), lambda i, j: (i, B_GATE // tn + br * nt + j))

    return pl.pallas_call(
        _merge_body,
        grid=(m // tm, nt),
        in_specs=[act(SWA_OUT), act(MLA_OUT), act(RET_OUT), wgt(SWA_OUT), wgt(MLA_OUT), wgt(RET_OUT),
                  gate(0), gate(1), gate(2)],
        out_specs=pl.BlockSpec((tm, tn), lambda i, j: (i, j)),
        out_shape=jax.ShapeDtypeStruct((m, D_MODEL), BF16),
        compiler_params=_params(("parallel", "arbitrary")),
        name="merge",
    )(o_a, o_b, o_c, w_a, w_b, w_c, z, z, z)


def _outproj_body(x_ref, w_ref, r_ref, o_ref):
    o_ref[...] = r_ref[...] + _dot(x_ref[...], w_ref[...].astype(BF16))


def _outproj(x, w, res, layer, *, tm=ROW_TILE_WIDE, tn=OUTPROJ_COL_TILE):
    m = x.shape[0]
    return pl.pallas_call(
        _outproj_body,
        grid=(m // tm, D_MODEL // tn),
        in_specs=[
            pl.BlockSpec((tm, D_MODEL), lambda i, j: (i, 0)),
            pl.BlockSpec((None, D_MODEL, tn), lambda i, j: (layer, 0, j)),
            pl.BlockSpec((tm, tn), lambda i, j: (i, j)),
        ],
        out_specs=pl.BlockSpec((tm, tn), lambda i, j: (i, j)),
        out_shape=jax.ShapeDtypeStruct((m, D_MODEL), F32),
        compiler_params=_params(("parallel", "arbitrary")),
        name="outproj",
    )(x, w, res)


def _rope_tables(pos, dim, width):
    half = dim // 2
    inv_freq = ROPE_BASE ** (-jnp.arange(half, dtype=F32) / half)
    ang = pos[:, None] * inv_freq[None, :]
    cos = jnp.cos(ang)
    sin = jnp.sin(ang)
    fill = jnp.zeros((pos.shape[0], width - dim), F32)
    return (jnp.concatenate([cos, cos, fill], axis=1), jnp.concatenate([-sin, sin, fill], axis=1))


def kernel(x, meta_tokens, ffn1_norm, ffn1_w_gate, ffn1_w_up, ffn1_w_down, mix_norm, w_in, swa_q_norm, swa_k_norm, swa_sinks, mla_q_a_norm, mla_w_uq, mla_kv_a_norm, mla_w_ukv, mla_qn_norm, mla_qr_norm, mla_kn_norm, mla_kr_norm, ret_gn, w_br_swa, w_br_mla, w_br_ret, w_o, ffn2_norm, ffn2_w_gate, ffn2_w_up, ffn2_w_down):
    batch, seq_in, _ = x.shape
    seq = seq_in + BLOCK
    nb = seq // BLOCK
    assert x.shape[2] == D_MODEL and seq % BLOCK == 0 and seq % MLA_BLOCK == 0 and seq % MLA_UP_ROW_TILE == 0
    assert (batch * seq) % ROW_TILE_WIDE == 0 and seq_in % FFN_TM_KEEP == 0

    meta = jnp.broadcast_to(meta_tokens[None].astype(x.dtype), (batch, N_META, D_MODEL))
    h = jnp.concatenate([jnp.zeros((batch, PAD, D_MODEL), x.dtype), meta, x], axis=1)
    h = h.reshape(batch * seq, D_MODEL)
    pos = (jnp.arange(seq) - PAD).astype(F32)

    w_in_t = jnp.swapaxes(w_in, 1, 2)

    qk_dim = MLA_NOPE_DIM + MLA_ROPE_DIM
    w_uq_pad = jnp.pad(mla_w_uq.reshape(DEPTH, MLA_Q_RANK, MLA_HEADS, qk_dim),
                       ((0, 0), (0, 0), (0, 0), (0, MLA_QK_PAD - qk_dim)))
    w_uq_pad = w_uq_pad.reshape(DEPTH, MLA_Q_RANK, MLA_HEADS * MLA_QK_PAD).astype(BF16)
    rope_fill = jnp.zeros((DEPTH, BLOCK - MLA_ROPE_DIM), F32)
    qr_norm_pad = jnp.concatenate([mla_qr_norm, rope_fill], axis=1)
    kr_norm_pad = jnp.concatenate([mla_kr_norm, rope_fill], axis=1)

    cos_m, sin_m = _rope_tables(pos, MLA_ROPE_DIM, BLOCK)
    cos_r, sin_r = _rope_tables(pos, RET_KEY_DIM, BLOCK)

    log_gamma = jnp.log(1.0 - 2.0 ** (-5.0 - jnp.arange(RET_HEADS, dtype=F32)))
    idx = jnp.arange(BLOCK, dtype=F32)
    diff = idx[:, None] - idx[None, :]
    decay = jnp.where(diff[None] >= 0, jnp.exp(jnp.maximum(diff, 0.0)[None] * log_gamma[:, None, None]), 0.0)
    zeta = jnp.exp((BLOCK - 1.0 - idx)[None, :] * log_gamma[:, None])
    xi = jnp.exp((idx + 1.0)[None, :] * log_gamma[:, None])
    zeta_b = jnp.broadcast_to(zeta[:, :, None], (RET_HEADS, BLOCK, RET_KEY_DIM))
    xi_b = jnp.broadcast_to(xi[:, :, None], (RET_HEADS, BLOCK, RET_KEY_DIM))
    chunk_decay = tuple(float(np.exp(BLOCK * np.log(1.0 - 2.0 ** (-5.0 - hh)))) for hh in range(RET_HEADS))

    for l in range(DEPTH):
        h, hn = _ffn(h, ffn1_norm, ffn1_w_gate, ffn1_w_up, ffn1_w_down, mix_norm, l, True)
        z_a, z_kr = _inproj_a(hn, w_in_t, l)
        sink_col = jnp.repeat(swa_sinks[l].astype(F32).reshape(SWA_KV_HEADS, SWA_GROUP), BLOCK, axis=1)
        z_b, o_a = _inproj_b_swa(hn, w_in_t, z_a, swa_q_norm[l], swa_k_norm[l], sink_col[:, None, :], l, nb, batch)
        q, k, v = _mla_up(z_a, z_kr, mla_q_a_norm[l], mla_kv_a_norm[l], w_uq_pad, mla_w_ukv, mla_qn_norm[l],
                          qr_norm_pad[l:l + 1], mla_kn_norm[l], kr_norm_pad[l:l + 1], cos_m, sin_m, l, seq)
        o_b = _mla_attn(q, k, v, seq, batch)
        o_c = _retention(z_b, cos_r, sin_r, decay, zeta_b, xi_b, ret_gn[l], chunk_decay, nb, batch)
        merged = _merge(z_b, o_a, o_b, o_c, w_br_swa, w_br_mla, w_br_ret, l)
        h = _outproj(merged, w_o, h, l)
        last = l == DEPTH - 1
        h, _ = _ffn(h, ffn2_norm, ffn2_w_gate, ffn2_w_up, ffn2_w_down, ffn2_norm, l, False,
                    keep=(seq, BLOCK) if last else None)
    return h.reshape(batch, seq_in, D_MODEL)
```

```python
import functools

import jax
import jax.numpy as jnp
import numpy as np
from jax import lax
from jax.experimental import pallas as pl
from jax.experimental.pallas import tpu as pltpu

F32 = jnp.float32
BF16 = jnp.bfloat16

D_MODEL = 2048
DEPTH = 4
N_META = 16
BLOCK = 128
PAD = BLOCK - N_META
EPS = 1e-6
NEG_INF = -1e30
ROPE_BASE = 10000.0
HALF_STEP = 0.5
LOG2_E = 1.4426950408889634

SWA_HEADS = 16
SWA_KV_HEADS = 2
SWA_HEAD_DIM = 64
SWA_GROUP = SWA_HEADS // SWA_KV_HEADS
WINDOW = 128

MLA_HEADS = 8
MLA_Q_RANK = 512
MLA_KV_RANK = 512
MLA_NOPE_DIM = 128
MLA_ROPE_DIM = 64
MLA_V_DIM = 128
MLA_QK_PAD = 256

RET_HEADS = 4
RET_KEY_DIM = 128
RET_VAL_DIM = 256

D_FF = 5632
N_BRANCH = 3

SWA_OUT = SWA_HEADS * SWA_HEAD_DIM
SWA_KV_W = SWA_KV_HEADS * SWA_HEAD_DIM
MLA_OUT = MLA_HEADS * MLA_V_DIM
RET_QK_W = RET_HEADS * RET_KEY_DIM
RET_OUT = RET_HEADS * RET_VAL_DIM

A_SWA_Q = 0
A_SWA_K = A_SWA_Q + SWA_OUT
A_SWA_V = A_SWA_K + SWA_KV_W
A_MLA_CQ = A_SWA_V + SWA_KV_W
A_MLA_CKV = A_MLA_CQ + MLA_Q_RANK
IN_KR = A_MLA_CKV + MLA_KV_RANK
IN_B = IN_KR + MLA_ROPE_DIM
B_RET_Q = 0
B_RET_K = B_RET_Q + RET_QK_W
B_RET_V = B_RET_K + RET_QK_W
B_RET_G = B_RET_V + RET_OUT
B_GATE = B_RET_G + RET_OUT
B_WIDTH = B_GATE + N_BRANCH * D_MODEL
IN_WIDTH = IN_B + B_WIDTH

VMEM_LIMIT = 56 * 1024 * 1024
VMEM_LIMIT_FFN = 60 * 1024 * 1024

ROW_TILE = 1056
ROW_TILE_WIDE = 2112
MLA_UP_ROW_TILE = 528
INPROJ_COL_TILE = 768
MERGE_COL_TILE = 512
OUTPROJ_COL_TILE = 512
MLA_BLOCK = 384


def _params(semantics, vmem_limit=VMEM_LIMIT):
    return pltpu.CompilerParams(dimension_semantics=semantics, vmem_limit_bytes=vmem_limit)


def _rms(x, g):
    return x * lax.rsqrt(jnp.mean(x * x, axis=-1, keepdims=True) + EPS) * g


def _sigmoid(x):
    return 0.5 * jnp.tanh(0.5 * x) + 0.5


def _dot(a, b):
    return jnp.dot(a, b, preferred_element_type=F32)


def _dot_nt(a, b):
    return lax.dot_general(a, b, (((1,), (1,)), ((), ())), preferred_element_type=F32)


FFN_TM = ROW_TILE
FFN_TM_KEEP = 1024
FFN_TF_HEAD = 256
FFN_TF_TAIL = 512


def _ffn_start(x_ref, g_ref, o_ref, xn_ref):
    x = x_ref[...]
    xn_ref[...] = _rms(x, g_ref[...]).astype(BF16)
    o_ref[...] = x


def _ffn_accumulate(xn_ref, wg, wu, wd, o_ref):
    xn = xn_ref[...]
    gate = _dot(xn, wg)
    up = _dot(xn, wu)
    act = (gate * _sigmoid(gate) * up).astype(BF16)
    o_ref[...] += HALF_STEP * _dot(act, wd)


def _ffn_head_body(x_ref, g_ref, wg_ref, wu_ref, wd_ref, g2_ref, o_ref, wgb_ref, wub_ref, wdb_ref, hn_ref, *,
                   emit_norm):
    f = pl.program_id(0)

    @pl.when(f == 0)
    def _():
        _ffn_start(x_ref, g_ref, o_ref, hn_ref)

    wg = wg_ref[...].astype(BF16)
    wu = wu_ref[...].astype(BF16)
    wd = wd_ref[...].astype(BF16)
    wgb_ref[...] = wg
    wub_ref[...] = wu
    wdb_ref[...] = wd
    _ffn_accumulate(hn_ref, wg, wu, wd, o_ref)

    if emit_norm:
        @pl.when(f == pl.num_programs(0) - 1)
        def _():
            hn_ref[...] = _rms(o_ref[...], g2_ref[...]).astype(BF16)


def _ffn_tail_body(x_ref, g_ref, wg_ref, wu_ref, wd_ref, g2_ref, head_ref, head_n_ref, o_ref, hn_ref, *, emit_norm):
    del head_ref, head_n_ref
    f = pl.program_id(1)

    @pl.when(f == 0)
    def _():
        _ffn_start(x_ref, g_ref, o_ref, hn_ref)

    _ffn_accumulate(hn_ref, wg_ref[...], wu_ref[...], wd_ref[...], o_ref)

    if emit_norm:
        @pl.when(f == pl.num_programs(1) - 1)
        def _():
            hn_ref[...] = _rms(o_ref[...], g2_ref[...]).astype(BF16)


def _ffn(h, gain, w_gate, w_up, w_down, next_gain, layer, emit_norm, keep=None):
    m = h.shape[0]
    tfh, tft = FFN_TF_HEAD, FFN_TF_TAIL
    if keep is None:
        tm, n_tiles = FFN_TM, m // FFN_TM

        def x_row(j):
            return j * tm
    else:
        seq, lead = keep
        tm = FFN_TM_KEEP
        per_seq = (seq - lead) // tm
        n_tiles = (m // seq) * per_seq

        def x_row(j):
            return (j // per_seq) * seq + lead + (j % per_seq) * tm

    m_out = n_tiles * tm
    x_block = (pl.Element(tm), pl.Element(D_MODEL))
    gain = gain.reshape(DEPTH, 1, D_MODEL)
    next_gain = next_gain.reshape(DEPTH, 1, D_MODEL)
    head, wg_b, wu_b, wd_b, head_n = pl.pallas_call(
        functools.partial(_ffn_head_body, emit_norm=emit_norm),
        grid=(D_FF // tfh,),
        in_specs=[
            pl.BlockSpec(x_block, lambda f: (x_row(0), 0), pipeline_mode=pl.Buffered(1)),
            pl.BlockSpec((None, 1, D_MODEL), lambda f: (layer, 0, 0)),
            pl.BlockSpec((None, D_MODEL, tfh), lambda f: (layer, 0, f)),
            pl.BlockSpec((None, D_MODEL, tfh), lambda f: (layer, 0, f)),
            pl.BlockSpec((None, tfh, D_MODEL), lambda f: (layer, f, 0)),
            pl.BlockSpec((None, 1, D_MODEL), lambda f: (layer, 0, 0)),
        ],
        out_specs=[
            pl.BlockSpec((tm, D_MODEL), lambda f: (0, 0)),
            pl.BlockSpec((D_MODEL, tfh), lambda f: (0, f)),
            pl.BlockSpec((D_MODEL, tfh), lambda f: (0, f)),
            pl.BlockSpec((tfh, D_MODEL), lambda f: (f, 0)),
            pl.BlockSpec((tm, D_MODEL), lambda f: (0, 0)),
        ],
        out_shape=[
            jax.ShapeDtypeStruct((m_out, D_MODEL), F32),
            jax.ShapeDtypeStruct((D_MODEL, D_FF), BF16),
            jax.ShapeDtypeStruct((D_MODEL, D_FF), BF16),
            jax.ShapeDtypeStruct((D_FF, D_MODEL), BF16),
            jax.ShapeDtypeStruct((m_out, D_MODEL), BF16),
        ],
        compiler_params=_params(("arbitrary",), VMEM_LIMIT_FFN),
        name="ffn_head",
    )(h, gain, w_gate, w_up, w_down, next_gain)
    return pl.pallas_call(
        functools.partial(_ffn_tail_body, emit_norm=emit_norm),
        grid=(n_tiles - 1, D_FF // tft),
        in_specs=[
            pl.BlockSpec(x_block, lambda i, f: (pl.multiple_of(x_row(i + 1), 8), 0)),
            pl.BlockSpec((None, 1, D_MODEL), lambda i, f: (layer, 0, 0)),
            pl.BlockSpec((D_MODEL, tft), lambda i, f: (0, f)),
            pl.BlockSpec((D_MODEL, tft), lambda i, f: (0, f)),
            pl.BlockSpec((tft, D_MODEL), lambda i, f: (f, 0)),
            pl.BlockSpec((None, 1, D_MODEL), lambda i, f: (layer, 0, 0)),
            pl.BlockSpec(memory_space=pl.ANY),
            pl.BlockSpec(memory_space=pl.ANY),
        ],
        out_specs=[
            pl.BlockSpec((tm, D_MODEL), lambda i, f: (i + 1, 0)),
            pl.BlockSpec((tm, D_MODEL), lambda i, f: (i + 1, 0)),
        ],
        out_shape=[
            jax.ShapeDtypeStruct((m_out, D_MODEL), F32),
            jax.ShapeDtypeStruct((m_out, D_MODEL), BF16),
        ],
        input_output_aliases={6: 0, 7: 1},
        compiler_params=_params(("parallel", "arbitrary"), VMEM_LIMIT_FFN),
        name="ffn_tail",
    )(h, gain, wg_b, wu_b, wd_b, next_gain, head, head_n)


def _inproj_a_body(xn_ref, w_ref, wkr_ref, za_ref, zk_ref):
    @pl.when(pl.program_id(1) == 0)
    def _():
        kr = _dot_nt(xn_ref[...], wkr_ref[...].astype(BF16))
        zk_ref[...] = jnp.concatenate([kr, jnp.zeros_like(kr)], axis=1).astype(BF16)

    za_ref[...] = _dot_nt(xn_ref[...], w_ref[...].astype(BF16)).astype(BF16)


def _inproj_a(xn, w_in_t, layer, *, tm=ROW_TILE_WIDE, tn=INPROJ_COL_TILE):
    m = xn.shape[0]
    return pl.pallas_call(
        _inproj_a_body,
        grid=(m // tm, IN_KR // tn),
        in_specs=[
            pl.BlockSpec((tm, D_MODEL), lambda i, j: (i, 0)),
            pl.BlockSpec((None, tn, D_MODEL), lambda i, j: (layer, j, 0)),
            pl.BlockSpec((None, MLA_ROPE_DIM, D_MODEL), lambda i, j: (layer, IN_KR // MLA_ROPE_DIM, 0)),
        ],
        out_specs=[
            pl.BlockSpec((tm, tn), lambda i, j: (i, j)),
            pl.BlockSpec((tm, BLOCK), lambda i, j: (i, 0)),
        ],
        out_shape=[
            jax.ShapeDtypeStruct((m, IN_KR), BF16),
            jax.ShapeDtypeStruct((m, BLOCK), BF16),
        ],
        compiler_params=_params(("parallel", "arbitrary")),
        name="inproj_a",
    )(xn, w_in_t, w_in_t)


def _swa_step(n, q_ref, kc_ref, kp_ref, km_ref, vc_ref, vp_ref, vm_ref, qn_ref, kn_ref, sink_ref, o_ref):
    j = lax.broadcasted_iota(jnp.int32, (BLOCK, BLOCK), 0)
    i_loc = lax.broadcasted_iota(jnp.int32, (BLOCK, BLOCK), 1)
    ok_cur = (j <= i_loc) & (n * BLOCK + j >= PAD)
    ok_prev = (j > i_loc) & ((n - 1) * BLOCK + j >= PAD)
    ok_meta = (j >= PAD) & (n * BLOCK + i_loc - j >= WINDOW)

    def mask_rows(s_blk, ok):
        return jnp.concatenate(
            [jnp.where(ok, s_blk[:, h * BLOCK:(h + 1) * BLOCK], NEG_INF) for h in range(SWA_GROUP)], axis=1)

    k_gain = kn_ref[...] * qn_ref[...] * (SWA_HEAD_DIM ** -0.5 * LOG2_E)
    for b in range(q_ref.shape[0]):
        _swa_block(b, q_ref, kc_ref, kp_ref, km_ref, vc_ref, vp_ref, vm_ref, sink_ref, o_ref, k_gain,
                   (ok_meta, ok_prev, ok_cur), mask_rows)


def _swa_block(b, q_ref, kc_ref, kp_ref, km_ref, vc_ref, vp_ref, vm_ref, sink_ref, o_ref, k_gain, oks, mask_rows):
    ok_meta, ok_prev, ok_cur = oks
    q_t = [q_ref[b, :, a * BLOCK:(a + 1) * BLOCK].astype(F32).T for a in range(SWA_HEADS // 2)]
    v_t = jnp.concatenate([r[b].astype(F32).T for r in (vm_ref, vp_ref, vc_ref)], axis=1)

    out_t = []
    for g in range(SWA_KV_HEADS):
        lo = g * SWA_HEAD_DIM
        hi = lo + SWA_HEAD_DIM
        heads = [g * SWA_GROUP + h for h in range(SWA_GROUP)]
        qt = jnp.concatenate(
            [q_t[hd // 2][(hd % 2) * SWA_HEAD_DIM:(hd % 2 + 1) * SWA_HEAD_DIM, :] for hd in heads], axis=1)
        q_rinv = lax.rsqrt(jnp.sum(qt * qt, axis=0, keepdims=True) * (1.0 / SWA_HEAD_DIM) + EPS)
        k_all = jnp.concatenate(
            [_rms(r[b, :, lo:hi].astype(F32), k_gain) for r in (km_ref, kp_ref, kc_ref)], axis=0).astype(BF16)
        s = _dot(k_all, qt.astype(BF16)) * q_rinv
        s = jnp.concatenate([mask_rows(s[:BLOCK], ok_meta), mask_rows(s[BLOCK:2 * BLOCK], ok_prev),
                             mask_rows(s[2 * BLOCK:], ok_cur)], axis=0)
        sink = sink_ref[g] * LOG2_E
        mx = jnp.maximum(jnp.max(s, axis=0, keepdims=True), sink)
        p = jnp.exp2(s - mx)
        den = jnp.sum(p, axis=0, keepdims=True) + jnp.exp2(sink - mx)
        o_t = _dot(v_t[lo:hi].astype(BF16), p.astype(BF16)) / den
        out_t.extend(o_t[:, h * BLOCK:(h + 1) * BLOCK] for h in range(SWA_GROUP))
    for a in range(SWA_HEADS // 2):
        pair = jnp.concatenate([out_t[2 * a], out_t[2 * a + 1]], axis=0)
        o_ref[b, :, a * BLOCK:(a + 1) * BLOCK] = pair.T.astype(BF16)


def _inproj_b_swa_body(xn_ref, w_ref, q_ref, kc_ref, kp_ref, km_ref, vc_ref, vp_ref, vm_ref, qn_ref, kn_ref,
                       sink_ref, zb_ref, oa_ref, *, n_col, nb):
    step = pl.program_id(0) * n_col + pl.program_id(1)

    def project():
        zb_ref[...] = _dot_nt(xn_ref[...], w_ref[0].astype(BF16)).astype(BF16)

    @pl.when(step < nb)
    def _():
        project()
        _swa_step(step, q_ref, kc_ref, kp_ref, km_ref, vc_ref, vp_ref, vm_ref, qn_ref, kn_ref, sink_ref, oa_ref)

    @pl.when(step >= nb)
    def _():
        project()


def _inproj_b_swa(xn, w_in_t, z_a, q_norm, k_norm, sink_col, layer, nb, batch, *, tm=ROW_TILE_WIDE,
                  tn=INPROJ_COL_TILE):
    m = xn.shape[0]
    n_col = B_WIDTH // tn
    assert (m // tm) * n_col >= nb
    z_a = z_a.reshape(batch, nb * BLOCK, z_a.shape[1])
    kcol = A_SWA_K // SWA_KV_W
    vcol = A_SWA_V // SWA_KV_W

    def blk(i, j):
        return jnp.minimum(i * n_col + j, nb - 1)

    def cur(c):
        return pl.BlockSpec((batch, BLOCK, SWA_KV_W), lambda i, j: (0, blk(i, j), c))

    def prev(c):
        return pl.BlockSpec((batch, BLOCK, SWA_KV_W), lambda i, j: (0, jnp.maximum(blk(i, j) - 1, 0), c))

    def meta(c):
        return pl.BlockSpec((batch, BLOCK, SWA_KV_W), lambda i, j: (0, 0, c))

    z_b, o_a = pl.pallas_call(
        functools.partial(_inproj_b_swa_body, n_col=n_col, nb=nb),
        grid=(m // tm, n_col),
        in_specs=[
            pl.BlockSpec((tm, D_MODEL), lambda i, j: (i, 0)),
            pl.BlockSpec((pl.Element(1), pl.Element(tn), pl.Element(D_MODEL)),
                         lambda i, j: (layer, pl.multiple_of(IN_B + j * tn, 8), 0)),
            pl.BlockSpec((batch, BLOCK, SWA_OUT), lambda i, j: (0, blk(i, j), A_SWA_Q // SWA_OUT)),
            cur(kcol), prev(kcol), meta(kcol),
            cur(vcol), prev(vcol), meta(vcol),
            pl.BlockSpec((1, SWA_HEAD_DIM), lambda i, j: (0, 0)),
            pl.BlockSpec((1, SWA_HEAD_DIM), lambda i, j: (0, 0)),
            pl.BlockSpec((SWA_KV_HEADS, 1, SWA_GROUP * BLOCK), lambda i, j: (0, 0, 0)),
        ],
        out_specs=[
            pl.BlockSpec((tm, tn), lambda i, j: (i, j)),
            pl.BlockSpec((batch, BLOCK, SWA_OUT), lambda i, j: (0, blk(i, j), 0)),
        ],
        out_shape=[
            jax.ShapeDtypeStruct((m, B_WIDTH), BF16),
            jax.ShapeDtypeStruct((batch, nb * BLOCK, SWA_OUT), BF16),
        ],
        compiler_params=_params(("arbitrary", "arbitrary")),
        name="inproj_b_swa",
    )(xn, w_in_t, z_a, z_a, z_a, z_a, z_a, z_a, z_a, q_norm.reshape(1, -1), k_norm.reshape(1, -1), sink_col)
    return z_b, o_a.reshape(m, SWA_OUT)


def _rope_rows(x, cos, sin, half):
    width = x.shape[-1]
    lane = lax.broadcasted_iota(jnp.int32, x.shape, 1)
    rot = jnp.where(lane < half, pltpu.roll(x, width - half, 1), pltpu.roll(x, half, 1))
    return x * cos + rot * sin


def _rope_key_norm(r, gain):
    return r * lax.rsqrt(jnp.sum(r * r, axis=-1, keepdims=True) * (1.0 / MLA_ROPE_DIM) + EPS) * gain


def _mla_up_body(cq0_ref, cq1_ref, ckv0_ref, ckv1_ref, kr_ref, qa_ref, kva_ref, wq_ref, wkv_ref, qn_ref, qr_ref,
                 kn_ref, krn_ref, cos_ref, sin_ref, q_ref, k_ref, v_ref):
    cos = cos_ref[...]
    sin = sin_ref[...]
    half = MLA_ROPE_DIM // 2
    scale = (MLA_NOPE_DIM + MLA_ROPE_DIM) ** -0.5 * LOG2_E
    qn_gain = qn_ref[...] * scale
    qr_gain = qr_ref[...] * scale
    c_q = jnp.concatenate([cq0_ref[...], cq1_ref[...]], axis=1).astype(F32)
    c_kv = jnp.concatenate([ckv0_ref[...], ckv1_ref[...]], axis=1).astype(F32)
    yq = _dot(_rms(c_q, qa_ref[...]).astype(BF16), wq_ref[...])
    ykv = _dot(_rms(c_kv, kva_ref[...]).astype(BF16), wkv_ref[...].astype(BF16))
    k_rope = _rope_rows(_rope_key_norm(kr_ref[...].astype(F32), krn_ref[...]), cos, sin, half).astype(BF16)
    for h in range(MLA_HEADS):
        lo = h * MLA_QK_PAD
        nope = _rms(yq[:, lo:lo + MLA_NOPE_DIM], qn_gain)
        r = _rope_rows(_rope_key_norm(yq[:, lo + MLA_NOPE_DIM:lo + MLA_QK_PAD], qr_gain), cos, sin, half)
        q_ref[:, lo:lo + MLA_NOPE_DIM] = nope.astype(BF16)
        q_ref[:, lo + MLA_NOPE_DIM:lo + MLA_QK_PAD] = r.astype(BF16)
        k_ref[:, lo:lo + MLA_NOPE_DIM] = _rms(ykv[:, lo:lo + MLA_NOPE_DIM], kn_ref[...]).astype(BF16)
        k_ref[:, lo + MLA_NOPE_DIM:lo + MLA_QK_PAD] = k_rope
        v_ref[:, h * MLA_V_DIM:(h + 1) * MLA_V_DIM] = ykv[:, lo + MLA_NOPE_DIM:lo + MLA_QK_PAD].astype(BF16)


def _mla_up(z_a, z_kr, qa_norm, kva_norm, w_uq_pad, w_ukv, qn_norm, qr_norm_pad, kn_norm, kr_norm_pad, cos_t, sin_t,
            layer, seq, *, tm=MLA_UP_ROW_TILE):
    m = z_a.shape[0]
    seq_tiles = seq // tm
    kv_w = MLA_NOPE_DIM + MLA_V_DIM
    half_rank = MLA_Q_RANK // 2

    def row(width, col):
        return pl.BlockSpec((tm, width), lambda i: (i, col))

    def vec(width):
        return pl.BlockSpec((1, width), lambda i: (0, 0))

    return pl.pallas_call(
        _mla_up_body,
        grid=(m // tm,),
        in_specs=[
            row(half_rank, A_MLA_CQ // half_rank), row(half_rank, A_MLA_CQ // half_rank + 1),
            row(half_rank, A_MLA_CKV // half_rank), row(half_rank, A_MLA_CKV // half_rank + 1),
            row(BLOCK, 0),
            vec(MLA_Q_RANK), vec(MLA_KV_RANK),
            pl.BlockSpec((None, MLA_Q_RANK, MLA_HEADS * MLA_QK_PAD), lambda i: (layer, 0, 0)),
            pl.BlockSpec((None, MLA_KV_RANK, MLA_HEADS * kv_w), lambda i: (layer, 0, 0)),
            vec(MLA_NOPE_DIM), vec(BLOCK), vec(MLA_NOPE_DIM), vec(BLOCK),
            pl.BlockSpec((tm, BLOCK), lambda i: (i % seq_tiles, 0)),
            pl.BlockSpec((tm, BLOCK), lambda i: (i % seq_tiles, 0)),
        ],
        out_specs=[row(MLA_HEADS * MLA_QK_PAD, 0), row(MLA_HEADS * MLA_QK_PAD, 0), row(MLA_OUT, 0)],
        out_shape=[
            jax.ShapeDtypeStruct((m, MLA_HEADS * MLA_QK_PAD), BF16),
            jax.ShapeDtypeStruct((m, MLA_HEADS * MLA_QK_PAD), BF16),
            jax.ShapeDtypeStruct((m, MLA_OUT), BF16),
        ],
        compiler_params=_params(("parallel",)),
        name="mla_up",
    )(z_a, z_a, z_a, z_a, z_kr, qa_norm.reshape(1, -1), kva_norm.reshape(1, -1), w_uq_pad, w_ukv,
      qn_norm.reshape(1, -1), qr_norm_pad, kn_norm.reshape(1, -1), kr_norm_pad, cos_t, sin_t)


def _mla_attn_body(q_ref, k_ref, v_ref, o_ref, m_ref, acc_ref, *, t, hb):
    i = pl.program_id(2)
    m_ref[...] = jnp.full(m_ref.shape, NEG_INF, F32)
    acc_ref[...] = jnp.zeros(acc_ref.shape, F32)

    def tile(blk, nblk, masked):
        width = nblk * t
        off = blk * t if isinstance(blk, int) else pl.multiple_of(blk * t, t)
        if masked:
            row = i * t + lax.broadcasted_iota(jnp.int32, (t, width), 0)
            col = blk * t + lax.broadcasted_iota(jnp.int32, (t, width), 1)
            ok = (col <= row) & (col >= PAD)
        for h in range(hb):
            q = q_ref[:, h * MLA_QK_PAD:(h + 1) * MLA_QK_PAD]
            kb = k_ref[pl.ds(off, width), h * MLA_QK_PAD:(h + 1) * MLA_QK_PAD]
            vb = v_ref[pl.ds(off, width), h * MLA_V_DIM:(h + 1) * MLA_V_DIM]
            s = _dot_nt(q, kb)
            if masked:
                s = jnp.where(ok, s, NEG_INF)
            m_run = m_ref[h]
            m_new = jnp.maximum(m_run, jnp.max(s, axis=-1, keepdims=True))
            alpha = jnp.exp2(m_run - m_new)
            p = jnp.concatenate(
                [jnp.exp2(s[:, c * BLOCK:(c + 1) * BLOCK] - m_new) for c in range(width // BLOCK)], axis=1)
            v_aug = jnp.concatenate([vb, jnp.ones_like(vb)], axis=1)
            acc_ref[h] = jnp.concatenate([alpha, alpha], axis=1) * acc_ref[h] + _dot(p.astype(BF16), v_aug)
            m_ref[h] = m_new

    @pl.when(i >= 2)
    def _():
        tile(0, 2, True)

    def body(jj, carry):
        tile(2 * jj, 2, False)
        return carry

    lax.fori_loop(1, i // 2, body, 0)

    @pl.when(i % 2 == 1)
    def _():
        tile(i - 1, 2, True)

    @pl.when(i % 2 == 0)
    def _():
        tile(i, 1, True)

    for h in range(hb):
        acc = acc_ref[h]
        o_ref[:, h * MLA_V_DIM:(h + 1) * MLA_V_DIM] = (acc[:, :MLA_V_DIM] / acc[:, MLA_V_DIM:]).astype(BF16)


def _mla_attn(q, k, v, seq, batch, *, t=MLA_BLOCK, hb=MLA_HEADS):
    m = q.shape[0]
    nq = seq // t
    return pl.pallas_call(
        functools.partial(_mla_attn_body, t=t, hb=hb),
        grid=(batch, MLA_HEADS // hb, nq),
        in_specs=[
            pl.BlockSpec((t, hb * MLA_QK_PAD), lambda b, h, i: (b * nq + i, h)),
            pl.BlockSpec((seq, hb * MLA_QK_PAD), lambda b, h, i: (b, h), pipeline_mode=pl.Buffered(1)),
            pl.BlockSpec((seq, hb * MLA_V_DIM), lambda b, h, i: (b, h), pipeline_mode=pl.Buffered(1)),
        ],
        out_specs=pl.BlockSpec((t, hb * MLA_V_DIM), lambda b, h, i: (b * nq + i, h)),
        out_shape=jax.ShapeDtypeStruct((m, MLA_OUT), BF16),
        scratch_shapes=[pltpu.VMEM((hb, t, BLOCK), F32), pltpu.VMEM((hb, t, 2 * MLA_V_DIM), F32)],
        compiler_params=_params(("parallel", "parallel", "arbitrary")),
        name="mla_attn",
    )(q, k, v)


def _ret_body(q_ref, k_ref, v_ref, g_ref, cos_ref, sin_ref, dec_ref, zeta_ref, xi_ref, gn_ref, o_ref,
              state_ref, *, chunk_decay):
    n = pl.program_id(0)

    @pl.when(n == 0)
    def _():
        state_ref[...] = jnp.zeros_like(state_ref)

    cos = cos_ref[...]
    sin = sin_ref[...]
    valid = (n * BLOCK + lax.broadcasted_iota(jnp.int32, (BLOCK, 1), 0) >= PAD).astype(F32)
    for b in range(q_ref.shape[0]):
        for h in range(RET_HEADS):
            ks = slice(h * RET_KEY_DIM, (h + 1) * RET_KEY_DIM)
            vs = slice(h * RET_VAL_DIM, (h + 1) * RET_VAL_DIM)
            qh = _rope_rows(q_ref[b, :, ks].astype(F32), cos, sin, RET_KEY_DIM // 2)
            kh = (_rope_rows(k_ref[b, :, ks].astype(F32), cos, sin, RET_KEY_DIM // 2)
                  * (RET_KEY_DIM ** -0.5) * valid)
            vb = v_ref[b, :, vs]
            s = _dot_nt(qh.astype(BF16), kh.astype(BF16)) * dec_ref[h]
            inner = _dot(s.astype(BF16), vb)
            prev = state_ref[b, h]
            cross = _dot((qh * xi_ref[h]).astype(BF16), prev.astype(BF16))
            kz_t = (kh * zeta_ref[h]).T.astype(BF16)
            state_ref[b, h] = prev * chunk_decay[h] + _dot(kz_t, vb)
            o = inner + cross
            mu = jnp.mean(o, axis=-1, keepdims=True)
            d = o - mu
            y = d * lax.rsqrt(jnp.mean(d * d, axis=-1, keepdims=True) + EPS)
            gate = g_ref[b, :, vs].astype(F32)
            o_ref[b, :, vs] = (y * gn_ref[:, vs] * (gate * _sigmoid(gate))).astype(BF16)


def _retention(z, cos_t, sin_t, decay, zeta_b, xi_b, gn, chunk_decay, nb, batch):
    m = z.shape[0]
    z = z.reshape(batch, nb * BLOCK, z.shape[1])
    tab = pl.BlockSpec((RET_HEADS, BLOCK, BLOCK), lambda n: (0, 0, 0))

    def rows(width, col):
        return pl.BlockSpec((batch, BLOCK, width), lambda n: (0, n, col // width))

    out = pl.pallas_call(
        functools.partial(_ret_body, chunk_decay=chunk_decay),
        grid=(nb,),
        in_specs=[
            rows(RET_QK_W, B_RET_Q), rows(RET_QK_W, B_RET_K), rows(RET_OUT, B_RET_V), rows(RET_OUT, B_RET_G),
            pl.BlockSpec((BLOCK, BLOCK), lambda n: (n, 0)),
            pl.BlockSpec((BLOCK, BLOCK), lambda n: (n, 0)),
            tab, tab, tab,
            pl.BlockSpec((1, RET_OUT), lambda n: (0, 0)),
        ],
        out_specs=pl.BlockSpec((batch, BLOCK, RET_OUT), lambda n: (0, n, 0)),
        out_shape=jax.ShapeDtypeStruct((batch, nb * BLOCK, RET_OUT), BF16),
        scratch_shapes=[pltpu.VMEM((batch, RET_HEADS, RET_KEY_DIM, RET_VAL_DIM), F32)],
        compiler_params=_params(("arbitrary",)),
        name="retention",
    )(z, z, z, z, cos_t, sin_t, decay, zeta_b, xi_b, gn.reshape(1, -1))
    return out.reshape(m, RET_OUT)


def _merge_body(oa_ref, ob_ref, oc_ref, wa_ref, wb_ref, wc_ref, ga_ref, gb_ref, gc_ref, o_ref):
    def branch(o, w, g):
        return _sigmoid(g[...].astype(F32)) * _dot(o[...], w[...].astype(BF16))

    o_ref[...] = (branch(oa_ref, wa_ref, ga_ref) + branch(ob_ref, wb_ref, gb_ref)
                  + branch(oc_ref, wc_ref, gc_ref)).astype(BF16)


def _merge(z, o_a, o_b, o_c, w_a, w_b, w_c, layer, *, tm=ROW_TILE, tn=MERGE_COL_TILE):
    m = z.shape[0]
    nt = D_MODEL // tn

    def act(width):
        return pl.BlockSpec((tm, width), lambda i, j: (i, 0))

    def wgt(width):
        return pl.BlockSpec((None, width, tn), lambda i, j: (layer, 0, j))

    def gate(br):
        return pl.BlockSpec((tm, tn), lambda i, j: (i, B_GATE // tn + br * nt + j))

    return pl.pallas_call(
        _merge_body,
        grid=(m // tm, nt),
        in_specs=[act(SWA_OUT), act(MLA_OUT), act(RET_OUT), wgt(SWA_OUT), wgt(MLA_OUT), wgt(RET_OUT),
                  gate(0), gate(1), gate(2)],
        out_specs=pl.BlockSpec((tm, tn), lambda i, j: (i, j)),
        out_shape=jax.ShapeDtypeStruct((m, D_MODEL), BF16),
        compiler_params=_params(("parallel", "arbitrary")),
        name="merge",
    )(o_a, o_b, o_c, w_a, w_b, w_c, z, z, z)


def _outproj_body(x_ref, w_ref, r_ref, o_ref):
    o_ref[...] = r_ref[...] + _dot(x_ref[...], w_ref[...].astype(BF16))


def _outproj(x, w, res, layer, *, tm=ROW_TILE_WIDE, tn=OUTPROJ_COL_TILE):
    m = x.shape[0]
    return pl.pallas_call(
        _outproj_body,
        grid=(m // tm, D_MODEL // tn),
        in_specs=[
            pl.BlockSpec((tm, D_MODEL), lambda i, j: (i, 0)),
            pl.BlockSpec((None, D_MODEL, tn), lambda i, j: (layer, 0, j)),
            pl.BlockSpec((tm, tn), lambda i, j: (i, j)),
        ],
        out_specs=pl.BlockSpec((tm, tn), lambda i, j: (i, j)),
        out_shape=jax.ShapeDtypeStruct((m, D_MODEL), F32),
        compiler_params=_params(("parallel", "arbitrary")),
        name="outproj",
    )(x, w, res)


def _rope_tables(pos, dim, width):
    half = dim // 2
    inv_freq = ROPE_BASE ** (-jnp.arange(half, dtype=F32) / half)
    ang = pos[:, None] * inv_freq[None, :]
    cos = jnp.cos(ang)
    sin = jnp.sin(ang)
    fill = jnp.zeros((pos.shape[0], width - dim), F32)
    return (jnp.concatenate([cos, cos, fill], axis=1), jnp.concatenate([-sin, sin, fill], axis=1))


def kernel(x, meta_tokens, ffn1_norm, ffn1_w_gate, ffn1_w_up, ffn1_w_down, mix_norm, w_in, swa_q_norm, swa_k_norm, swa_sinks, mla_q_a_norm, mla_w_uq, mla_kv_a_norm, mla_w_ukv, mla_qn_norm, mla_qr_norm, mla_kn_norm, mla_kr_norm, ret_gn, w_br_swa, w_br_mla, w_br_ret, w_o, ffn2_norm, ffn2_w_gate, ffn2_w_up, ffn2_w_down):
    batch, seq_in, _ = x.shape
    seq = seq_in + BLOCK
    nb = seq // BLOCK
    assert x.shape[2] == D_MODEL and seq % BLOCK == 0 and seq % MLA_BLOCK == 0 and seq % MLA_UP_ROW_TILE == 0
    assert (batch * seq) % ROW_TILE_WIDE == 0 and seq_in % FFN_TM_KEEP == 0

    meta = jnp.broadcast_to(meta_tokens[None].astype(x.dtype), (batch, N_META, D_MODEL))
    h = jnp.concatenate([jnp.zeros((batch, PAD, D_MODEL), x.dtype), meta, x], axis=1)
    h = h.reshape(batch * seq, D_MODEL)
    pos = (jnp.arange(seq) - PAD).astype(F32)

    w_in_t = jnp.swapaxes(w_in, 1, 2)

    qk_dim = MLA_NOPE_DIM + MLA_ROPE_DIM
    w_uq_pad = jnp.pad(mla_w_uq.reshape(DEPTH, MLA_Q_RANK, MLA_HEADS, qk_dim),
                       ((0, 0), (0, 0), (0, 0), (0, MLA_QK_PAD - qk_dim)))
    w_uq_pad = w_uq_pad.reshape(DEPTH, MLA_Q_RANK, MLA_HEADS * MLA_QK_PAD).astype(BF16)
    rope_fill = jnp.zeros((DEPTH, BLOCK - MLA_ROPE_DIM), F32)
    qr_norm_pad = jnp.concatenate([mla_qr_norm, rope_fill], axis=1)
    kr_norm_pad = jnp.concatenate([mla_kr_norm, rope_fill], axis=1)

    cos_m, sin_m = _rope_tables(pos, MLA_ROPE_DIM, BLOCK)
    cos_r, sin_r = _rope_tables(pos, RET_KEY_DIM, BLOCK)

    log_gamma = jnp.log(1.0 - 2.0 ** (-5.0 - jnp.arange(RET_HEADS, dtype=F32)))
    idx = jnp.arange(BLOCK, dtype=F32)
    diff = idx[:, None] - idx[None, :]
    decay = jnp.where(diff[None] >= 0, jnp.exp(jnp.maximum(diff, 0.0)[None] * log_gamma[:, None, None]), 0.0)
    zeta = jnp.exp((BLOCK - 1.0 - idx)[None, :] * log_gamma[:, None])
    xi = jnp.exp((idx + 1.0)[None, :] * log_gamma[:, None])
    zeta_b = jnp.broadcast_to(zeta[:, :, None], (RET_HEADS, BLOCK, RET_KEY_DIM))
    xi_b = jnp.broadcast_to(xi[:, :, None], (RET_HEADS, BLOCK, RET_KEY_DIM))
    chunk_decay = tuple(float(np.exp(BLOCK * np.log(1.0 - 2.0 ** (-5.0 - hh)))) for hh in range(RET_HEADS))

    for l in range(DEPTH):
        h, hn = _ffn(h, ffn1_norm, ffn1_w_gate, ffn1_w_up, ffn1_w_down, mix_norm, l, True)
        z_a, z_kr = _inproj_a(hn, w_in_t, l)
        sink_col = jnp.repeat(swa_sinks[l].astype(F32).reshape(SWA_KV_HEADS, SWA_GROUP), BLOCK, axis=1)
        z_b, o_a = _inproj_b_swa(hn, w_in_t, z_a, swa_q_norm[l], swa_k_norm[l], sink_col[:, None, :], l, nb, batch)
        q, k, v = _mla_up(z_a, z_kr, mla_q_a_norm[l], mla_kv_a_norm[l], w_uq_pad, mla_w_ukv, mla_qn_norm[l],
                          qr_norm_pad[l:l + 1], mla_kn_norm[l], kr_norm_pad[l:l + 1], cos_m, sin_m, l, seq)
        o_b = _mla_attn(q, k, v, seq, batch)
        o_c = _retention(z_b, cos_r, sin_r, decay, zeta_b, xi_b, ret_gn[l], chunk_decay, nb, batch)
        merged = _merge(z_b, o_a, o_b, o_c, w_br_swa, w_br_mla, w_br_ret, l)
        h = _outproj(merged, w_o, h, l)
        last = l == DEPTH - 1
        h, _ = _ffn(h, ffn2_norm, ffn2_w_gate, ffn2_w_up, ffn2_w_down, ffn2_norm, l, False,
                    keep=(seq, BLOCK) if last else None)
    return h.reshape(batch, seq_in, D_MODEL)
```
